```python
import math
import jax
import jax.numpy as jnp
from jax import lax
import numpy as np

D_MODEL = 2048
BATCH = 2
SEQ = 16384
DEPTH = 4

GRID_W = 64
CTX_LEN = 256
D_MIX = D_MODEL
HEAD_DIM = 128
NA_WIDTH = D_MIX // 2
NA_HEADS = NA_WIDTH // HEAD_DIM
NA_KH = 8
NA_KW = 16
FN_WIDTH = D_MIX // 4
FN_GROUPS = 4
FN_GROUP_DIM = FN_WIDTH // FN_GROUPS
HY_WIDTH = D_MIX - NA_WIDTH - FN_WIDTH
HY_ORDER = 2
HY_BANDS = 16
HY_EMB = 1 + 2 * HY_BANDS
HY_FILTER_HIDDEN = 64
HY_FAST_DECAY = 0.3
HY_SLOW_DECAY = 1.5
HY_TARGET = 1e-2
HY_MIN_DECAY = math.log(HY_TARGET) / HY_SLOW_DECAY
HY_MAX_DECAY = math.log(HY_TARGET) / HY_FAST_DECAY
CONV_W = 3
D_FF = 11 * D_MODEL // 4
D_IN = 3 * NA_WIDTH + FN_WIDTH + (HY_ORDER + 1) * HY_WIDTH
N_MOD = 6
EPS = 1e-6

kernel_name = "hybrid_na_fourier_hyena_dit"


def rmsnorm(x, g):
    xf = x.astype(jnp.float32)
    y = xf * lax.rsqrt(jnp.mean(xf * xf, axis=-1, keepdims=True) + EPS)
    return (y * g.astype(jnp.float32)).astype(x.dtype)


def modulate(h, shift, scale):
    return h * (1 + scale) + shift


def dwconv_centred(x, w, b):
    L = x.shape[1]
    pad = CONV_W // 2
    xp = jnp.pad(x, ((0, 0), (pad, pad), (0, 0)))
    return sum(xp[:, j:j + L] * w[j] for j in range(CONV_W)) + b


def heads(t):
    return t.reshape(*t.shape[:-1], NA_HEADS, HEAD_DIM)


def split_projection(p):
    o1, o2, o3 = NA_WIDTH, 2 * NA_WIDTH, 3 * NA_WIDTH
    o4 = o3 + FN_WIDTH
    return p[..., :o1], p[..., o1:o2], p[..., o2:o3], p[..., o3:o4], p[..., o4:]


def neighbourhood_attention(q, k, v, kc, vc, rpb):
    B, L, H, Dh = q.shape
    rows = L // GRID_W
    kh = min(NA_KH, rows)
    n_loc = kh * NA_KW
    scale = Dh ** -0.5
    col = np.arange(GRID_W)
    col_start = np.clip(col - NA_KW // 2, 0, GRID_W - NA_KW)
    col_idx = col_start[:, None] + np.arange(NA_KW)[None, :]
    dcol = col_idx - col[:, None] + (NA_KW - 1)
    rpb_cols = rpb[:, :, dcol]
    kg = k.reshape(B, rows, GRID_W, H, Dh)
    vg = v.reshape(B, rows, GRID_W, H, Dh)
    q_rows = q.reshape(B, rows, GRID_W, H, Dh).transpose(1, 0, 2, 3, 4)

    def row_block(args):
        r, q_r = args
        r0 = jnp.clip(r - kh // 2, 0, rows - kh)
        k_rows = lax.dynamic_slice_in_dim(kg, r0, kh, axis=1)
        v_rows = lax.dynamic_slice_in_dim(vg, r0, kh, axis=1)
        k_win = jnp.take(k_rows, col_idx, axis=2)
        v_win = jnp.take(v_rows, col_idx, axis=2)
        drow = r0 + jnp.arange(kh) - r + (NA_KH - 1)
        bias = jnp.take(rpb_cols, drow, axis=1).transpose(0, 2, 1, 3)
        s_loc = jnp.einsum("bqhd,biqjhd->bhqij", q_r, k_win).astype(jnp.float32) * scale
        s_loc = s_loc + bias.astype(jnp.float32)
        s_ctx = jnp.einsum("bqhd,bchd->bhqc", q_r, kc).astype(jnp.float32) * scale
        s = jnp.concatenate([s_loc.reshape(B, H, GRID_W, n_loc), s_ctx], axis=-1)
        p = jax.nn.softmax(s, axis=-1).astype(v.dtype)
        p_loc = p[..., :n_loc].reshape(B, H, GRID_W, kh, NA_KW)
        return (jnp.einsum("bhqij,biqjhd->bqhd", p_loc, v_win)
                + jnp.einsum("bhqc,bchd->bqhd", p[..., n_loc:], vc))

    out = lax.map(row_block, (jnp.arange(rows), q_rows))
    return out.transpose(1, 0, 2, 3, 4).reshape(B, L, H * Dh)


def context_attention(qc, kc, vc):
    B, Lc, H, Dh = qc.shape
    s = jnp.einsum("bqhd,bkhd->bhqk", qc, kc).astype(jnp.float32) * (Dh ** -0.5)
    p = jax.nn.softmax(s, axis=-1).astype(vc.dtype)
    return jnp.einsum("bhqk,bkhd->bqhd", p, vc).reshape(B, Lc, H * Dh)


def fourier_mix(f):
    B, L, _ = f.shape
    fg = f.astype(jnp.float32).reshape(B, L, FN_GROUPS, FN_GROUP_DIM)
    y = jnp.fft.fft2(fg, axes=(1, 3), norm="ortho").real
    return y.reshape(B, L, FN_WIDTH).astype(f.dtype)


def hyena_kernel_spectrum(L, w1, b1, w2, b2, w3, freq):
    pos = jnp.arange(L, dtype=jnp.float32)[:, None]
    t = pos / max(L - 1, 1)
    bands = jnp.linspace(1e-4, HY_BANDS - 1, HY_BANDS, dtype=jnp.float32)
    ang = bands * (2.0 * math.pi / L) * pos
    z = jnp.concatenate([t, jnp.cos(ang), -jnp.sin(ang)], axis=-1)
    hdn = jnp.sin(freq * (z @ w1 + b1))
    hdn = jnp.sin(freq * (hdn @ w2 + b2))
    h = (hdn @ w3).astype(jnp.float32).reshape(L, HY_ORDER, 2, HY_WIDTH)
    deltas = jnp.linspace(HY_MIN_DECAY, HY_MAX_DECAY, HY_WIDTH, dtype=jnp.float32)
    h = h * jnp.exp(-t * jnp.abs(deltas))[:, None, None, :]
    hf, hb = h[:, :, 0], h[:, :, 1]
    kernel = jnp.concatenate([hf, jnp.zeros_like(hf[:1]), hb[:0:-1]], axis=0)
    return jnp.fft.rfft(kernel, axis=0)


def long_conv(z, k_spec, bias):
    L = z.shape[1]
    zf = z.astype(jnp.float32)
    y = jnp.fft.irfft(jnp.fft.rfft(zf, n=2 * L, axis=1) * k_spec, n=2 * L, axis=1)[:, :L]
    return (y + zf * bias).astype(z.dtype)


def hyena_mix(u, conv_w, conv_b, k_spec, bias):
    u = dwconv_centred(u, conv_w, conv_b)
    v, x1, x2 = jnp.split(u, HY_ORDER + 1, axis=-1)
    z = x1 * long_conv(v, k_spec[:, 0], bias[0])
    return x2 * long_conv(z, k_spec[:, 1], bias[1])


def merge_groups(a, f, hy, g):
    o1, o2 = NA_WIDTH, NA_WIDTH + FN_WIDTH
    return jnp.concatenate([rmsnorm(a, g[:o1]), rmsnorm(f, g[o1:o2]), rmsnorm(hy, g[o2:])], axis=-1)


def conv_gated_mlp(h, w_up, conv_w, conv_b, w_down):
    gate, up = jnp.split(h @ w_up, 2, axis=-1)
    return (jax.nn.gelu(dwconv_centred(gate, conv_w, conv_b), approximate=True) * up) @ w_down


def setup_inputs(seed: int = 0) -> dict:
    key = jax.random.key(seed)
    ks = jax.random.split(key, 26)

    def nrm(k, shape, s):
        return s * jax.random.normal(k, shape, dtype=jnp.float32)

    def gain(k, shape):
        return 1.0 + 0.02 * jax.random.normal(k, shape, dtype=jnp.float32)

    return {
        "x": nrm(ks[0], (BATCH, SEQ, D_MODEL), 1.0),
        "c": nrm(ks[1], (BATCH, D_MODEL), 1.0),
        "ctx": nrm(ks[2], (BATCH, CTX_LEN, D_MODEL), 1.0),
        "c_ctx": nrm(ks[3], (D_MODEL,), 1.0),
        "ada_w": nrm(ks[4], (DEPTH, D_MODEL, N_MOD * D_MODEL), 0.3 * D_MODEL ** -0.5),
        "ada_b": nrm(ks[5], (DEPTH, N_MOD * D_MODEL), 0.02),
        "norm1_g": gain(ks[6], (DEPTH, D_MODEL)),
        "norm2_g": gain(ks[7], (DEPTH, D_MODEL)),
        "w_in": nrm(ks[8], (DEPTH, D_MODEL, D_IN), D_MODEL ** -0.5),
        "na_rpb": nrm(ks[9], (DEPTH, NA_HEADS, 2 * NA_KH - 1, 2 * NA_KW - 1), 0.02),
        "hy_conv_w": nrm(ks[10], (DEPTH, CONV_W, (HY_ORDER + 1) * HY_WIDTH), CONV_W ** -0.5),
        "hy_conv_b": nrm(ks[11], (DEPTH, (HY_ORDER + 1) * HY_WIDTH), 0.02),
        "hy_w1": nrm(ks[12], (DEPTH, HY_EMB, HY_FILTER_HIDDEN), HY_EMB ** -0.5),
        "hy_b1": nrm(ks[13], (DEPTH, HY_FILTER_HIDDEN), 0.02),
        "hy_w2": nrm(ks[14], (DEPTH, HY_FILTER_HIDDEN, HY_FILTER_HIDDEN), HY_FILTER_HIDDEN ** -0.5),
        "hy_b2": nrm(ks[15], (DEPTH, HY_FILTER_HIDDEN), 0.02),
        "hy_w3": nrm(ks[16], (DEPTH, HY_FILTER_HIDDEN, HY_ORDER * 2 * HY_WIDTH), HY_FILTER_HIDDEN ** -0.5),
        "hy_freq": gain(ks[17], (DEPTH, HY_FILTER_HIDDEN)),
        "hy_bias": nrm(ks[18], (DEPTH, HY_ORDER, HY_WIDTH), 0.1),
        "mix_norm_g": gain(ks[19], (DEPTH, D_MIX)),
        "w_out": nrm(ks[20], (DEPTH, D_MIX, D_MODEL), D_MIX ** -0.5),
        "ffn_w_up": nrm(ks[21], (DEPTH, D_MODEL, 2 * D_FF), D_MODEL ** -0.5),
        "ffn_conv_w": nrm(ks[22], (DEPTH, CONV_W, D_FF), CONV_W ** -0.5),
        "ffn_conv_b": nrm(ks[23], (DEPTH, D_FF), 0.02),
        "ffn_w_down": nrm(ks[24], (DEPTH, D_FF, D_MODEL), D_FF ** -0.5),
        "final_norm_g": gain(ks[25], (D_MODEL,)),
    }


def reference(x, c, ctx, c_ctx, ada_w, ada_b, norm1_g, norm2_g, w_in, na_rpb,
              hy_conv_w, hy_conv_b, hy_w1, hy_b1, hy_w2, hy_b2, hy_w3, hy_freq, hy_bias,
              mix_norm_g, w_out, ffn_w_up, ffn_conv_w, ffn_conv_b, ffn_w_down, final_norm_g):
    L = x.shape[1]
    Lc = ctx.shape[1]
    xc = ctx
    s_lat = jax.nn.silu(c)
    s_ctx = jax.nn.silu(c_ctx)
    for l in range(DEPTH):
        update_ctx = l < DEPTH - 1
        mod = (s_lat @ ada_w[l] + ada_b[l])[:, None, :]
        sh1, sc1, g1, sh2, sc2, g2 = jnp.split(mod, N_MOD, axis=-1)
        mod_c = s_ctx @ ada_w[l] + ada_b[l]
        csh1, csc1, cg1, csh2, csc2, cg2 = jnp.split(mod_c, N_MOD, axis=-1)
        filt = (hy_w1[l], hy_b1[l], hy_w2[l], hy_b2[l], hy_w3[l], hy_freq[l])

        h = modulate(rmsnorm(x, norm1_g[l]), sh1, sc1)
        hc = modulate(rmsnorm(xc, norm1_g[l]), csh1, csc1)
        q, k, v, f, hy = split_projection(h @ w_in[l])
        if update_ctx:
            qc, kc, vc, fc, hyc = split_projection(hc @ w_in[l])
        else:
            kc, vc = jnp.split(hc @ w_in[l][:, NA_WIDTH:3 * NA_WIDTH], 2, axis=-1)
        kc_h, vc_h = heads(kc), heads(vc)
        a = neighbourhood_attention(heads(q), heads(k), heads(v), kc_h, vc_h, na_rpb[l])
        y = merge_groups(a, fourier_mix(f),
                         hyena_mix(hy, hy_conv_w[l], hy_conv_b[l], hyena_kernel_spectrum(L, *filt), hy_bias[l]),
                         mix_norm_g[l])
        x = x + g1 * (y @ w_out[l])
        if update_ctx:
            ac = context_attention(heads(qc), kc_h, vc_h)
            yc = merge_groups(ac, fourier_mix(fc),
                              hyena_mix(hyc, hy_conv_w[l], hy_conv_b[l], hyena_kernel_spectrum(Lc, *filt), hy_bias[l]),
                              mix_norm_g[l])
            xc = xc + cg1 * (yc @ w_out[l])

        ffn = (ffn_w_up[l], ffn_conv_w[l], ffn_conv_b[l], ffn_w_down[l])
        x = x + g2 * conv_gated_mlp(modulate(rmsnorm(x, norm2_g[l]), sh2, sc2), *ffn)
        if update_ctx:
            xc = xc + cg2 * conv_gated_mlp(modulate(rmsnorm(xc, norm2_g[l]), csh2, csc2), *ffn)
    return rmsnorm(x, final_norm_g)
```

```python
import functools
import math

import numpy as np
import jax
import jax.numpy as jnp
from jax import lax
from jax.experimental import pallas as pl
from jax.experimental.pallas import tpu as pltpu

F32 = jnp.float32
BF16 = jnp.bfloat16

EPS = 1e-6
HEAD_DIM = 128
GRID_W = 64
NA_KH = 8
NA_KW = 16
NA_ROWS_PER_BLOCK = 8
FN_GROUP_DIM = 128
HY_ORDER = 2
HY_BANDS = 16
HY_FAST_DECAY = 0.3
HY_SLOW_DECAY = 1.5
HY_TARGET = 1e-2
HY_MIN_DECAY = math.log(HY_TARGET) / HY_SLOW_DECAY
HY_MAX_DECAY = math.log(HY_TARGET) / HY_FAST_DECAY
N_MOD = 6
LANES = 128
NEG_INF = -1e30
VMEM_LIMIT = 56 * 1024 * 1024

NT_DIMS = (((1,), (1,)), ((), ()))


def _cparams(sem):
    return pltpu.CompilerParams(dimension_semantics=sem, vmem_limit_bytes=VMEM_LIMIT)


def _hi_lo(a):
    hi = a.astype(BF16)
    lo = (a - hi.astype(F32)).astype(BF16)
    return hi, lo


def _np_hi_lo(m):
    m = np.asarray(m, np.float32)
    hi = m.astype(BF16)
    lo = (m - hi.astype(np.float32)).astype(BF16)
    return hi, lo


def _const_rhs3(m):
    hi, lo = _np_hi_lo(m)
    return jnp.asarray(np.concatenate([hi, hi, lo], axis=0))


def _const_lhs3(m):
    hi, lo = _np_hi_lo(m)
    return jnp.asarray(np.concatenate([hi, hi, lo], axis=1))


def _dot_data_const(a, c3):
    hi, lo = _hi_lo(a)
    return jnp.dot(jnp.concatenate([hi, lo, hi], axis=1), c3, preferred_element_type=F32)


def _dot_const_data(c3, b):
    hi, lo = _hi_lo(b)
    return jnp.dot(c3, jnp.concatenate([hi, lo, hi], axis=0), preferred_element_type=F32)


def _dot3(a, b):
    ah, al = _hi_lo(a)
    bh, bl = _hi_lo(b)
    return (jnp.dot(ah, bh, preferred_element_type=F32)
            + jnp.dot(al, bh, preferred_element_type=F32)
            + jnp.dot(ah, bl, preferred_element_type=F32))


def _ada_kernel(s_ref, w_ref, b_ref, o_ref):
    s = s_ref[...]
    s = s / (1.0 + jnp.exp(-s))
    o_ref[0] = _dot3(s, w_ref[0]) + b_ref[0]


def _ada_mod(s_in, ada_w, ada_b):
    depth, d, n = ada_w.shape
    tn = 1024
    return pl.pallas_call(
        _ada_kernel,
        out_shape=jax.ShapeDtypeStruct((depth, 8, n), F32),
        grid=(depth, n // tn),
        in_specs=[
            pl.BlockSpec((8, d), lambda l, j: (0, 0)),
            pl.BlockSpec((1, d, tn), lambda l, j: (l, 0, j)),
            pl.BlockSpec((1, 1, tn), lambda l, j: (l, 0, j)),
        ],
        out_specs=pl.BlockSpec((1, 8, tn), lambda l, j: (l, 0, j)),
        compiler_params=_cparams(("parallel", "parallel")),
        name="ada_mod",
    )(s_in, ada_w, ada_b.reshape(depth, 1, n))


def _norm_mod(x, g, sh, sc):
    ms = jnp.mean(x * x, axis=-1, keepdims=True)
    return (x * lax.rsqrt(ms + EPS) * g) * (1.0 + sc) + sh


def _proj_kernel(x_ref, g_ref, sh_ref, sc_ref, w_ref, o_ref, h_ref, *, channel_major):
    @pl.when(pl.program_id(2) == 0)
    def _():
        h_ref[...] = _norm_mod(x_ref[0], g_ref[...], sh_ref[0], sc_ref[0]).astype(BF16)

    if channel_major:
        o_ref[0] = lax.dot_general(w_ref[...], h_ref[...], NT_DIMS, preferred_element_type=F32)
    else:
        o_ref[0] = jnp.dot(h_ref[...], w_ref[...], preferred_element_type=F32)


def _proj(x, g, sh, sc, w, *, channel_major, tm, tn):
    b, l, d = x.shape
    n = w.shape[0] if channel_major else w.shape[1]
    tm = min(tm, l)
    if channel_major:
        w_spec = pl.BlockSpec((tn, d), lambda bi, i, j: (j, 0))
        out_shape = jax.ShapeDtypeStruct((b, n, l), F32)
        out_spec = pl.BlockSpec((1, tn, tm), lambda bi, i, j: (bi, j, i))
    else:
        w_spec = pl.BlockSpec((d, tn), lambda bi, i, j: (0, j))
        out_shape = jax.ShapeDtypeStruct((b, l, n), F32)
        out_spec = pl.BlockSpec((1, tm, tn), lambda bi, i, j: (bi, i, j))
    vec = pl.BlockSpec((1, 1, d), lambda bi, i, j: (bi, 0, 0))
    return pl.pallas_call(
        functools.partial(_proj_kernel, channel_major=channel_major),
        out_shape=out_shape,
        grid=(b, l // tm, n // tn),
        in_specs=[
            pl.BlockSpec((1, tm, d), lambda bi, i, j: (bi, i, 0)),
            pl.BlockSpec((1, d), lambda bi, i, j: (0, 0)),
            vec, vec, w_spec,
        ],
        out_specs=out_spec,
        scratch_shapes=[pltpu.VMEM((tm, d), BF16)],
        compiler_params=_cparams(("parallel", "parallel", "arbitrary")),
        name="proj_cm" if channel_major else "proj_tm",
    )(x, g.reshape(1, d), sh.reshape(b, 1, d), sc.reshape(b, 1, d), w)


def _fold_kernel(w_ref, m_ref, o_ref):
    o_ref[0] = _dot_data_const(w_ref[...], m_ref[...])


def _fold_fourier_weights(w_f):
    d, width = w_f.shape
    gd = FN_GROUP_DIM
    idx = np.arange(gd)
    ang = 2.0 * np.pi * np.outer(idx, idx) / gd
    mat = np.concatenate([np.cos(ang), -np.sin(ang)], axis=1) / math.sqrt(gd)
    groups = width // gd
    return pl.pallas_call(
        _fold_kernel,
        out_shape=jax.ShapeDtypeStruct((2, d, width), F32),
        grid=(groups, 2),
        in_specs=[
            pl.BlockSpec((d, gd), lambda g, p: (0, g)),
            pl.BlockSpec((3 * gd, gd), lambda g, p: (0, p)),
        ],
        out_specs=pl.BlockSpec((1, d, gd), lambda g, p: (p, 0, g)),
        compiler_params=_cparams(("parallel", "parallel")),
        name="fold_fourier",
    )(w_f, _const_rhs3(mat))


def _na_bias_table(rpb, rows):
    h = rpb.shape[0]
    w, kw, kh, rb = GRID_W, NA_KW, NA_KH, NA_ROWS_PER_BLOCK
    col = np.arange(w)
    cs = np.clip(col - kw // 2, 0, w - kw)
    kc = np.arange(w)[None, :]
    col_ok = (kc >= cs[:, None]) & (kc < cs[:, None] + kw)
    dcol = np.clip(kc - col[:, None] + (kw - 1), 0, 2 * kw - 2)
    t1 = jnp.where(col_ok[None, None], rpb[:, :, dcol], NEG_INF)
    t1 = jnp.concatenate([t1, jnp.full((h, 1, w, w), NEG_INF, rpb.dtype)], axis=1)
    drow_idx = np.full((3, rb, 2 * rb), 2 * kh - 1, np.int32)
    for t, row0 in enumerate((0, rb, rows - rb)):
        for qr in range(rb):
            r = row0 + qr
            r0 = r - kh // 2 if t == 1 else min(max(r - kh // 2, 0), rows - kh)
            for kr in range(2 * rb):
                a = row0 - rb // 2 + kr
                if r0 <= a < r0 + kh and (t == 1 or 0 <= a < rows):
                    drow_idx[t, qr, kr] = a - r + (kh - 1)
    tab = t1[:, drow_idx]
    tab = tab.transpose(1, 0, 2, 4, 3, 5)
    return tab.reshape(3, h, rb * w, 2 * rb * w)


def _na_kernel(q_ref, kp_ref, kc_ref, kn_ref, vp_ref, vc_ref, vn_ref, kx_ref, vx_ref, bias_ref, o_ref):
    scale = HEAD_DIM ** -0.5
    q = (q_ref[0] * scale).astype(BF16)
    k = jnp.concatenate([kp_ref[0], kc_ref[0], kn_ref[0]], axis=0).astype(BF16)
    v = jnp.concatenate([vp_ref[0], vc_ref[0], vn_ref[0]], axis=0).astype(BF16)
    s = lax.dot_general(q, k, NT_DIMS, preferred_element_type=F32) + bias_ref[0, 0]
    sx = lax.dot_general(q, kx_ref[0].astype(BF16), NT_DIMS, preferred_element_type=F32)
    m = jnp.maximum(jnp.max(s, axis=-1, keepdims=True), jnp.max(sx, axis=-1, keepdims=True))
    p = jnp.exp(s - m)
    px = jnp.exp(sx - m)
    den = jnp.sum(p, axis=-1, keepdims=True) + jnp.sum(px, axis=-1, keepdims=True)
    o = (jnp.dot(p.astype(BF16), v, preferred_element_type=F32)
         + jnp.dot(px.astype(BF16), vx_ref[0].astype(BF16), preferred_element_type=F32))
    o_ref[0] = o / den


def _na_attention(qkv, qkv_c, bias_tab, n_heads):
    b, l, _ = qkv.shape
    lc = qkv_c.shape[1]
    hd = HEAD_DIM
    tq = NA_ROWS_PER_BLOCK * GRID_W
    th = tq // 2
    nb = l // tq
    nh = l // th
    assert nb >= 2 and l % tq == 0

    def btype(i):
        return jnp.where(i == 0, 0, jnp.where(i == nb - 1, 2, 1))

    def cur(off):
        return pl.BlockSpec((1, tq, hd), lambda h, bi, i: (bi, i, off + h))

    def prev(off):
        return pl.BlockSpec((1, th, hd), lambda h, bi, i: (bi, jnp.maximum(2 * i - 1, 0), off + h))

    def nxt(off):
        return pl.BlockSpec((1, th, hd), lambda h, bi, i: (bi, jnp.minimum(2 * i + 2, nh - 1), off + h))

    def ctx(off):
        return pl.BlockSpec((1, lc, hd), lambda h, bi, i: (bi, 0, off + h))

    ko, vo = n_heads, 2 * n_heads
    return pl.pallas_call(
        _na_kernel,
        out_shape=jax.ShapeDtypeStruct((b, l, n_heads * hd), F32),
        grid=(n_heads, b, nb),
        in_specs=[cur(0), prev(ko), cur(ko), nxt(ko), prev(vo), cur(vo), nxt(vo), ctx(ko), ctx(vo),
                  pl.BlockSpec((1, 1, tq, 2 * tq), lambda h, bi, i: (btype(i), h, 0, 0))],
        out_specs=pl.BlockSpec((1, tq, hd), lambda h, bi, i: (bi, i, h)),
        compiler_params=_cparams(("parallel", "parallel", "arbitrary")),
        name="na_attention",
    )(qkv, qkv, qkv, qkv, qkv, qkv, qkv, qkv_c, qkv_c, bias_tab)


def _ctx_attn_kernel(q_ref, k_ref, v_ref, o_ref):
    scale = HEAD_DIM ** -0.5
    q = (q_ref[0] * scale).astype(BF16)
    s = lax.dot_general(q, k_ref[0].astype(BF16), NT_DIMS, preferred_element_type=F32)
    m = jnp.max(s, axis=-1, keepdims=True)
    p = jnp.exp(s - m)
    den = jnp.sum(p, axis=-1, keepdims=True)
    o_ref[0] = jnp.dot(p.astype(BF16), v_ref[0].astype(BF16), preferred_element_type=F32) / den


def _ctx_attention(qkv_c, n_heads):
    b, lc, _ = qkv_c.shape
    hd = HEAD_DIM

    def spec(off):
        return pl.BlockSpec((1, lc, hd), lambda bi, h: (bi, 0, off + h))

    return pl.pallas_call(
        _ctx_attn_kernel,
        out_shape=jax.ShapeDtypeStruct((b, lc, n_heads * hd), F32),
        grid=(b, n_heads),
        in_specs=[spec(0), spec(n_heads), spec(2 * n_heads)],
        out_specs=spec(0),
        compiler_params=_cparams(("parallel", "parallel")),
        name="ctx_attention",
    )(qkv_c, qkv_c, qkv_c)


def _cos_sin(n_out, n_in, period):
    ang = 2.0 * np.pi * ((np.arange(n_out)[:, None] * np.arange(n_in)[None, :]) % period) / period
    return np.cos(ang), np.sin(ang)


def _lane_cat(x3):
    return jnp.concatenate([x3[j] for j in range(x3.shape[0])], axis=1)


def _rows_from_lanes(re, im, nc):
    return jnp.concatenate(
        [jnp.concatenate([re[:, j * LANES:(j + 1) * LANES], im[:, j * LANES:(j + 1) * LANES]], axis=1)
         for j in range(nc)], axis=0)


def _lanes_from_rows(x, nc, r):
    re = jnp.concatenate([x[j * r:(j + 1) * r, :LANES] for j in range(nc)], axis=1)
    im = jnp.concatenate([x[j * r:(j + 1) * r, LANES:] for j in range(nc)], axis=1)
    return jnp.concatenate([re, im], axis=0)


def _fourier_tables(r, nc):
    l = r * LANES
    c1, s1 = _cos_sin(r, r, r)
    m1 = np.block([[c1, s1], [-s1, c1]])
    tc, ts = _cos_sin(r, LANES, l)
    c2, s2 = _cos_sin(LANES, LANES, LANES)
    m2 = np.concatenate([c2, s2], axis=0) / math.sqrt(l)
    return (_const_lhs3(m1), jnp.asarray(np.tile(tc, (1, nc)), F32), jnp.asarray(np.tile(ts, (1, nc)), F32),
            _const_rhs3(m2))


def _fourier_kernel(zr_ref, zi_ref, m1_ref, tc_ref, ts_ref, m2_ref, o_ref, *, nc, r):
    z = jnp.concatenate([_lane_cat(zr_ref[0]), _lane_cat(zi_ref[0])], axis=0)
    a = _dot_const_data(m1_ref[...], z)
    ar, ai = a[:r], a[r:]
    tc, ts = tc_ref[...], ts_ref[...]
    br = ar * tc + ai * ts
    bi = ai * tc - ar * ts
    y = _dot_data_const(_rows_from_lanes(br, bi, nc), m2_ref[...])
    for j in range(nc):
        o_ref[0, j] = y[j * r:(j + 1) * r].T


def _fourier_latent(pt, n_ch, nc):
    b, _, l = pt.shape
    r = l // LANES
    p4 = pt.reshape(b, pt.shape[1], r, LANES)
    m1, tc, ts, m2 = _fourier_tables(r, nc)
    nblk = n_ch // nc

    def const(a):
        return pl.BlockSpec(a.shape, lambda bi, c: (0,) * a.ndim)

    out = pl.pallas_call(
        functools.partial(_fourier_kernel, nc=nc, r=r),
        out_shape=jax.ShapeDtypeStruct((b, n_ch, LANES, r), F32),
        grid=(b, nblk),
        in_specs=[
            pl.BlockSpec((1, nc, r, LANES), lambda bi, c: (bi, c, 0, 0)),
            pl.BlockSpec((1, nc, r, LANES), lambda bi, c: (bi, nblk + c, 0, 0)),
            const(m1), const(tc), const(ts), const(m2),
        ],
        out_specs=pl.BlockSpec((1, nc, LANES, r), lambda bi, c: (bi, c, 0, 0)),
        compiler_params=_cparams(("parallel", "parallel")),
        name="fourier_latent",
    )(p4, p4, m1, tc, ts, m2)
    return out.reshape(b, n_ch, l)


def _filter_kernel(bands_ref, w1t_ref, w1c_ref, w1s_ref, b1_ref, w2_ref, b2_ref, w3_ref, fr_ref, dl_ref,
                   o_ref, *, l, tl):
    i = pl.program_id(0)
    n = (i * tl + lax.broadcasted_iota(jnp.int32, (1, tl), 1))
    pos_i = jnp.where(n < l, n, 2 * l - n)
    pos = pos_i.astype(F32)
    t = pos / float(max(l - 1, 1))
    ang = bands_ref[...] * (2.0 * math.pi / l) * pos
    fr = fr_ref[...]
    cos_a, sin_a = jnp.cos(ang), jnp.sin(ang)
    w1c, w1s = w1c_ref[...], w1s_ref[...]
    pre = w1t_ref[...] * t
    for k in range(HY_BANDS):
        pre = pre + w1c[:, k:k + 1] * cos_a[k:k + 1, :] - w1s[:, k:k + 1] * sin_a[k:k + 1, :]
    h1 = jnp.sin(fr * (pre + b1_ref[...]))
    h2 = jnp.sin(fr * (_dot3(w2_ref[...], h1) + b2_ref[...]))
    out = _dot3(w3_ref[0], h2) * jnp.exp(-t * dl_ref[...])
    o_ref[...] = jnp.where(n == l, 0.0, out)


def _hyena_filters(l, w1, b1, w2, b2, w3, freq, width):
    hid0 = w1.shape[1]
    hid = -(-hid0 // LANES) * LANES
    pad = hid - hid0
    tl = min(2048, l)
    rows = HY_ORDER * width
    bands = np.linspace(1e-4, HY_BANDS - 1, HY_BANDS, dtype=np.float32).reshape(HY_BANDS, 1)
    deltas = np.abs(np.linspace(HY_MIN_DECAY, HY_MAX_DECAY, width, dtype=np.float32))
    dl = np.tile(deltas, HY_ORDER).reshape(rows, 1)
    w1t = jnp.pad(w1.T, ((0, pad), (0, 0)))
    w2 = jnp.pad(w2, ((0, pad), (0, pad)))
    w3d = w3.reshape(hid0, HY_ORDER, 2, width).transpose(2, 1, 3, 0).reshape(2, rows, hid0)
    w3d = jnp.pad(w3d, ((0, 0), (0, 0), (0, pad)))
    col = lambda v: jnp.pad(v, (0, pad)).reshape(hid, 1)

    def const(shape):
        return pl.BlockSpec(shape, lambda i: (0,) * len(shape))

    return pl.pallas_call(
        functools.partial(_filter_kernel, l=l, tl=tl),
        out_shape=jax.ShapeDtypeStruct((rows, 2 * l), F32),
        grid=(2 * l // tl,),
        in_specs=[const((HY_BANDS, 1)), const((hid, 1)), const((hid, HY_BANDS)), const((hid, HY_BANDS)),
                  const((hid, 1)), const((hid, hid)), const((hid, 1)),
                  pl.BlockSpec((1, rows, hid), lambda i: (i // (l // tl), 0, 0)),
                  const((hid, 1)), const((rows, 1))],
        out_specs=pl.BlockSpec((rows, tl), lambda i: (0, i)),
        compiler_params=_cparams(("parallel",)),
        name="hyena_filters",
    )(jnp.asarray(bands), w1t[:, 0:1], w1t[:, 1:1 + HY_BANDS], w1t[:, 1 + HY_BANDS:], col(b1), w2.T, col(b2),
      w3d, col(freq), jnp.asarray(dl))


def _conv_tables(r, nc):
    r2 = 2 * r
    n = r2 * LANES
    c1, s1 = _cos_sin(r2, r2, r2)
    c1h, s1h = c1[:, :r], s1[:, :r]
    m1 = np.block([[c1h, s1h], [-s1h, c1h]])
    m1_real = np.concatenate([c1, -s1], axis=0)
    tc, ts = _cos_sin(r2, LANES, n)
    c2, s2 = _cos_sin(LANES, LANES, LANES)
    m2 = np.block([[c2, -s2], [s2, c2]])
    m2i = np.block([[c2, s2], [-s2, c2]])
    ct, st = c1h.T, s1h.T
    m1i = np.block([[ct, -st], [st, ct]]) / n
    f32 = lambda a: jnp.asarray(a, F32)
    return dict(
        m1=_const_lhs3(m1), m1_real=_const_lhs3(m1_real), m2=_const_rhs3(m2), m2i=_const_rhs3(m2i),
        m1i=_const_lhs3(m1i),
        tc_l=f32(np.tile(tc, (1, nc))), ts_l=f32(np.tile(ts, (1, nc))),
        tc_r=f32(np.tile(tc, (nc, 1))), ts_r=f32(np.tile(ts, (nc, 1))))


def _spec_kernel(k_ref, m1_ref, tc_ref, ts_ref, m2_ref, o_ref, *, nc, r2):
    a = _dot_const_data(m1_ref[...], _lane_cat(k_ref[...]))
    ar, ai = a[:r2], a[r2:]
    tc, ts = tc_ref[...], ts_ref[...]
    br = ar * tc + ai * ts
    bi = ai * tc - ar * ts
    x = _dot_data_const(_rows_from_lanes(br, bi, nc), m2_ref[...])
    o_ref[...] = x.reshape(nc, r2, 2 * LANES)


def _hyena_spectra(k2t, tabs, nc):
    c2n, n = k2t.shape
    r2 = n // LANES
    k3 = k2t.reshape(c2n, r2, LANES)

    def const(a):
        return pl.BlockSpec(a.shape, lambda c: (0,) * a.ndim)

    return pl.pallas_call(
        functools.partial(_spec_kernel, nc=nc, r2=r2),
        out_shape=jax.ShapeDtypeStruct((c2n, r2, 2 * LANES), F32),
        grid=(c2n // nc,),
        in_specs=[pl.BlockSpec((nc, r2, LANES), lambda c: (c, 0, 0)),
                  const(tabs["m1_real"]), const(tabs["tc_l"]), const(tabs["ts_l"]), const(tabs["m2"])],
        out_specs=pl.BlockSpec((nc, r2, 2 * LANES), lambda c: (c, 0, 0)),
        compiler_params=_cparams(("parallel",)),
        name="hyena_spectra",
    )(k3, tabs["m1_real"], tabs["tc_l"], tabs["ts_l"], tabs["m2"])


def _shift_tokens(t, r, direction):
    w = t.shape[1]
    lane = lax.broadcasted_iota(jnp.int32, t.shape, 1) & (LANES - 1)
    row = lax.broadcasted_iota(jnp.int32, t.shape, 0)
    if direction < 0:
        near = pltpu.roll(t, 1, 1)
        wrap = pltpu.roll(pltpu.roll(t, w - (LANES - 1), 1), 1, 0)
        edge = lane == 0
        dead = edge & (row == 0)
    else:
        near = pltpu.roll(t, w - 1, 1)
        wrap = pltpu.roll(pltpu.roll(t, LANES - 1, 1), r - 1, 0)
        edge = lane == LANES - 1
        dead = edge & (row == r - 1)
    return jnp.where(dead, 0.0, jnp.where(edge, wrap, near))


def _long_conv(zr, zi, kspec, m1, m2, m2i, m1i, tc_l, ts_l, tc_r, ts_r, nc, r):
    r2 = 2 * r
    a = _dot_const_data(m1, jnp.concatenate([zr, zi], axis=0))
    ar, ai = a[:r2], a[r2:]
    br = ar * tc_l + ai * ts_l
    bi = ai * tc_l - ar * ts_l
    x = _dot_data_const(_rows_from_lanes(br, bi, nc), m2)
    xr, xi = x[:, :LANES], x[:, LANES:]
    kr, ki = kspec[:, :LANES], kspec[:, LANES:]
    y = jnp.concatenate([xr * kr - xi * ki, xr * ki + xi * kr], axis=1)
    bb = _dot_data_const(y, m2i)
    pr, pi = bb[:, :LANES], bb[:, LANES:]
    qr = pr * tc_r - pi * ts_r
    qi = pi * tc_r + pr * ts_r
    out = _dot_const_data(m1i, _lanes_from_rows(jnp.concatenate([qr, qi], axis=1), nc, r2))
    return out[:r], out[r:]


def _hyena_kernel(v0, v1, a0, a1, b0, b1, k0_ref, k1_ref, cw_ref, cb_ref, bias_ref,
                  m1_ref, m2_ref, m2i_ref, m1i_ref, tcl_ref, tsl_ref, tcr_ref, tsr_ref, o_ref, *, nc, r):
    tabs = (m1_ref[...], m2_ref[...], m2i_ref[...], m1i_ref[...],
            tcl_ref[...], tsl_ref[...], tcr_ref[...], tsr_ref[...])

    def short_conv(ref, part):
        t = _lane_cat(ref[0])
        w = cw_ref[part]
        return (_shift_tokens(t, r, -1) * w[0:1] + t * w[1:2] + _shift_tokens(t, r, +1) * w[2:3]
                + cb_ref[part])

    vr, vi = short_conv(v0, 0), short_conv(v1, 0)
    x1r, x1i = short_conv(a0, 1), short_conv(a1, 1)
    x2r, x2i = short_conv(b0, 2), short_conv(b1, 2)
    bias = bias_ref[...]
    r2 = 2 * r
    yr, yi = _long_conv(vr, vi, k0_ref[...].reshape(nc * r2, 2 * LANES), *tabs, nc, r)
    zr = x1r * (yr + vr * bias[0:1])
    zi = x1i * (yi + vi * bias[0:1])
    yr, yi = _long_conv(zr, zi, k1_ref[...].reshape(nc * r2, 2 * LANES), *tabs, nc, r)
    outr = x2r * (yr + zr * bias[1:2])
    outi = x2i * (yi + zi * bias[1:2])
    for j in range(nc):
        o_ref[0, j] = outr[:, j * LANES:(j + 1) * LANES]
        o_ref[1, j] = outi[:, j * LANES:(j + 1) * LANES]


def _hyena_latent(pt, ch0, width, spec, conv_w, conv_b, bias, tabs, nc):
    b, c_all, l = pt.shape
    assert b == 2, "the two batch entries ride one complex transform"
    r = l // LANES
    r2 = 2 * r
    p4 = pt.reshape(b, c_all, r, LANES)
    nblk = width // nc
    rep = lambda a: jnp.repeat(a, LANES, axis=-1)
    cw = rep(conv_w.reshape(3, 3, width).transpose(1, 0, 2))
    cb = rep(conv_b.reshape(3, 1, width))
    bs = rep(bias)

    def inp(bi, part):
        off = (ch0 + part * width) // nc
        return pl.BlockSpec((1, nc, r, LANES), lambda c: (bi, off + c, 0, 0))

    def const(a):
        return pl.BlockSpec(a.shape, lambda c: (0,) * a.ndim)

    names = ("m1", "m2", "m2i", "m1i", "tc_l", "ts_l", "tc_r", "ts_r")
    consts = [tabs[k] for k in names]
    out = pl.pallas_call(
        functools.partial(_hyena_kernel, nc=nc, r=r),
        out_shape=jax.ShapeDtypeStruct((b, width, r, LANES), F32),
        grid=(nblk,),
        in_specs=[inp(0, 0), inp(1, 0), inp(0, 1), inp(1, 1), inp(0, 2), inp(1, 2),
                  pl.BlockSpec((nc, r2, 2 * LANES), lambda c: (c, 0, 0)),
                  pl.BlockSpec((nc, r2, 2 * LANES), lambda c: (nblk + c, 0, 0)),
                  pl.BlockSpec((3, 3, nc * LANES), lambda c: (0, 0, c)),
                  pl.BlockSpec((3, 1, nc * LANES), lambda c: (0, 0, c)),
                  pl.BlockSpec((HY_ORDER, nc * LANES), lambda c: (0, c))]
                 + [const(a) for a in consts],
        out_specs=pl.BlockSpec((b, nc, r, LANES), lambda c: (0, c, 0, 0)),
        compiler_params=_cparams(("parallel",)),
        name="hyena_latent",
    )(p4, p4, p4, p4, p4, p4, spec, spec, cw, cb, bs, *consts)
    return out.reshape(b, width, l)


def _ctx_fourier_kernel(zr_ref, zi_ref, m_ref, o_ref):
    z = jnp.concatenate([zr_ref[0], zi_ref[0]], axis=1)
    o_ref[0] = _dot_data_const(z, m_ref[...])


def _ctx_fourier(ptc, n_ch):
    b, _, lc = ptc.shape
    c, s = _cos_sin(lc, lc, lc)
    m = _const_rhs3(np.concatenate([c, s], axis=0) / math.sqrt(lc))
    return pl.pallas_call(
        _ctx_fourier_kernel,
        out_shape=jax.ShapeDtypeStruct((b, n_ch, lc), F32),
        grid=(b,),
        in_specs=[pl.BlockSpec((1, n_ch, lc), lambda bi: (bi, 0, 0)),
                  pl.BlockSpec((1, n_ch, lc), lambda bi: (bi, 1, 0)),
                  pl.BlockSpec(m.shape, lambda bi: (0, 0))],
        out_specs=pl.BlockSpec((1, n_ch, lc), lambda bi: (bi, 0, 0)),
        compiler_params=_cparams(("parallel",)),
        name="ctx_fourier",
    )(ptc, ptc, m)


def _ctx_hyena_kernel(v0, v1, a0, a1, b0, b1, k_ref, cw_ref, cb_ref, bias_ref, mk_ref, mf_ref, mi_ref,
                      o_ref, *, lc, width):
    def short_conv(ref, part):
        t = ref[0]
        lane = lax.broadcasted_iota(jnp.int32, t.shape, 1)
        prv = jnp.where(lane == 0, 0.0, pltpu.roll(t, 1, 1))
        nxt = jnp.where(lane == lc - 1, 0.0, pltpu.roll(t, lc - 1, 1))
        w = cw_ref[part]
        return prv * w[:, 0:1] + t * w[:, 1:2] + nxt * w[:, 2:3] + cb_ref[part]

    kspec = _dot_data_const(k_ref[...], mk_ref[...])
    n = 2 * lc

    def long_conv(zr, zi, ks):
        x = _dot_data_const(jnp.concatenate([zr, zi], axis=1), mf_ref[...])
        xr, xi = x[:, :n], x[:, n:]
        kr, ki = ks[:, :n], ks[:, n:]
        y = jnp.concatenate([xr * kr - xi * ki, xr * ki + xi * kr], axis=1)
        out = _dot_data_const(y, mi_ref[...])
        return out[:, :lc], out[:, lc:]

    vr, vi = short_conv(v0, 0), short_conv(v1, 0)
    x1r, x1i = short_conv(a0, 1), short_conv(a1, 1)
    x2r, x2i = short_conv(b0, 2), short_conv(b1, 2)
    bias = bias_ref[...]
    yr, yi = long_conv(vr, vi, kspec[:width])
    zr = x1r * (yr + vr * bias[:, 0:1])
    zi = x1i * (yi + vi * bias[:, 0:1])
    yr, yi = long_conv(zr, zi, kspec[width:])
    o_ref[0] = x2r * (yr + zr * bias[:, 1:2])
    o_ref[1] = x2i * (yi + zi * bias[:, 1:2])


def _ctx_hyena(ptc, ch0, width, k2t, conv_w, conv_b, bias):
    b, _, lc = ptc.shape
    assert b == 2
    n = 2 * lc
    c, s = _cos_sin(n, n, n)
    mk = _const_rhs3(np.concatenate([c, -s], axis=1))
    ch, sh = c[:lc], s[:lc]
    mf = _const_rhs3(np.block([[ch, -sh], [sh, ch]]))
    ci, si = c[:, :lc], s[:, :lc]
    mi = _const_rhs3(np.block([[ci, si], [-si, ci]]) / n)
    cw = conv_w.reshape(3, 3, width).transpose(1, 2, 0)
    cb = conv_b.reshape(3, width, 1)

    def inp(bi, part):
        return pl.BlockSpec((1, width, lc), lambda i: (bi, ch0 // width + part, 0))

    def const(a):
        return pl.BlockSpec(a.shape, lambda i: (0,) * a.ndim)

    args = (k2t, cw, cb, bias.T, mk, mf, mi)
    return pl.pallas_call(
        functools.partial(_ctx_hyena_kernel, lc=lc, width=width),
        out_shape=jax.ShapeDtypeStruct((b, width, lc), F32),
        grid=(1,),
        in_specs=[inp(0, 0), inp(1, 0), inp(0, 1), inp(1, 1), inp(0, 2), inp(1, 2)] + [const(a) for a in args],
        out_specs=pl.BlockSpec((b, width, lc), lambda i: (0, 0, 0)),
        compiler_params=_cparams(("arbitrary",)),
        name="ctx_hyena",
    )(ptc, ptc, ptc, ptc, ptc, ptc, *args)


def _merge_kernel(a_ref, f_ref, hy_ref, x_ref, gate_ref, ga_ref, gf_ref, gh_ref, wa_ref, wf_ref, wh_ref, o_ref):
    a = a_ref[0]
    ya = a * lax.rsqrt(jnp.mean(a * a, axis=-1, keepdims=True) + EPS) * ga_ref[...]
    acc = jnp.dot(ya.astype(BF16), wa_ref[...], preferred_element_type=F32)

    def cm_part(ref, g_ref, w_ref):
        t = ref[0]
        y = t * lax.rsqrt(jnp.mean(t * t, axis=0, keepdims=True) + EPS) * g_ref[...]
        return jnp.dot(y.T.astype(BF16), w_ref[...], preferred_element_type=F32)

    acc = acc + cm_part(f_ref, gf_ref, wf_ref) + cm_part(hy_ref, gh_ref, wh_ref)
    o_ref[0] = x_ref[0] + gate_ref[0] * acc


def _merge_out(a, ft, ht, x, gate, g, w_out, tm):
    b, l, d = x.shape
    wa_n, wf_n, wh_n = a.shape[2], ft.shape[1], ht.shape[1]
    tm = min(tm, l)
    w = w_out.astype(BF16)
    wa, wf, wh = w[:wa_n], w[wa_n:wa_n + wf_n], w[wa_n + wf_n:]
    ga = g[:wa_n].reshape(1, wa_n)
    gf = g[wa_n:wa_n + wf_n].reshape(wf_n, 1)
    gh = g[wa_n + wf_n:].reshape(wh_n, 1)

    def const(arr):
        return pl.BlockSpec(arr.shape, lambda bi, i: (0,) * arr.ndim)

    return pl.pallas_call(
        _merge_kernel,
        out_shape=jax.ShapeDtypeStruct((b, l, d), F32),
        grid=(b, l // tm),
        in_specs=[pl.BlockSpec((1, tm, wa_n), lambda bi, i: (bi, i, 0)),
                  pl.BlockSpec((1, wf_n, tm), lambda bi, i: (bi, 0, i)),
                  pl.BlockSpec((1, wh_n, tm), lambda bi, i: (bi, 0, i)),
                  pl.BlockSpec((1, tm, d), lambda bi, i: (bi, i, 0)),
                  pl.BlockSpec((1, 1, d), lambda bi, i: (bi, 0, 0)),
                  const(ga), const(gf), const(gh), const(wa), const(wf), const(wh)],
        out_specs=pl.BlockSpec((1, tm, d), lambda bi, i: (bi, i, 0)),
        compiler_params=_cparams(("parallel", "parallel")),
        name="merge_out",
    )(a, ft, ht, x, gate.reshape(b, 1, d), ga, gf, gh, wa, wf, wh)


HALO = 8


def _ffn_kernel(xp_ref, x_ref, xn_ref, g_ref, sh_ref, sc_ref, gate_ref, wg_ref, wu_ref, cw_ref, cb_ref, wd_ref,
                o_ref, h_ref, acc_ref, *, tm):
    i = pl.program_id(1)
    c = pl.program_id(2)
    last_tile = pl.num_programs(1) - 1

    @pl.when(c == 0)
    def _():
        g, sh, sc = g_ref[...], sh_ref[0], sc_ref[0]
        h_ref[0:HALO] = _norm_mod(xp_ref[0, 0], g, sh, sc).astype(BF16)
        h_ref[HALO:HALO + tm] = _norm_mod(x_ref[0], g, sh, sc).astype(BF16)
        h_ref[HALO + tm:] = _norm_mod(xn_ref[0, 0], g, sh, sc).astype(BF16)
        acc_ref[...] = jnp.zeros_like(acc_ref)

    h = h_ref[...]
    ge = jnp.dot(h, wg_ref[...], preferred_element_type=F32)
    up = jnp.dot(h[HALO:HALO + tm], wu_ref[...], preferred_element_type=F32)
    row = lax.broadcasted_iota(jnp.int32, ge.shape, 0)
    outside = ((row < HALO) & (i == 0)) | ((row >= HALO + tm) & (i == last_tile))
    ge = jnp.where(outside, 0.0, ge)
    n_ext = tm + 2 * HALO
    gp = pltpu.roll(ge, 1, 0)[HALO:HALO + tm]
    gn = pltpu.roll(ge, n_ext - 1, 0)[HALO:HALO + tm]
    cw = cw_ref[...]
    conv = gp * cw[0:1] + ge[HALO:HALO + tm] * cw[1:2] + gn * cw[2:3] + cb_ref[...]
    inner = 0.7978845608028654 * (conv + 0.044715 * (conv * conv * conv))
    act = 0.5 * conv * (1.0 + jnp.tanh(inner)) * up
    acc_ref[...] += jnp.dot(act.astype(BF16), wd_ref[...], preferred_element_type=F32)

    @pl.when(c == pl.num_programs(2) - 1)
    def _():
        o_ref[0] = x_ref[0] + gate_ref[0] * acc_ref[...]


def _ffn(x, g, sh, sc, gate, w_up, conv_w, conv_b, w_down, tm, tf):
    b, l, d = x.shape
    dff = w_down.shape[0]
    tm = min(tm, l)
    nch = dff // tf
    x4 = x.reshape(b, l // HALO, HALO, d)
    per_tile = tm // HALO
    nhalo = l // HALO
    vec = pl.BlockSpec((1, 1, d), lambda bi, i, c: (bi, 0, 0))
    return pl.pallas_call(
        functools.partial(_ffn_kernel, tm=tm),
        out_shape=jax.ShapeDtypeStruct((b, l, d), F32),
        grid=(b, l // tm, nch),
        in_specs=[
            pl.BlockSpec((1, 1, HALO, d), lambda bi, i, c: (bi, jnp.maximum(i * per_tile - 1, 0), 0, 0)),
            pl.BlockSpec((1, tm, d), lambda bi, i, c: (bi, i, 0)),
            pl.BlockSpec((1, 1, HALO, d), lambda bi, i, c: (bi, jnp.minimum((i + 1) * per_tile, nhalo - 1), 0, 0)),
            pl.BlockSpec((1, d), lambda bi, i, c: (0, 0)),
            vec, vec, vec,
            pl.BlockSpec((d, tf), lambda bi, i, c: (0, c)),
            pl.BlockSpec((d, tf), lambda bi, i, c: (0, nch + c)),
            pl.BlockSpec((3, tf), lambda bi, i, c: (0, c)),
            pl.BlockSpec((1, tf), lambda bi, i, c: (0, c)),
            pl.BlockSpec((tf, d), lambda bi, i, c: (c, 0)),
        ],
        out_specs=pl.BlockSpec((1, tm, d), lambda bi, i, c: (bi, i, 0)),
        scratch_shapes=[pltpu.VMEM((tm + 2 * HALO, d), BF16), pltpu.VMEM((tm, d), F32)],
        compiler_params=_cparams(("parallel", "parallel", "arbitrary")),
        name="ffn",
    )(x4, x, x4, g.reshape(1, d), sh.reshape(b, 1, d), sc.reshape(b, 1, d), gate.reshape(b, 1, d),
      w_up, w_up, conv_w, conv_b.reshape(1, dff), w_down)


def _final_norm_kernel(x_ref, g_ref, o_ref):
    x = x_ref[0]
    o_ref[0] = x * lax.rsqrt(jnp.mean(x * x, axis=-1, keepdims=True) + EPS) * g_ref[...]


def _final_norm(x, g, tm):
    b, l, d = x.shape
    tm = min(tm, l)
    return pl.pallas_call(
        _final_norm_kernel,
        out_shape=jax.ShapeDtypeStruct((b, l, d), F32),
        grid=(b, l // tm),
        in_specs=[pl.BlockSpec((1, tm, d), lambda bi, i: (bi, i, 0)), pl.BlockSpec((1, d), lambda bi, i: (0, 0))],
        out_specs=pl.BlockSpec((1, tm, d), lambda bi, i: (bi, i, 0)),
        compiler_params=_cparams(("parallel", "parallel")),
        name="final_norm",
    )(x, g.reshape(1, d))


TOKEN_TILE = 512
PROJ_COL_TILE = 512
FFN_COL_TILE = 512
MIX_CHANNELS = 8


def kernel(x, c, ctx, c_ctx, ada_w, ada_b, norm1_g, norm2_g, w_in, na_rpb, hy_conv_w, hy_conv_b, hy_w1, hy_b1,
           hy_w2, hy_b2, hy_w3, hy_freq, hy_bias, mix_norm_g, w_out, ffn_w_up, ffn_conv_w, ffn_conv_b, ffn_w_down,
           final_norm_g):
    b, l, d = x.shape
    lc = ctx.shape[1]
    depth = ada_w.shape[0]
    na_w, fn_w = d // 2, d // 4
    hy_w = d - na_w - fn_w
    n_heads = na_w // HEAD_DIM
    qkv_w = 3 * na_w
    rows = l // GRID_W
    assert b + 1 <= 8 and l % (NA_ROWS_PER_BLOCK * GRID_W) == 0 and rows >= 2 * NA_ROWS_PER_BLOCK
    r = l // LANES
    nc = MIX_CHANNELS
    conv_tabs = _conv_tables(r, nc)

    s_in = jnp.concatenate([c, c_ctx[None], jnp.zeros((8 - b - 1, d), F32)], axis=0)
    mod_all = _ada_mod(s_in, ada_w, ada_b)
    xc = ctx
    for layer in range(depth):
        update_ctx = layer < depth - 1
        mod = mod_all[layer]
        sh1, sc1, g1, sh2, sc2, g2 = jnp.split(mod[:b], N_MOD, axis=-1)
        csh1, csc1, cg1, csh2, csc2, cg2 = jnp.split(jnp.broadcast_to(mod[b:b + 1], (b, N_MOD * d)), N_MOD, axis=-1)

        wl = w_in[layer]
        w_qkv = wl[:, :qkv_w].astype(BF16)
        w_fold = _fold_fourier_weights(wl[:, qkv_w:qkv_w + fn_w])
        w_cm = jnp.concatenate([w_fold[0].T, w_fold[1].T, wl[:, qkv_w + fn_w:].T], axis=0).astype(BF16)
        hy0 = 2 * fn_w

        qkv = _proj(x, norm1_g[layer], sh1, sc1, w_qkv, channel_major=False, tm=TOKEN_TILE, tn=PROJ_COL_TILE)
        pt = _proj(x, norm1_g[layer], sh1, sc1, w_cm, channel_major=True, tm=TOKEN_TILE, tn=PROJ_COL_TILE)
        qkv_c = _proj(xc, norm1_g[layer], csh1, csc1, w_qkv, channel_major=False, tm=TOKEN_TILE, tn=PROJ_COL_TILE)

        a = _na_attention(qkv, qkv_c, _na_bias_table(na_rpb[layer], rows), n_heads)
        yf = _fourier_latent(pt, fn_w, nc)
        filt = (hy_w1[layer], hy_b1[layer], hy_w2[layer], hy_b2[layer], hy_w3[layer], hy_freq[layer])
        spec = _hyena_spectra(_hyena_filters(l, *filt, hy_w), conv_tabs, nc)
        yh = _hyena_latent(pt, hy0, hy_w, spec, hy_conv_w[layer], hy_conv_b[layer], hy_bias[layer], conv_tabs, nc)
        x_new = _merge_out(a, yf, yh, x, g1, mix_norm_g[layer], w_out[layer], TOKEN_TILE)

        if update_ctx:
            ptc = _proj(xc, norm1_g[layer], csh1, csc1, w_cm, channel_major=True, tm=TOKEN_TILE, tn=PROJ_COL_TILE)
            ac = _ctx_attention(qkv_c, n_heads)
            yfc = _ctx_fourier(ptc, fn_w)
            yhc = _ctx_hyena(ptc, hy0, hy_w, _hyena_filters(lc, *filt, hy_w), hy_conv_w[layer], hy_conv_b[layer],
                             hy_bias[layer])
            xc = _merge_out(ac, yfc, yhc, xc, cg1, mix_norm_g[layer], w_out[layer], TOKEN_TILE)
        x = x_new

        w_up = ffn_w_up[layer].astype(BF16)
        w_down = ffn_w_down[layer].astype(BF16)
        x = _ffn(x, norm2_g[layer], sh2, sc2, g2, w_up, ffn_conv_w[layer], ffn_conv_b[layer], w_down,
                 TOKEN_TILE, FFN_COL_TILE)
        if update_ctx:
            xc = _ffn(xc, norm2_g[layer], csh2, csc2, cg2, w_up, ffn_conv_w[layer], ffn_conv_b[layer], w_down,
                      TOKEN_TILE, FFN_COL_TILE)
    return _final_norm(x, final_norm_g, TOKEN_TILE)
```

```python
import functools
import math

import numpy as np
import jax
import jax.numpy as jnp
from jax import lax
from jax.experimental import pallas as pl
from jax.experimental.pallas import tpu as pltpu

F32 = jnp.float32
BF16 = jnp.bfloat16

EPS = 1e-6
HEAD_DIM = 128
GRID_W = 64
NA_KH = 8
NA_KW = 16
NA_ROWS_PER_BLOCK = 8
FN_GROUP_DIM = 128
HY_ORDER = 2
HY_BANDS = 16
HY_FAST_DECAY = 0.3
HY_SLOW_DECAY = 1.5
HY_TARGET = 1e-2
HY_MIN_DECAY = math.log(HY_TARGET) / HY_SLOW_DECAY
HY_MAX_DECAY = math.log(HY_TARGET) / HY_FAST_DECAY
N_MOD = 6
LANES = 128
NEG_INF = -1e30
VMEM_LIMIT = 56 * 1024 * 1024

NT_DIMS = (((1,), (1,)), ((), ()))


def _cparams(sem):
    return pltpu.CompilerParams(dimension_semantics=sem, vmem_limit_bytes=VMEM_LIMIT)


def _hi_lo(a):
    hi = a.astype(BF16)
    lo = (a - hi.astype(F32)).astype(BF16)
    return hi, lo


def _np_hi_lo(m):
    m = np.asarray(m, np.float32)
    hi = m.astype(BF16)
    lo = (m - hi.astype(np.float32)).astype(BF16)
    return hi, lo


def _const_rhs3(m):
    hi, lo = _np_hi_lo(m)
    return jnp.asarray(np.concatenate([hi, hi, lo], axis=0))


def _const_lhs3(m):
    hi, lo = _np_hi_lo(m)
    return jnp.asarray(np.concatenate([hi, hi, lo], axis=1))


def _dot_data_const(a, c3):
    hi, lo = _hi_lo(a)
    return jnp.dot(jnp.concatenate([hi, lo, hi], axis=1), c3, preferred_element_type=F32)


def _dot_const_data(c3, b):
    hi, lo = _hi_lo(b)
    return jnp.dot(c3, jnp.concatenate([hi, lo, hi], axis=0), preferred_element_type=F32)


def _dot3(a, b):
    ah, al = _hi_lo(a)
    bh, bl = _hi_lo(b)
    return (jnp.dot(ah, bh, preferred_element_type=F32)
            + jnp.dot(al, bh, preferred_element_type=F32)
            + jnp.dot(ah, bl, preferred_element_type=F32))


def _dot3k(a, b):
    ah, al = _hi_lo(a)
    bh, bl = _hi_lo(b)
    return jnp.dot(jnp.concatenate([ah, al, ah], axis=1), jnp.concatenate([bh, bh, bl], axis=0),
                   preferred_element_type=F32)


def _const_bf16(m):
    return jnp.asarray(np.asarray(m, np.float32).astype(BF16))


def _dot1(a, b):
    return jnp.dot(a.astype(BF16), b.astype(BF16), preferred_element_type=F32)


def _ada_kernel(s_ref, w_ref, b_ref, o_ref):
    s = s_ref[...]
    s = s / (1.0 + jnp.exp(-s))
    o_ref[0] = _dot1(s, w_ref[0]) + b_ref[0]


def _ada_mod(s_in, ada_w, ada_b):
    depth, d, n = ada_w.shape
    tn = 1024
    return pl.pallas_call(
        _ada_kernel,
        out_shape=jax.ShapeDtypeStruct((depth, 8, n), F32),
        grid=(depth, n // tn),
        in_specs=[
            pl.BlockSpec((8, d), lambda l, j: (0, 0)),
            pl.BlockSpec((1, d, tn), lambda l, j: (l, 0, j)),
            pl.BlockSpec((1, 1, tn), lambda l, j: (l, 0, j)),
        ],
        out_specs=pl.BlockSpec((1, 8, tn), lambda l, j: (l, 0, j)),
        compiler_params=_cparams(("parallel", "parallel")),
        name="ada_mod",
    )(s_in, ada_w, ada_b.reshape(depth, 1, n))


def _norm_mod(x, g, sh, sc):
    ms = jnp.mean(x * x, axis=-1, keepdims=True)
    return (x * lax.rsqrt(ms + EPS) * g) * (1.0 + sc) + sh


NORM_SLABS = 4


def _proj_kernel(x_ref, g_ref, sh_ref, sc_ref, w_ref, otm_ref, ocm_ref, h_ref, *, n_tm):
    j = pl.program_id(2)
    tm = x_ref.shape[1]
    rs = tm // NORM_SLABS

    @pl.when(j == 0)
    def _():
        g, sh, sc = g_ref[...], sh_ref[0], sc_ref[0]
        w = w_ref[...]
        for s in range(NORM_SLABS):
            h = _norm_mod(x_ref[0, s * rs:(s + 1) * rs, :], g, sh, sc).astype(BF16)
            h_ref[s * rs:(s + 1) * rs, :] = h
            otm_ref[0, s * rs:(s + 1) * rs, :] = lax.dot_general(
                h, w, NT_DIMS, preferred_element_type=F32).astype(otm_ref.dtype)

    @pl.when((j > 0) & (j < n_tm))
    def _():
        otm_ref[0] = lax.dot_general(h_ref[...], w_ref[...], NT_DIMS,
                                     preferred_element_type=F32).astype(otm_ref.dtype)

    @pl.when(j >= n_tm)
    def _():
        ocm_ref[0] = lax.dot_general(w_ref[...], h_ref[...], NT_DIMS,
                                     preferred_element_type=F32).astype(ocm_ref.dtype)


def _proj(x, g, sh, sc, wt, n_tok, *, tm, tn):
    b, l, d = x.shape
    n = wt.shape[0]
    tm = min(tm, l)
    n_tm = n_tok // tn
    n_cm = (n - n_tok) // tn
    assert n_tm >= 1 and n_cm >= 1 and n_tm * tn == n_tok and tm % (16 * NORM_SLABS) == 0
    vec = pl.BlockSpec((1, 1, d), lambda bi, i, j: (bi, 0, 0))
    return pl.pallas_call(
        functools.partial(_proj_kernel, n_tm=n_tm),
        out_shape=(jax.ShapeDtypeStruct((b, l, n_tok), BF16), jax.ShapeDtypeStruct((b, n - n_tok, l), BF16)),
        grid=(b, l // tm, n_tm + n_cm),
        in_specs=[
            pl.BlockSpec((1, tm, d), lambda bi, i, j: (bi, i, 0)),
            pl.BlockSpec((1, d), lambda bi, i, j: (0, 0)),
            vec, vec,
            pl.BlockSpec((tn, d), lambda bi, i, j: (j, 0)),
        ],
        out_specs=(pl.BlockSpec((1, tm, tn), lambda bi, i, j: (bi, i, jnp.minimum(j, n_tm - 1))),
                   pl.BlockSpec((1, tn, tm), lambda bi, i, j: (bi, jnp.maximum(j - n_tm, 0), i))),
        scratch_shapes=[pltpu.VMEM((tm, d), BF16)],
        compiler_params=_cparams(("parallel", "parallel", "arbitrary")),
        name="proj_in",
    )(x, g.reshape(1, d), sh.reshape(b, 1, d), sc.reshape(b, 1, d), wt)


def _fold_kernel(w_ref, m_ref, o_ref):
    o_ref[0] = _dot_data_const(w_ref[...], m_ref[...])


def _fold_fourier_weights(w_f):
    d, width = w_f.shape
    gd = FN_GROUP_DIM
    idx = np.arange(gd)
    ang = 2.0 * np.pi * np.outer(idx, idx) / gd
    mat = np.concatenate([np.cos(ang), -np.sin(ang)], axis=1) / math.sqrt(gd)
    groups = width // gd
    return pl.pallas_call(
        _fold_kernel,
        out_shape=jax.ShapeDtypeStruct((2, d, width), F32),
        grid=(groups, 2),
        in_specs=[
            pl.BlockSpec((d, gd), lambda g, p: (0, g)),
            pl.BlockSpec((3 * gd, gd), lambda g, p: (0, p)),
        ],
        out_specs=pl.BlockSpec((1, d, gd), lambda g, p: (p, 0, g)),
        compiler_params=_cparams(("parallel", "parallel")),
        name="fold_fourier",
    )(w_f, _const_rhs3(mat))


def _na_bias_table(rpb, rows):
    h = rpb.shape[0]
    w, kw, kh, rb = GRID_W, NA_KW, NA_KH, NA_ROWS_PER_BLOCK
    col = np.arange(w)
    cs = np.clip(col - kw // 2, 0, w - kw)
    kc = np.arange(w)[None, :]
    col_ok = (kc >= cs[:, None]) & (kc < cs[:, None] + kw)
    padded = jnp.pad(rpb, ((0, 0), (0, 0), (w - kw, w - kw)))
    t1 = jnp.stack([padded[:, :, w - 1 - q:2 * w - 1 - q] for q in range(w)], axis=1)
    t1 = jnp.where(col_ok[None, :, None, :], t1, NEG_INF)
    slots = 2 * rb
    blocks = []
    for t, row0 in enumerate((0, rb, rows - rb)):
        per_row = []
        for qr in range(rb):
            r = row0 + qr
            r0 = r - kh // 2 if t == 1 else min(max(r - kh // 2, 0), rows - kh)
            valid = [kr for kr in range(slots)
                     if r0 <= row0 - rb // 2 + kr < r0 + kh and (t == 1 or 0 <= row0 - rb // 2 + kr < rows)]
            lo, hi = valid[0], valid[-1] + 1
            d0 = (row0 - rb // 2 + lo) - r + (kh - 1)
            piece = t1[:, :, d0:d0 + hi - lo, :]
            per_row.append(jnp.pad(piece, ((0, 0), (0, 0), (lo, slots - hi), (0, 0)), constant_values=NEG_INF))
        blocks.append(jnp.stack(per_row, axis=1))
    return jnp.stack(blocks, axis=0).reshape(3, h, rb * w, slots * w)


NA_HEADS_PER_STEP = 2


def _na_kernel(q_ref, kp_ref, kc_ref, kn_ref, vp_ref, vc_ref, vn_ref, kx_ref, vx_ref, bias_ref, o_ref):
    k_all = jnp.concatenate([kp_ref[0], kc_ref[0], kn_ref[0]], axis=0)
    v_all = jnp.concatenate([vp_ref[0], vc_ref[0], vn_ref[0]], axis=0)
    for h in range(NA_HEADS_PER_STEP):
        sl = slice(h * HEAD_DIM, (h + 1) * HEAD_DIM)
        q = q_ref[0, :, sl]
        s = lax.dot_general(q, k_all[:, sl], NT_DIMS, preferred_element_type=F32) + bias_ref[0, h]
        sx = lax.dot_general(q, kx_ref[0, :, sl], NT_DIMS, preferred_element_type=F32)
        m = jnp.maximum(jnp.max(s, axis=-1, keepdims=True), jnp.max(sx, axis=-1, keepdims=True))
        p = jnp.exp(s - m)
        px = jnp.exp(sx - m)
        den = jnp.sum(p, axis=-1, keepdims=True) + jnp.sum(px, axis=-1, keepdims=True)
        o = (jnp.dot(p.astype(BF16), v_all[:, sl], preferred_element_type=F32)
             + jnp.dot(px.astype(BF16), vx_ref[0, :, sl], preferred_element_type=F32))
        o_ref[0, :, sl] = (o / den).astype(o_ref.dtype)


def _na_attention(qkv, qkv_c, bias_tab, n_heads):
    b, l, _ = qkv.shape
    lc = qkv_c.shape[1]
    hps = NA_HEADS_PER_STEP
    wd = hps * HEAD_DIM
    tq = NA_ROWS_PER_BLOCK * GRID_W
    th = tq // 2
    nb = l // tq
    nh = l // th
    ng = n_heads // hps
    assert nb >= 2 and l % tq == 0 and n_heads % hps == 0

    def btype(i):
        return jnp.where(i == 0, 0, jnp.where(i == nb - 1, 2, 1))

    def cur(off):
        return pl.BlockSpec((1, tq, wd), lambda h, bi, i: (bi, i, off + h))

    def prev(off):
        return pl.BlockSpec((1, th, wd), lambda h, bi, i: (bi, jnp.maximum(2 * i - 1, 0), off + h))

    def nxt(off):
        return pl.BlockSpec((1, th, wd), lambda h, bi, i: (bi, jnp.minimum(2 * i + 2, nh - 1), off + h))

    def ctx(off):
        return pl.BlockSpec((1, lc, wd), lambda h, bi, i: (bi, 0, off + h))

    ko, vo = ng, 2 * ng
    return pl.pallas_call(
        _na_kernel,
        out_shape=jax.ShapeDtypeStruct((b, l, n_heads * HEAD_DIM), BF16),
        grid=(ng, b, nb),
        in_specs=[cur(0), prev(ko), cur(ko), nxt(ko), prev(vo), cur(vo), nxt(vo), ctx(ko), ctx(vo),
                  pl.BlockSpec((1, hps, tq, 2 * tq), lambda h, bi, i: (btype(i), h, 0, 0))],
        out_specs=pl.BlockSpec((1, tq, wd), lambda h, bi, i: (bi, i, h)),
        compiler_params=_cparams(("parallel", "parallel", "arbitrary")),
        name="na_attention",
    )(qkv, qkv, qkv, qkv, qkv, qkv, qkv, qkv_c, qkv_c, bias_tab)


def _ctx_attn_kernel(q_ref, k_ref, v_ref, o_ref):
    s = lax.dot_general(q_ref[0], k_ref[0], NT_DIMS, preferred_element_type=F32)
    m = jnp.max(s, axis=-1, keepdims=True)
    p = jnp.exp(s - m)
    den = jnp.sum(p, axis=-1, keepdims=True)
    o_ref[0] = (jnp.dot(p.astype(BF16), v_ref[0], preferred_element_type=F32) / den).astype(o_ref.dtype)


def _ctx_attention(qkv_c, n_heads):
    b, lc, _ = qkv_c.shape
    hd = HEAD_DIM

    def spec(off):
        return pl.BlockSpec((1, lc, hd), lambda bi, h: (bi, 0, off + h))

    return pl.pallas_call(
        _ctx_attn_kernel,
        out_shape=jax.ShapeDtypeStruct((b, lc, n_heads * hd), F32),
        grid=(b, n_heads),
        in_specs=[spec(0), spec(n_heads), spec(2 * n_heads)],
        out_specs=spec(0),
        compiler_params=_cparams(("parallel", "parallel")),
        name="ctx_attention",
    )(qkv_c, qkv_c, qkv_c)


def _cos_sin(n_out, n_in, period):
    ang = 2.0 * np.pi * ((np.arange(n_out)[:, None] * np.arange(n_in)[None, :]) % period) / period
    return np.cos(ang), np.sin(ang)


def _lane_cat(x3):
    return jnp.concatenate([x3[j] for j in range(x3.shape[0])], axis=1)


def _rows_from_lanes(re, im, nc):
    return jnp.concatenate(
        [jnp.concatenate([re[:, j * LANES:(j + 1) * LANES], im[:, j * LANES:(j + 1) * LANES]], axis=1)
         for j in range(nc)], axis=0)


def _lanes_from_rows(x, nc, r):
    re = jnp.concatenate([x[j * r:(j + 1) * r, :LANES] for j in range(nc)], axis=1)
    im = jnp.concatenate([x[j * r:(j + 1) * r, LANES:] for j in range(nc)], axis=1)
    return jnp.concatenate([re, im], axis=0)


def _fourier_tables(r, nc):
    l = r * LANES
    c1, s1 = _cos_sin(r, r, r)
    m1 = np.block([[c1, s1], [-s1, c1]])
    tc, ts = _cos_sin(r, LANES, l)
    c2, s2 = _cos_sin(LANES, LANES, LANES)
    m2 = np.concatenate([c2, s2], axis=0) / math.sqrt(l)
    return (_const_bf16(m1), jnp.asarray(np.tile(tc, (1, nc)), F32), jnp.asarray(np.tile(ts, (1, nc)), F32),
            _const_bf16(m2))


def _fourier_kernel(zr_ref, zi_ref, m1_ref, tc_ref, ts_ref, m2_ref, o_ref, *, nc, r):
    z = jnp.concatenate([_lane_cat(zr_ref[0]), _lane_cat(zi_ref[0])], axis=0)
    a = _dot1(m1_ref[...], z)
    ar, ai = a[:r], a[r:]
    tc, ts = tc_ref[...], ts_ref[...]
    br = (ar * tc + ai * ts).astype(BF16)
    bi = (ai * tc - ar * ts).astype(BF16)
    y = _dot1(_rows_from_lanes(br, bi, nc), m2_ref[...])
    for j in range(nc):
        o_ref[0, j] = y[j * r:(j + 1) * r].T.astype(o_ref.dtype)


def _fourier_latent(pt, n_ch, nc):
    b, _, l = pt.shape
    r = l // LANES
    p4 = pt.reshape(b, pt.shape[1], r, LANES)
    m1, tc, ts, m2 = _fourier_tables(r, nc)
    nblk = n_ch // nc

    def const(a):
        return pl.BlockSpec(a.shape, lambda bi, c: (0,) * a.ndim)

    out = pl.pallas_call(
        functools.partial(_fourier_kernel, nc=nc, r=r),
        out_shape=jax.ShapeDtypeStruct((b, n_ch, LANES, r), BF16),
        grid=(b, nblk),
        in_specs=[
            pl.BlockSpec((1, nc, r, LANES), lambda bi, c: (bi, c, 0, 0)),
            pl.BlockSpec((1, nc, r, LANES), lambda bi, c: (bi, nblk + c, 0, 0)),
            const(m1), const(tc), const(ts), const(m2),
        ],
        out_specs=pl.BlockSpec((1, nc, LANES, r), lambda bi, c: (bi, c, 0, 0)),
        compiler_params=_cparams(("parallel", "parallel")),
        name="fourier_latent",
    )(p4, p4, m1, tc, ts, m2)
    return out.reshape(b, n_ch, l)


def _filter_kernel(bands_ref, w1t_ref, w1c_ref, w1s_ref, b1_ref, w2_ref, b2_ref, w3_ref, fr_ref, dl_ref,
                   o_ref, *, l, tl):
    i = pl.program_id(0)
    n = (i * tl + lax.broadcasted_iota(jnp.int32, (1, tl), 1))
    pos_i = jnp.where(n < l, n, 2 * l - n)
    pos = pos_i.astype(F32)
    t = pos / float(max(l - 1, 1))
    ang = bands_ref[...] * (2.0 * math.pi / l) * pos
    fr = fr_ref[...]
    cos_a, sin_a = jnp.cos(ang), jnp.sin(ang)
    w1c, w1s = w1c_ref[...], w1s_ref[...]
    pre = w1t_ref[...] * t
    for k in range(HY_BANDS):
        pre = pre + w1c[:, k:k + 1] * cos_a[k:k + 1, :] - w1s[:, k:k + 1] * sin_a[k:k + 1, :]
    kpad = jnp.zeros((w2_ref.shape[1] - w2_ref.shape[0], tl), F32)
    h1 = jnp.sin(fr * (pre + b1_ref[...]))
    h2 = jnp.sin(fr * (_dot3k(w2_ref[...], jnp.concatenate([h1, kpad], axis=0)) + b2_ref[...]))
    out = _dot3k(w3_ref[0], jnp.concatenate([h2, kpad], axis=0)) * jnp.exp(-t * dl_ref[...])
    o_ref[...] = jnp.where(n == l, 0.0, out)


def _hyena_filters(l, w1, b1, w2, b2, w3, freq, width):
    hid = w1.shape[1]
    kdim = -(-hid // LANES) * LANES
    pad = kdim - hid
    tl = min(2048, l)
    rows = HY_ORDER * width
    bands = np.linspace(1e-4, HY_BANDS - 1, HY_BANDS, dtype=np.float32).reshape(HY_BANDS, 1)
    deltas = np.abs(np.linspace(HY_MIN_DECAY, HY_MAX_DECAY, width, dtype=np.float32))
    dl = np.tile(deltas, HY_ORDER).reshape(rows, 1)
    w1t = w1.T
    w2t = jnp.pad(w2.T, ((0, 0), (0, pad)))
    w3d = w3.reshape(hid, HY_ORDER, 2, width).transpose(2, 1, 3, 0).reshape(2, rows, hid)
    w3d = jnp.pad(w3d, ((0, 0), (0, 0), (0, pad)))
    col = lambda v: v.reshape(hid, 1)

    def const(shape):
        return pl.BlockSpec(shape, lambda i: (0,) * len(shape))

    return pl.pallas_call(
        functools.partial(_filter_kernel, l=l, tl=tl),
        out_shape=jax.ShapeDtypeStruct((rows, 2 * l), F32),
        grid=(2 * l // tl,),
        in_specs=[const((HY_BANDS, 1)), const((hid, 1)), const((hid, HY_BANDS)), const((hid, HY_BANDS)),
                  const((hid, 1)), const((hid, kdim)), const((hid, 1)),
                  pl.BlockSpec((1, rows, kdim), lambda i: (i // (l // tl), 0, 0)),
                  const((hid, 1)), const((rows, 1))],
        out_specs=pl.BlockSpec((rows, tl), lambda i: (0, i)),
        compiler_params=_cparams(("parallel",)),
        name="hyena_filters",
    )(jnp.asarray(bands), w1t[:, 0:1], w1t[:, 1:1 + HY_BANDS], w1t[:, 1 + HY_BANDS:], col(b1), w2t, col(b2),
      w3d, col(freq), jnp.asarray(dl))


def _conv_tables(r, nc):
    r2 = 2 * r
    n = r2 * LANES
    c1, s1 = _cos_sin(r2, r2, r2)
    c1h, s1h = c1[:, :r], s1[:, :r]
    m1 = np.block([[c1h, s1h], [-s1h, c1h]])
    m1_real = np.concatenate([c1, -s1], axis=0)
    tc, ts = _cos_sin(r2, LANES, n)
    c2, s2 = _cos_sin(LANES, LANES, LANES)
    m2 = np.block([[c2, -s2], [s2, c2]])
    m2i = np.block([[c2, s2], [-s2, c2]])
    ct, st = c1h.T, s1h.T
    m1i = np.block([[ct, -st], [st, ct]]) / n
    f32 = lambda a: jnp.asarray(a, F32)
    return dict(
        m1=_const_bf16(m1), m1_real=_const_bf16(m1_real), m2=_const_bf16(m2), m2i=_const_bf16(m2i),
        m1i=_const_bf16(m1i),
        tc_l=f32(np.tile(tc, (1, nc))), ts_l=f32(np.tile(ts, (1, nc))),
        tc_r=f32(np.tile(tc, (nc, 1))), ts_r=f32(np.tile(ts, (nc, 1))))


def _spec_kernel(k_ref, m1_ref, tc_ref, ts_ref, m2_ref, o_ref, *, nc, r2):
    a = _dot1(m1_ref[...], _lane_cat(k_ref[...]))
    ar, ai = a[:r2], a[r2:]
    tc, ts = tc_ref[...], ts_ref[...]
    br = (ar * tc + ai * ts).astype(BF16)
    bi = (ai * tc - ar * ts).astype(BF16)
    x = _dot1(_rows_from_lanes(br, bi, nc), m2_ref[...])
    o_ref[...] = x.reshape(nc, r2, 2 * LANES)


def _hyena_spectra(k2t, tabs, nc):
    c2n, n = k2t.shape
    r2 = n // LANES
    k3 = k2t.reshape(c2n, r2, LANES)

    def const(a):
        return pl.BlockSpec(a.shape, lambda c: (0,) * a.ndim)

    return pl.pallas_call(
        functools.partial(_spec_kernel, nc=nc, r2=r2),
        out_shape=jax.ShapeDtypeStruct((c2n, r2, 2 * LANES), F32),
        grid=(c2n // nc,),
        in_specs=[pl.BlockSpec((nc, r2, LANES), lambda c: (c, 0, 0)),
                  const(tabs["m1_real"]), const(tabs["tc_l"]), const(tabs["ts_l"]), const(tabs["m2"])],
        out_specs=pl.BlockSpec((nc, r2, 2 * LANES), lambda c: (c, 0, 0)),
        compiler_params=_cparams(("parallel",)),
        name="hyena_spectra",
    )(k3, tabs["m1_real"], tabs["tc_l"], tabs["ts_l"], tabs["m2"])


def _shift_tokens(t, r, direction):
    w = t.shape[1]
    lane = lax.broadcasted_iota(jnp.int32, t.shape, 1) & (LANES - 1)
    row = lax.broadcasted_iota(jnp.int32, t.shape, 0)
    if direction < 0:
        near = pltpu.roll(t, 1, 1)
        wrap = pltpu.roll(pltpu.roll(t, w - (LANES - 1), 1), 1, 0)
        edge = lane == 0
        dead = edge & (row == 0)
    else:
        near = pltpu.roll(t, w - 1, 1)
        wrap = pltpu.roll(pltpu.roll(t, LANES - 1, 1), r - 1, 0)
        edge = lane == LANES - 1
        dead = edge & (row == r - 1)
    return jnp.where(dead, 0.0, jnp.where(edge, wrap, near))


def _long_conv(zr, zi, kspec, m1, m2, m2i, m1i, tc_l, ts_l, tc_r, ts_r, nc, r):
    r2 = 2 * r
    a = _dot1(m1, jnp.concatenate([zr.astype(BF16), zi.astype(BF16)], axis=0))
    ar, ai = a[:r2], a[r2:]
    br = (ar * tc_l + ai * ts_l).astype(BF16)
    bi = (ai * tc_l - ar * ts_l).astype(BF16)
    x = _dot1(_rows_from_lanes(br, bi, nc), m2)
    xr, xi = x[:, :LANES], x[:, LANES:]
    kr, ki = kspec[:, :LANES], kspec[:, LANES:]
    y = jnp.concatenate([(xr * kr - xi * ki).astype(BF16), (xr * ki + xi * kr).astype(BF16)], axis=1)
    bb = _dot1(y, m2i)
    pr, pi = bb[:, :LANES], bb[:, LANES:]
    q = jnp.concatenate([(pr * tc_r - pi * ts_r).astype(BF16), (pi * tc_r + pr * ts_r).astype(BF16)], axis=1)
    out = _dot1(m1i, _lanes_from_rows(q, nc, r2))
    return out[:r], out[r:]


def _hyena_kernel(v0, v1, a0, a1, b0, b1, k0_ref, k1_ref, cw_ref, cb_ref, bias_ref,
                  m1_ref, m2_ref, m2i_ref, m1i_ref, tcl_ref, tsl_ref, tcr_ref, tsr_ref, o_ref, *, nc, r):
    tabs = (m1_ref[...], m2_ref[...], m2i_ref[...], m1i_ref[...],
            tcl_ref[...], tsl_ref[...], tcr_ref[...], tsr_ref[...])

    def short_conv(ref, part):
        t = _lane_cat(ref[0]).astype(F32)
        w = cw_ref[part]
        return (_shift_tokens(t, r, -1) * w[0:1] + t * w[1:2] + _shift_tokens(t, r, +1) * w[2:3]
                + cb_ref[part])

    vr, vi = short_conv(v0, 0), short_conv(v1, 0)
    x1r, x1i = short_conv(a0, 1), short_conv(a1, 1)
    x2r, x2i = short_conv(b0, 2), short_conv(b1, 2)
    bias = bias_ref[...]
    r2 = 2 * r
    yr, yi = _long_conv(vr, vi, k0_ref[...].reshape(nc * r2, 2 * LANES), *tabs, nc, r)
    zr = x1r * (yr + vr * bias[0:1])
    zi = x1i * (yi + vi * bias[0:1])
    yr, yi = _long_conv(zr, zi, k1_ref[...].reshape(nc * r2, 2 * LANES), *tabs, nc, r)
    outr = x2r * (yr + zr * bias[1:2])
    outi = x2i * (yi + zi * bias[1:2])
    for j in range(nc):
        o_ref[0, j] = outr[:, j * LANES:(j + 1) * LANES].astype(o_ref.dtype)
        o_ref[1, j] = outi[:, j * LANES:(j + 1) * LANES].astype(o_ref.dtype)


def _hyena_latent(pt, ch0, width, spec, conv_w, conv_b, bias, tabs, nc):
    b, c_all, l = pt.shape
    assert b == 2, "the two batch entries ride one complex transform"
    r = l // LANES
    r2 = 2 * r
    p4 = pt.reshape(b, c_all, r, LANES)
    nblk = width // nc
    rep = lambda a: jnp.repeat(a, LANES, axis=-1)
    cw = rep(conv_w.reshape(3, 3, width).transpose(1, 0, 2))
    cb = rep(conv_b.reshape(3, 1, width))
    bs = rep(bias)

    def inp(bi, part):
        off = (ch0 + part * width) // nc
        return pl.BlockSpec((1, nc, r, LANES), lambda c: (bi, off + c, 0, 0))

    def const(a):
        return pl.BlockSpec(a.shape, lambda c: (0,) * a.ndim)

    names = ("m1", "m2", "m2i", "m1i", "tc_l", "ts_l", "tc_r", "ts_r")
    consts = [tabs[k] for k in names]
    out = pl.pallas_call(
        functools.partial(_hyena_kernel, nc=nc, r=r),
        out_shape=jax.ShapeDtypeStruct((b, width, r, LANES), BF16),
        grid=(nblk,),
        in_specs=[inp(0, 0), inp(1, 0), inp(0, 1), inp(1, 1), inp(0, 2), inp(1, 2),
                  pl.BlockSpec((nc, r2, 2 * LANES), lambda c: (c, 0, 0)),
                  pl.BlockSpec((nc, r2, 2 * LANES), lambda c: (nblk + c, 0, 0)),
                  pl.BlockSpec((3, 3, nc * LANES), lambda c: (0, 0, c)),
                  pl.BlockSpec((3, 1, nc * LANES), lambda c: (0, 0, c)),
                  pl.BlockSpec((HY_ORDER, nc * LANES), lambda c: (0, c))]
                 + [const(a) for a in consts],
        out_specs=pl.BlockSpec((b, nc, r, LANES), lambda c: (0, c, 0, 0)),
        compiler_params=_cparams(("parallel",)),
        name="hyena_latent",
    )(p4, p4, p4, p4, p4, p4, spec, spec, cw, cb, bs, *consts)
    return out.reshape(b, width, l)


def _ctx_fourier_kernel(zr_ref, zi_ref, m_ref, o_ref):
    z = jnp.concatenate([zr_ref[0], zi_ref[0]], axis=1).astype(F32)
    o_ref[0] = _dot_data_const(z, m_ref[...])


def _ctx_fourier(ptc, n_ch):
    b, _, lc = ptc.shape
    c, s = _cos_sin(lc, lc, lc)
    m = _const_rhs3(np.concatenate([c, s], axis=0) / math.sqrt(lc))
    return pl.pallas_call(
        _ctx_fourier_kernel,
        out_shape=jax.ShapeDtypeStruct((b, n_ch, lc), F32),
        grid=(b,),
        in_specs=[pl.BlockSpec((1, n_ch, lc), lambda bi: (bi, 0, 0)),
                  pl.BlockSpec((1, n_ch, lc), lambda bi: (bi, 1, 0)),
                  pl.BlockSpec(m.shape, lambda bi: (0, 0))],
        out_specs=pl.BlockSpec((1, n_ch, lc), lambda bi: (bi, 0, 0)),
        compiler_params=_cparams(("parallel",)),
        name="ctx_fourier",
    )(ptc, ptc, m)


def _ctx_hyena_kernel(v0, v1, a0, a1, b0, b1, k_ref, cw_ref, cb_ref, bias_ref, mk_ref, mf_ref, mi_ref,
                      o_ref, *, lc, width):
    def short_conv(ref, part):
        t = ref[0].astype(F32)
        lane = lax.broadcasted_iota(jnp.int32, t.shape, 1)
        prv = jnp.where(lane == 0, 0.0, pltpu.roll(t, 1, 1))
        nxt = jnp.where(lane == lc - 1, 0.0, pltpu.roll(t, lc - 1, 1))
        w = cw_ref[part]
        return prv * w[:, 0:1] + t * w[:, 1:2] + nxt * w[:, 2:3] + cb_ref[part]

    kspec = _dot_data_const(k_ref[...], mk_ref[...])
    n = 2 * lc

    def long_conv(zr, zi, ks):
        x = _dot_data_const(jnp.concatenate([zr, zi], axis=1), mf_ref[...])
        xr, xi = x[:, :n], x[:, n:]
        kr, ki = ks[:, :n], ks[:, n:]
        y = jnp.concatenate([xr * kr - xi * ki, xr * ki + xi * kr], axis=1)
        out = _dot_data_const(y, mi_ref[...])
        return out[:, :lc], out[:, lc:]

    vr, vi = short_conv(v0, 0), short_conv(v1, 0)
    x1r, x1i = short_conv(a0, 1), short_conv(a1, 1)
    x2r, x2i = short_conv(b0, 2), short_conv(b1, 2)
    bias = bias_ref[...]
    yr, yi = long_conv(vr, vi, kspec[:width])
    zr = x1r * (yr + vr * bias[:, 0:1])
    zi = x1i * (yi + vi * bias[:, 0:1])
    yr, yi = long_conv(zr, zi, kspec[width:])
    o_ref[0] = x2r * (yr + zr * bias[:, 1:2])
    o_ref[1] = x2i * (yi + zi * bias[:, 1:2])


def _ctx_hyena(ptc, ch0, width, k2t, conv_w, conv_b, bias):
    b, _, lc = ptc.shape
    assert b == 2
    n = 2 * lc
    c, s = _cos_sin(n, n, n)
    mk = _const_rhs3(np.concatenate([c, -s], axis=1))
    ch, sh = c[:lc], s[:lc]
    mf = _const_rhs3(np.block([[ch, -sh], [sh, ch]]))
    ci, si = c[:, :lc], s[:, :lc]
    mi = _const_rhs3(np.block([[ci, si], [-si, ci]]) / n)
    cw = conv_w.reshape(3, 3, width).transpose(1, 2, 0)
    cb = conv_b.reshape(3, width, 1)

    def inp(bi, part):
        return pl.BlockSpec((1, width, lc), lambda i: (bi, ch0 // width + part, 0))

    def const(a):
        return pl.BlockSpec(a.shape, lambda i: (0,) * a.ndim)

    args = (k2t, cw, cb, bias.T, mk, mf, mi)
    return pl.pallas_call(
        functools.partial(_ctx_hyena_kernel, lc=lc, width=width),
        out_shape=jax.ShapeDtypeStruct((b, width, lc), F32),
        grid=(1,),
        in_specs=[inp(0, 0), inp(1, 0), inp(0, 1), inp(1, 1), inp(0, 2), inp(1, 2)] + [const(a) for a in args],
        out_specs=pl.BlockSpec((b, width, lc), lambda i: (0, 0, 0)),
        compiler_params=_cparams(("arbitrary",)),
        name="ctx_hyena",
    )(ptc, ptc, ptc, ptc, ptc, ptc, *args)


def _merge_kernel(a_ref, f_ref, hy_ref, x_ref, gate_ref, ga_ref, gf_ref, gh_ref, wa_ref, wf_ref, wh_ref, o_ref):
    a = a_ref[0].astype(F32)
    ya = a * lax.rsqrt(jnp.mean(a * a, axis=-1, keepdims=True) + EPS) * ga_ref[...]
    acc = jnp.dot(ya.astype(BF16), wa_ref[...], preferred_element_type=F32)

    def cm_part(ref, g_ref, w_ref):
        t = ref[0].astype(F32)
        y = t * lax.rsqrt(jnp.mean(t * t, axis=0, keepdims=True) + EPS) * g_ref[...]
        return jnp.dot(y.T.astype(BF16), w_ref[...], preferred_element_type=F32)

    acc = acc + cm_part(f_ref, gf_ref, wf_ref) + cm_part(hy_ref, gh_ref, wh_ref)
    o_ref[0] = x_ref[0] + gate_ref[0] * acc


def _merge_out(a, ft, ht, x, gate, g, w_out, tm):
    b, l, d = x.shape
    wa_n, wf_n, wh_n = a.shape[2], ft.shape[1], ht.shape[1]
    tm = min(tm, l)
    w = w_out.astype(BF16)
    wa, wf, wh = w[:wa_n], w[wa_n:wa_n + wf_n], w[wa_n + wf_n:]
    ga = g[:wa_n].reshape(1, wa_n)
    gf = g[wa_n:wa_n + wf_n].reshape(wf_n, 1)
    gh = g[wa_n + wf_n:].reshape(wh_n, 1)

    def const(arr):
        return pl.BlockSpec(arr.shape, lambda bi, i: (0,) * arr.ndim)

    return pl.pallas_call(
        _merge_kernel,
        out_shape=jax.ShapeDtypeStruct((b, l, d), F32),
        grid=(b, l // tm),
        in_specs=[pl.BlockSpec((1, tm, wa_n), lambda bi, i: (bi, i, 0)),
                  pl.BlockSpec((1, wf_n, tm), lambda bi, i: (bi, 0, i)),
                  pl.BlockSpec((1, wh_n, tm), lambda bi, i: (bi, 0, i)),
                  pl.BlockSpec((1, tm, d), lambda bi, i: (bi, i, 0)),
                  pl.BlockSpec((1, 1, d), lambda bi, i: (bi, 0, 0)),
                  const(ga), const(gf), const(gh), const(wa), const(wf), const(wh)],
        out_specs=pl.BlockSpec((1, tm, d), lambda bi, i: (bi, i, 0)),
        compiler_params=_cparams(("parallel", "parallel")),
        name="merge_out",
    )(a, ft, ht, x, gate.reshape(b, 1, d), ga, gf, gh, wa, wf, wh)


HALO = 16


def _ffn_kernel(xp_ref, x_ref, xn_ref, g_ref, sh_ref, sc_ref, gate_ref, wg_ref, wu_ref, cw_ref, cb_ref, wd_ref,
                fg_ref, o_ref, h_ref, acc_ref, *, tm, final_norm):
    i = pl.program_id(1)
    c = pl.program_id(2)
    last_tile = pl.num_programs(1) - 1

    n_ext = tm + 2 * HALO

    def down_partial(ge, up):
        row = lax.broadcasted_iota(jnp.int32, ge.shape, 0)
        outside = ((row < HALO) & (i == 0)) | ((row >= HALO + tm) & (i == last_tile))
        ge = jnp.where(outside, 0.0, ge)
        gp = pltpu.roll(ge, 1, 0)[HALO:HALO + tm]
        gn = pltpu.roll(ge, n_ext - 1, 0)[HALO:HALO + tm]
        cw = cw_ref[...]
        conv = gp * cw[0:1] + ge[HALO:HALO + tm] * cw[1:2] + gn * cw[2:3] + cb_ref[...]
        inner = 0.7978845608028654 * (conv + 0.044715 * (conv * conv * conv))
        act = 0.5 * conv * (1.0 + jnp.tanh(inner)) * up
        return jnp.dot(act.astype(BF16), wd_ref[...], preferred_element_type=F32)

    @pl.when(c == 0)
    def _():
        g, sh, sc = g_ref[...], sh_ref[0], sc_ref[0]
        wg, wu = wg_ref[...], wu_ref[...]
        rs = tm // NORM_SLABS
        bounds = [(0, HALO)] + [(HALO + s * rs, HALO + (s + 1) * rs) for s in range(NORM_SLABS)] + [(HALO + tm, n_ext)]
        ge_parts, up_parts = [], []
        for lo, hi in bounds:
            if lo == 0:
                xs = xp_ref[0, 0]
            elif hi == n_ext:
                xs = xn_ref[0, 0]
            else:
                xs = x_ref[0, lo - HALO:hi - HALO, :]
            h = _norm_mod(xs, g, sh, sc).astype(BF16)
            h_ref[lo:hi, :] = h
            ge_parts.append(jnp.dot(h, wg, preferred_element_type=F32))
            if lo != 0 and hi != n_ext:
                up_parts.append(jnp.dot(h, wu, preferred_element_type=F32))
        acc_ref[...] = down_partial(jnp.concatenate(ge_parts, axis=0), jnp.concatenate(up_parts, axis=0))

    @pl.when(c > 0)
    def _():
        h = h_ref[...]
        ge = jnp.dot(h, wg_ref[...], preferred_element_type=F32)
        up = jnp.dot(h[HALO:HALO + tm], wu_ref[...], preferred_element_type=F32)
        acc_ref[...] += down_partial(ge, up)

    @pl.when(c == pl.num_programs(2) - 1)
    def _():
        y = x_ref[0] + gate_ref[0] * acc_ref[...]
        if final_norm:
            y = y * lax.rsqrt(jnp.mean(y * y, axis=-1, keepdims=True) + EPS) * fg_ref[...]
        o_ref[0] = y


def _ffn(x, g, sh, sc, gate, w_up, conv_w, conv_b, w_down, final_g, tm, tf, *, final_norm):
    b, l, d = x.shape
    dff = w_down.shape[0]
    tm = min(tm, l)
    nch = dff // tf
    x4 = x.reshape(b, l // HALO, HALO, d)
    per_tile = tm // HALO
    nhalo = l // HALO
    vec = pl.BlockSpec((1, 1, d), lambda bi, i, c: (bi, 0, 0))
    return pl.pallas_call(
        functools.partial(_ffn_kernel, tm=tm, final_norm=final_norm),
        out_shape=jax.ShapeDtypeStruct((b, l, d), F32),
        grid=(b, l // tm, nch),
        in_specs=[
            pl.BlockSpec((1, 1, HALO, d), lambda bi, i, c: (bi, jnp.maximum(i * per_tile - 1, 0), 0, 0)),
            pl.BlockSpec((1, tm, d), lambda bi, i, c: (bi, i, 0)),
            pl.BlockSpec((1, 1, HALO, d), lambda bi, i, c: (bi, jnp.minimum((i + 1) * per_tile, nhalo - 1), 0, 0)),
            pl.BlockSpec((1, d), lambda bi, i, c: (0, 0)),
            vec, vec, vec,
            pl.BlockSpec((d, tf), lambda bi, i, c: (0, c)),
            pl.BlockSpec((d, tf), lambda bi, i, c: (0, nch + c)),
            pl.BlockSpec((3, tf), lambda bi, i, c: (0, c)),
            pl.BlockSpec((1, tf), lambda bi, i, c: (0, c)),
            pl.BlockSpec((tf, d), lambda bi, i, c: (c, 0)),
            pl.BlockSpec((1, d), lambda bi, i, c: (0, 0)),
        ],
        out_specs=pl.BlockSpec((1, tm, d), lambda bi, i, c: (bi, i, 0)),
        scratch_shapes=[pltpu.VMEM((tm + 2 * HALO, d), BF16), pltpu.VMEM((tm, d), F32)],
        compiler_params=_cparams(("parallel", "parallel", "arbitrary")),
        name="ffn",
    )(x4, x, x4, g.reshape(1, d), sh.reshape(b, 1, d), sc.reshape(b, 1, d), gate.reshape(b, 1, d),
      w_up, w_up, conv_w, conv_b.reshape(1, dff), w_down, final_g.reshape(1, d))


TOKEN_TILE = 512
PROJ_TOKEN_TILE = 1024
PROJ_COL_TILE = 512
FFN_COL_TILE = 512
MIX_CHANNELS = 8


def kernel(x, c, ctx, c_ctx, ada_w, ada_b, norm1_g, norm2_g, w_in, na_rpb, hy_conv_w, hy_conv_b, hy_w1, hy_b1,
           hy_w2, hy_b2, hy_w3, hy_freq, hy_bias, mix_norm_g, w_out, ffn_w_up, ffn_conv_w, ffn_conv_b, ffn_w_down,
           final_norm_g):
    b, l, d = x.shape
    lc = ctx.shape[1]
    depth = ada_w.shape[0]
    na_w, fn_w = d // 2, d // 4
    hy_w = d - na_w - fn_w
    n_heads = na_w // HEAD_DIM
    qkv_w = 3 * na_w
    rows = l // GRID_W
    assert b + 1 <= 8 and l % (NA_ROWS_PER_BLOCK * GRID_W) == 0 and rows >= 2 * NA_ROWS_PER_BLOCK
    r = l // LANES
    nc = MIX_CHANNELS
    conv_tabs = _conv_tables(r, nc)

    s_in = jnp.concatenate([c, c_ctx[None], jnp.zeros((8 - b - 1, d), F32)], axis=0)
    mod_all = _ada_mod(s_in, ada_w, ada_b)
    xc = ctx
    for layer in range(depth):
        update_ctx = layer < depth - 1
        mod = mod_all[layer]
        sh1, sc1, g1, sh2, sc2, g2 = jnp.split(mod[:b], N_MOD, axis=-1)
        csh1, csc1, cg1, csh2, csc2, cg2 = jnp.split(jnp.broadcast_to(mod[b:b + 1], (b, N_MOD * d)), N_MOD, axis=-1)

        wl = w_in[layer]
        w_fold = _fold_fourier_weights(wl[:, qkv_w:qkv_w + fn_w])
        wt = jnp.concatenate([(wl[:, :na_w] * HEAD_DIM ** -0.5).T, wl[:, na_w:qkv_w].T, w_fold[0].T, w_fold[1].T,
                              wl[:, qkv_w + fn_w:].T], axis=0).astype(BF16)
        hy0 = 2 * fn_w

        qkv, pt = _proj(x, norm1_g[layer], sh1, sc1, wt, qkv_w, tm=PROJ_TOKEN_TILE, tn=PROJ_COL_TILE)
        qkv_c, ptc = _proj(xc, norm1_g[layer], csh1, csc1, wt, qkv_w, tm=PROJ_TOKEN_TILE, tn=PROJ_COL_TILE)

        a = _na_attention(qkv, qkv_c, _na_bias_table(na_rpb[layer], rows), n_heads)
        yf = _fourier_latent(pt, fn_w, nc)
        filt = (hy_w1[layer], hy_b1[layer], hy_w2[layer], hy_b2[layer], hy_w3[layer], hy_freq[layer])
        spec = _hyena_spectra(_hyena_filters(l, *filt, hy_w), conv_tabs, nc)
        yh = _hyena_latent(pt, hy0, hy_w, spec, hy_conv_w[layer], hy_conv_b[layer], hy_bias[layer], conv_tabs, nc)
        x_new = _merge_out(a, yf, yh, x, g1, mix_norm_g[layer], w_out[layer], TOKEN_TILE)

        if update_ctx:
            ac = _ctx_attention(qkv_c, n_heads)
            yfc = _ctx_fourier(ptc, fn_w)
            yhc = _ctx_hyena(ptc, hy0, hy_w, _hyena_filters(lc, *filt, hy_w), hy_conv_w[layer], hy_conv_b[layer],
                             hy_bias[layer])
            xc = _merge_out(ac, yfc, yhc, xc, cg1, mix_norm_g[layer], w_out[layer], TOKEN_TILE)
        x = x_new

        w_up = ffn_w_up[layer].astype(BF16)
        w_down = ffn_w_down[layer].astype(BF16)
        x = _ffn(x, norm2_g[layer], sh2, sc2, g2, w_up, ffn_conv_w[layer], ffn_conv_b[layer], w_down,
                 final_norm_g, TOKEN_TILE, FFN_COL_TILE, final_norm=not update_ctx)
        if update_ctx:
            xc = _ffn(xc, norm2_g[layer], csh2, csc2, cg2, w_up, ffn_conv_w[layer], ffn_conv_b[layer], w_down,
                      final_norm_g, TOKEN_TILE, FFN_COL_TILE, final_norm=False)
    return x
```

```python
import functools
import math

import numpy as np
import jax
import jax.numpy as jnp
from jax import lax
from jax.experimental import pallas as pl
from jax.experimental.pallas import tpu as pltpu

F32 = jnp.float32
BF16 = jnp.bfloat16

EPS = 1e-6
HEAD_DIM = 128
GRID_W = 64
NA_KH = 8
NA_KW = 16
NA_ROWS_PER_BLOCK = 8
FN_GROUP_DIM = 128
HY_ORDER = 2
HY_BANDS = 16
HY_FAST_DECAY = 0.3
HY_SLOW_DECAY = 1.5
HY_TARGET = 1e-2
HY_MIN_DECAY = math.log(HY_TARGET) / HY_SLOW_DECAY
HY_MAX_DECAY = math.log(HY_TARGET) / HY_FAST_DECAY
N_MOD = 6
LANES = 128
NEG_INF = -1e30
LOG2E = math.log2(math.e)
VMEM_LIMIT = 56 * 1024 * 1024

NT_DIMS = (((1,), (1,)), ((), ()))


def _cparams(sem):
    return pltpu.CompilerParams(dimension_semantics=sem, vmem_limit_bytes=VMEM_LIMIT)


def _hi_lo(a):
    hi = a.astype(BF16)
    lo = (a - hi.astype(F32)).astype(BF16)
    return hi, lo


def _np_hi_lo(m):
    m = np.asarray(m, np.float32)
    hi = m.astype(BF16)
    lo = (m - hi.astype(np.float32)).astype(BF16)
    return hi, lo


def _const_rhs3(m):
    hi, lo = _np_hi_lo(m)
    return jnp.asarray(np.concatenate([hi, hi, lo], axis=0))


def _const_lhs3(m):
    hi, lo = _np_hi_lo(m)
    return jnp.asarray(np.concatenate([hi, hi, lo], axis=1))


def _dot_data_const(a, c3):
    hi, lo = _hi_lo(a)
    return jnp.dot(jnp.concatenate([hi, lo, hi], axis=1), c3, preferred_element_type=F32)


def _dot_const_data(c3, b):
    hi, lo = _hi_lo(b)
    return jnp.dot(c3, jnp.concatenate([hi, lo, hi], axis=0), preferred_element_type=F32)


def _dot3(a, b):
    ah, al = _hi_lo(a)
    bh, bl = _hi_lo(b)
    return (jnp.dot(ah, bh, preferred_element_type=F32)
            + jnp.dot(al, bh, preferred_element_type=F32)
            + jnp.dot(ah, bl, preferred_element_type=F32))


def _dot3k(a, b):
    ah, al = _hi_lo(a)
    bh, bl = _hi_lo(b)
    return jnp.dot(jnp.concatenate([ah, al, ah], axis=1), jnp.concatenate([bh, bh, bl], axis=0),
                   preferred_element_type=F32)


def _const_bf16(m):
    return jnp.asarray(np.asarray(m, np.float32).astype(BF16))


def _dot1(a, b):
    return jnp.dot(a.astype(BF16), b.astype(BF16), preferred_element_type=F32)


def _ada_kernel(s_ref, w_ref, b_ref, o_ref):
    s = s_ref[...]
    s = s / (1.0 + jnp.exp(-s))
    o_ref[0] = _dot1(s, w_ref[0]) + b_ref[0]


def _ada_mod(s_in, ada_w, ada_b):
    depth, d, n = ada_w.shape
    tn = 1024
    return pl.pallas_call(
        _ada_kernel,
        out_shape=jax.ShapeDtypeStruct((depth, 8, n), F32),
        grid=(depth, n // tn),
        in_specs=[
            pl.BlockSpec((8, d), lambda l, j: (0, 0)),
            pl.BlockSpec((1, d, tn), lambda l, j: (l, 0, j)),
            pl.BlockSpec((1, 1, tn), lambda l, j: (l, 0, j)),
        ],
        out_specs=pl.BlockSpec((1, 8, tn), lambda l, j: (l, 0, j)),
        compiler_params=_cparams(("parallel", "parallel")),
        name="ada_mod",
    )(s_in, ada_w, ada_b.reshape(depth, 1, n))


def _norm_mod(x, g, sh, sc):
    ms = jnp.mean(x * x, axis=-1, keepdims=True)
    return (x * lax.rsqrt(ms + EPS) * g) * (1.0 + sc) + sh


NORM_SLABS = 4


def _proj_kernel(x_ref, g_ref, sh_ref, sc_ref, w_ref, otm_ref, ocm_ref, h_ref, *, n_tm):
    j = pl.program_id(2)
    tm = x_ref.shape[1]
    rs = tm // NORM_SLABS

    @pl.when(j == 0)
    def _():
        g, sh, sc = g_ref[...], sh_ref[0], sc_ref[0]
        w = w_ref[...]
        for s in range(NORM_SLABS):
            h = _norm_mod(x_ref[0, s * rs:(s + 1) * rs, :], g, sh, sc).astype(BF16)
            h_ref[s * rs:(s + 1) * rs, :] = h
            otm_ref[0, s * rs:(s + 1) * rs, :] = lax.dot_general(
                h, w, NT_DIMS, preferred_element_type=F32).astype(otm_ref.dtype)

    @pl.when((j > 0) & (j < n_tm))
    def _():
        otm_ref[0] = lax.dot_general(h_ref[...], w_ref[...], NT_DIMS,
                                     preferred_element_type=F32).astype(otm_ref.dtype)

    @pl.when(j >= n_tm)
    def _():
        ocm_ref[0] = lax.dot_general(w_ref[...], h_ref[...], NT_DIMS,
                                     preferred_element_type=F32).astype(ocm_ref.dtype)


def _proj(x, g, sh, sc, wt, n_tok, *, tm, tn):
    b, l, d = x.shape
    n = wt.shape[0]
    tm = min(tm, l)
    n_tm = n_tok // tn
    n_cm = (n - n_tok) // tn
    assert n_tm >= 1 and n_cm >= 1 and n_tm * tn == n_tok and tm % (16 * NORM_SLABS) == 0
    vec = pl.BlockSpec((1, 1, d), lambda bi, i, j: (bi, 0, 0))
    return pl.pallas_call(
        functools.partial(_proj_kernel, n_tm=n_tm),
        out_shape=(jax.ShapeDtypeStruct((b, l, n_tok), BF16), jax.ShapeDtypeStruct((b, n - n_tok, l), BF16)),
        grid=(b, l // tm, n_tm + n_cm),
        in_specs=[
            pl.BlockSpec((1, tm, d), lambda bi, i, j: (bi, i, 0)),
            pl.BlockSpec((1, d), lambda bi, i, j: (0, 0)),
            vec, vec,
            pl.BlockSpec((tn, d), lambda bi, i, j: (j, 0)),
        ],
        out_specs=(pl.BlockSpec((1, tm, tn), lambda bi, i, j: (bi, i, jnp.minimum(j, n_tm - 1))),
                   pl.BlockSpec((1, tn, tm), lambda bi, i, j: (bi, jnp.maximum(j - n_tm, 0), i))),
        scratch_shapes=[pltpu.VMEM((tm, d), BF16)],
        compiler_params=_cparams(("parallel", "parallel", "arbitrary")),
        name="proj_in",
    )(x, g.reshape(1, d), sh.reshape(b, 1, d), sc.reshape(b, 1, d), wt)


def _fold_kernel(w_ref, m_ref, o_ref):
    o_ref[0] = _dot_data_const(w_ref[...], m_ref[...])


def _fold_fourier_weights(w_f):
    d, width = w_f.shape
    gd = FN_GROUP_DIM
    idx = np.arange(gd)
    ang = 2.0 * np.pi * np.outer(idx, idx) / gd
    mat = np.concatenate([np.cos(ang), -np.sin(ang)], axis=1) / math.sqrt(gd)
    groups = width // gd
    return pl.pallas_call(
        _fold_kernel,
        out_shape=jax.ShapeDtypeStruct((2, d, width), F32),
        grid=(groups, 2),
        in_specs=[
            pl.BlockSpec((d, gd), lambda g, p: (0, g)),
            pl.BlockSpec((3 * gd, gd), lambda g, p: (0, p)),
        ],
        out_specs=pl.BlockSpec((1, d, gd), lambda g, p: (p, 0, g)),
        compiler_params=_cparams(("parallel", "parallel")),
        name="fold_fourier",
    )(w_f, _const_rhs3(mat))


def _na_bias_table(rpb, rows):
    h = rpb.shape[0]
    w, kw, kh, rb = GRID_W, NA_KW, NA_KH, NA_ROWS_PER_BLOCK
    col = np.arange(w)
    cs = np.clip(col - kw // 2, 0, w - kw)
    kc = np.arange(w)[None, :]
    col_ok = (kc >= cs[:, None]) & (kc < cs[:, None] + kw)
    padded = jnp.pad(rpb * LOG2E, ((0, 0), (0, 0), (w - kw, w - kw)))
    t1 = jnp.stack([padded[:, :, w - 1 - q:2 * w - 1 - q] for q in range(w)], axis=1)
    t1 = jnp.where(col_ok[None, :, None, :], t1, NEG_INF)
    slots = 2 * rb
    blocks = []
    for t, row0 in enumerate((0, rb, rows - rb)):
        per_row = []
        for qr in range(rb):
            r = row0 + qr
            r0 = r - kh // 2 if t == 1 else min(max(r - kh // 2, 0), rows - kh)
            valid = [kr for kr in range(slots)
                     if r0 <= row0 - rb // 2 + kr < r0 + kh and (t == 1 or 0 <= row0 - rb // 2 + kr < rows)]
            lo, hi = valid[0], valid[-1] + 1
            d0 = (row0 - rb // 2 + lo) - r + (kh - 1)
            piece = t1[:, :, d0:d0 + hi - lo, :]
            per_row.append(jnp.pad(piece, ((0, 0), (0, 0), (lo, slots - hi), (0, 0)), constant_values=NEG_INF))
        blocks.append(jnp.stack(per_row, axis=1))
    return jnp.stack(blocks, axis=0).reshape(3, h, rb * w, slots * w)


NA_HEADS_PER_STEP = 4
NA_QUERY_SPLIT = 2


def _na_kernel(q_ref, kp_ref, kc_ref, kn_ref, vp_ref, vc_ref, vn_ref, kx_ref, vx_ref, bias_ref, o_ref):
    k_all = jnp.concatenate([kp_ref[0], kc_ref[0], kn_ref[0]], axis=0)
    v_all = jnp.concatenate([vp_ref[0], vc_ref[0], vn_ref[0]], axis=0)
    tq = q_ref.shape[1]
    rq = tq // NA_QUERY_SPLIT
    assert NA_QUERY_SPLIT in (1, 2)
    band = (NA_ROWS_PER_BLOCK // NA_QUERY_SPLIT + NA_KH) * GRID_W
    for h in range(NA_HEADS_PER_STEP):
        sl = slice(h * HEAD_DIM, (h + 1) * HEAD_DIM)
        kx, vx = kx_ref[0, :, sl], vx_ref[0, :, sl]
        for part in range(NA_QUERY_SPLIT):
            rows = slice(part * rq, (part + 1) * rq)
            keys = slice(part * rq, part * rq + band)
            q = q_ref[0, rows, sl]
            s = lax.dot_general(q, k_all[keys, sl], NT_DIMS, preferred_element_type=F32) + bias_ref[0, h, rows, keys]
            sx = lax.dot_general(q, kx, NT_DIMS, preferred_element_type=F32)
            m = jnp.maximum(jnp.max(s, axis=-1, keepdims=True), jnp.max(sx, axis=-1, keepdims=True))
            p = jnp.exp2(s - m)
            px = jnp.exp2(sx - m)
            den = jnp.sum(p, axis=-1, keepdims=True) + jnp.sum(px, axis=-1, keepdims=True)
            o = (jnp.dot(p.astype(BF16), v_all[keys, sl], preferred_element_type=F32)
                 + jnp.dot(px.astype(BF16), vx, preferred_element_type=F32))
            o_ref[0, rows, sl] = (o / den).astype(o_ref.dtype)


def _na_attention(qkv, qkv_c, bias_tab, n_heads):
    b, l, _ = qkv.shape
    lc = qkv_c.shape[1]
    hps = NA_HEADS_PER_STEP
    wd = hps * HEAD_DIM
    tq = NA_ROWS_PER_BLOCK * GRID_W
    th = tq // 2
    nb = l // tq
    nh = l // th
    ng = n_heads // hps
    assert nb >= 2 and l % tq == 0 and n_heads % hps == 0

    def btype(i):
        return jnp.where(i == 0, 0, jnp.where(i == nb - 1, 2, 1))

    def cur(off):
        return pl.BlockSpec((1, tq, wd), lambda h, bi, i: (bi, i, off + h))

    def prev(off):
        return pl.BlockSpec((1, th, wd), lambda h, bi, i: (bi, jnp.maximum(2 * i - 1, 0), off + h))

    def nxt(off):
        return pl.BlockSpec((1, th, wd), lambda h, bi, i: (bi, jnp.minimum(2 * i + 2, nh - 1), off + h))

    def ctx(off):
        return pl.BlockSpec((1, lc, wd), lambda h, bi, i: (bi, 0, off + h))

    ko, vo = ng, 2 * ng
    return pl.pallas_call(
        _na_kernel,
        out_shape=jax.ShapeDtypeStruct((b, l, n_heads * HEAD_DIM), BF16),
        grid=(ng, b, nb),
        in_specs=[cur(0), prev(ko), cur(ko), nxt(ko), prev(vo), cur(vo), nxt(vo), ctx(ko), ctx(vo),
                  pl.BlockSpec((1, hps, tq, 2 * tq), lambda h, bi, i: (btype(i), h, 0, 0))],
        out_specs=pl.BlockSpec((1, tq, wd), lambda h, bi, i: (bi, i, h)),
        compiler_params=_cparams(("parallel", "parallel", "arbitrary")),
        name="na_attention",
    )(qkv, qkv, qkv, qkv, qkv, qkv, qkv, qkv_c, qkv_c, bias_tab)


def _ctx_attn_kernel(q_ref, k_ref, v_ref, o_ref):
    s = lax.dot_general(q_ref[0], k_ref[0], NT_DIMS, preferred_element_type=F32)
    m = jnp.max(s, axis=-1, keepdims=True)
    p = jnp.exp2(s - m)
    den = jnp.sum(p, axis=-1, keepdims=True)
    o_ref[0] = (jnp.dot(p.astype(BF16), v_ref[0], preferred_element_type=F32) / den).astype(o_ref.dtype)


def _ctx_attention(qkv_c, n_heads):
    b, lc, _ = qkv_c.shape
    hd = HEAD_DIM

    def spec(off):
        return pl.BlockSpec((1, lc, hd), lambda bi, h: (bi, 0, off + h))

    return pl.pallas_call(
        _ctx_attn_kernel,
        out_shape=jax.ShapeDtypeStruct((b, lc, n_heads * hd), F32),
        grid=(b, n_heads),
        in_specs=[spec(0), spec(n_heads), spec(2 * n_heads)],
        out_specs=spec(0),
        compiler_params=_cparams(("parallel", "parallel")),
        name="ctx_attention",
    )(qkv_c, qkv_c, qkv_c)


def _cos_sin(n_out, n_in, period):
    ang = 2.0 * np.pi * ((np.arange(n_out)[:, None] * np.arange(n_in)[None, :]) % period) / period
    return np.cos(ang), np.sin(ang)


def _lane_cat(x3):
    return jnp.concatenate([x3[j] for j in range(x3.shape[0])], axis=1)


def _rows_from_lanes(re, im, nc):
    return jnp.concatenate(
        [jnp.concatenate([re[:, j * LANES:(j + 1) * LANES], im[:, j * LANES:(j + 1) * LANES]], axis=1)
         for j in range(nc)], axis=0)


def _lanes_from_rows(x, nc, r):
    re = jnp.concatenate([x[j * r:(j + 1) * r, :LANES] for j in range(nc)], axis=1)
    im = jnp.concatenate([x[j * r:(j + 1) * r, LANES:] for j in range(nc)], axis=1)
    return jnp.concatenate([re, im], axis=0)


def _fourier_tables(r, nc):
    l = r * LANES
    c1, s1 = _cos_sin(r, r, r)
    m1 = np.block([[c1, s1], [-s1, c1]])
    tc, ts = _cos_sin(r, LANES, l)
    c2, s2 = _cos_sin(LANES, LANES, LANES)
    m2 = np.concatenate([c2, s2], axis=0) / math.sqrt(l)
    return (_const_bf16(m1), jnp.asarray(np.tile(tc, (1, nc)), F32), jnp.asarray(np.tile(ts, (1, nc)), F32),
            _const_bf16(m2))


def _fourier_kernel(zr_ref, zi_ref, m1_ref, tc_ref, ts_ref, m2_ref, o_ref, *, nc, r):
    z = jnp.concatenate([_lane_cat(zr_ref[0]), _lane_cat(zi_ref[0])], axis=0)
    a = _dot1(m1_ref[...], z)
    ar, ai = a[:r], a[r:]
    tc, ts = tc_ref[...], ts_ref[...]
    br = (ar * tc + ai * ts).astype(BF16)
    bi = (ai * tc - ar * ts).astype(BF16)
    y = _dot1(_rows_from_lanes(br, bi, nc), m2_ref[...])
    for j in range(nc):
        o_ref[0, j] = y[j * r:(j + 1) * r].T.astype(o_ref.dtype)


def _fourier_latent(pt, n_ch, nc):
    b, _, l = pt.shape
    r = l // LANES
    p4 = pt.reshape(b, pt.shape[1], r, LANES)
    m1, tc, ts, m2 = _fourier_tables(r, nc)
    nblk = n_ch // nc

    def const(a):
        return pl.BlockSpec(a.shape, lambda bi, c: (0,) * a.ndim)

    out = pl.pallas_call(
        functools.partial(_fourier_kernel, nc=nc, r=r),
        out_shape=jax.ShapeDtypeStruct((b, n_ch, LANES, r), BF16),
        grid=(b, nblk),
        in_specs=[
            pl.BlockSpec((1, nc, r, LANES), lambda bi, c: (bi, c, 0, 0)),
            pl.BlockSpec((1, nc, r, LANES), lambda bi, c: (bi, nblk + c, 0, 0)),
            const(m1), const(tc), const(ts), const(m2),
        ],
        out_specs=pl.BlockSpec((1, nc, LANES, r), lambda bi, c: (bi, c, 0, 0)),
        compiler_params=_cparams(("parallel", "parallel")),
        name="fourier_latent",
    )(p4, p4, m1, tc, ts, m2)
    return out.reshape(b, n_ch, l)


def _filter_kernel(bands_ref, w1t_ref, w1c_ref, w1s_ref, b1_ref, w2_ref, b2_ref, w3_ref, fr_ref, dl_ref,
                   o_ref, *, l, tl):
    i = pl.program_id(0)
    n = (i * tl + lax.broadcasted_iota(jnp.int32, (1, tl), 1))
    pos_i = jnp.where(n < l, n, 2 * l - n)
    pos = pos_i.astype(F32)
    t = pos / float(max(l - 1, 1))
    ang = bands_ref[...] * (2.0 * math.pi / l) * pos
    fr = fr_ref[...]
    cos_a, sin_a = jnp.cos(ang), jnp.sin(ang)
    w1c, w1s = w1c_ref[...], w1s_ref[...]
    pre = w1t_ref[...] * t
    for k in range(HY_BANDS):
        pre = pre + w1c[:, k:k + 1] * cos_a[k:k + 1, :] - w1s[:, k:k + 1] * sin_a[k:k + 1, :]
    kpad = jnp.zeros((w2_ref.shape[1] - w2_ref.shape[0], tl), F32)
    h1 = jnp.sin(fr * (pre + b1_ref[...]))
    h2 = jnp.sin(fr * (_dot3k(w2_ref[...], jnp.concatenate([h1, kpad], axis=0)) + b2_ref[...]))
    out = _dot3k(w3_ref[0], jnp.concatenate([h2, kpad], axis=0)) * jnp.exp(-t * dl_ref[...])
    o_ref[...] = jnp.where(n == l, 0.0, out).astype(o_ref.dtype)


def _hyena_filters(l, w1, b1, w2, b2, w3, freq, width):
    hid = w1.shape[1]
    kdim = -(-hid // LANES) * LANES
    pad = kdim - hid
    tl = min(2048, l)
    rows = HY_ORDER * width
    bands = np.linspace(1e-4, HY_BANDS - 1, HY_BANDS, dtype=np.float32).reshape(HY_BANDS, 1)
    deltas = np.abs(np.linspace(HY_MIN_DECAY, HY_MAX_DECAY, width, dtype=np.float32))
    dl = np.tile(deltas, HY_ORDER).reshape(rows, 1)
    w1t = w1.T
    w2t = jnp.pad(w2.T, ((0, 0), (0, pad)))
    w3d = w3.reshape(hid, HY_ORDER, 2, width).transpose(2, 1, 3, 0).reshape(2, rows, hid)
    w3d = jnp.pad(w3d, ((0, 0), (0, 0), (0, pad)))
    col = lambda v: v.reshape(hid, 1)

    def const(shape):
        return pl.BlockSpec(shape, lambda i: (0,) * len(shape))

    return pl.pallas_call(
        functools.partial(_filter_kernel, l=l, tl=tl),
        out_shape=jax.ShapeDtypeStruct((rows, 2 * l), BF16),
        grid=(2 * l // tl,),
        in_specs=[const((HY_BANDS, 1)), const((hid, 1)), const((hid, HY_BANDS)), const((hid, HY_BANDS)),
                  const((hid, 1)), const((hid, kdim)), const((hid, 1)),
                  pl.BlockSpec((1, rows, kdim), lambda i: (i // (l // tl), 0, 0)),
                  const((hid, 1)), const((rows, 1))],
        out_specs=pl.BlockSpec((rows, tl), lambda i: (0, i)),
        compiler_params=_cparams(("parallel",)),
        name="hyena_filters",
    )(jnp.asarray(bands), w1t[:, 0:1], w1t[:, 1:1 + HY_BANDS], w1t[:, 1 + HY_BANDS:], col(b1), w2t, col(b2),
      w3d, col(freq), jnp.asarray(dl))


def _conv_tables(r, nc):
    r2 = 2 * r
    n = r2 * LANES
    c1, s1 = _cos_sin(r2, r2, r2)
    c1h, s1h = c1[:, :r], s1[:, :r]
    m1 = np.block([[c1h, s1h], [-s1h, c1h]])
    m1_real = np.concatenate([c1, -s1], axis=0)
    tc, ts = _cos_sin(r2, LANES, n)
    c2, s2 = _cos_sin(LANES, LANES, LANES)
    m2 = np.block([[c2, -s2], [s2, c2]])
    m2i = np.block([[c2, s2], [-s2, c2]])
    ct, st = c1h.T, s1h.T
    m1i = np.block([[ct, -st], [st, ct]]) / n
    f32 = lambda a: jnp.asarray(a, F32)
    return dict(
        m1=_const_bf16(m1), m1_real=_const_bf16(m1_real), m2=_const_bf16(m2), m2i=_const_bf16(m2i),
        m1i=_const_bf16(m1i),
        tc_l=f32(np.tile(tc, (1, nc))), ts_l=f32(np.tile(ts, (1, nc))),
        tc_r=f32(np.tile(tc, (nc, 1))), ts_r=f32(np.tile(ts, (nc, 1))))


def _spec_kernel(k_ref, m1_ref, tc_ref, ts_ref, m2_ref, o_ref, *, nc, r2):
    a = _dot1(m1_ref[...], _lane_cat(k_ref[...]))
    ar, ai = a[:r2], a[r2:]
    tc, ts = tc_ref[...], ts_ref[...]
    br = (ar * tc + ai * ts).astype(BF16)
    bi = (ai * tc - ar * ts).astype(BF16)
    x = _dot1(_rows_from_lanes(br, bi, nc), m2_ref[...])
    o_ref[...] = x.reshape(nc, r2, 2 * LANES).astype(o_ref.dtype)


def _hyena_spectra(k2t, tabs, nc):
    c2n, n = k2t.shape
    r2 = n // LANES
    k3 = k2t.reshape(c2n, r2, LANES)

    def const(a):
        return pl.BlockSpec(a.shape, lambda c: (0,) * a.ndim)

    return pl.pallas_call(
        functools.partial(_spec_kernel, nc=nc, r2=r2),
        out_shape=jax.ShapeDtypeStruct((c2n, r2, 2 * LANES), BF16),
        grid=(c2n // nc,),
        in_specs=[pl.BlockSpec((nc, r2, LANES), lambda c: (c, 0, 0)),
                  const(tabs["m1_real"]), const(tabs["tc_l"]), const(tabs["ts_l"]), const(tabs["m2"])],
        out_specs=pl.BlockSpec((nc, r2, 2 * LANES), lambda c: (c, 0, 0)),
        compiler_params=_cparams(("parallel",)),
        name="hyena_spectra",
    )(k3, tabs["m1_real"], tabs["tc_l"], tabs["ts_l"], tabs["m2"])


def _shift_tokens(t, r, direction):
    w = t.shape[1]
    lane = lax.broadcasted_iota(jnp.int32, t.shape, 1) & (LANES - 1)
    row = lax.broadcasted_iota(jnp.int32, t.shape, 0)
    if direction < 0:
        near = pltpu.roll(t, 1, 1)
        wrap = pltpu.roll(pltpu.roll(t, w - (LANES - 1), 1), 1, 0)
        edge = lane == 0
        dead = edge & (row == 0)
    else:
        near = pltpu.roll(t, w - 1, 1)
        wrap = pltpu.roll(pltpu.roll(t, LANES - 1, 1), r - 1, 0)
        edge = lane == LANES - 1
        dead = edge & (row == r - 1)
    return jnp.where(dead, 0.0, jnp.where(edge, wrap, near))


def _long_conv(zr, zi, kspec, m1, m2, m2i, m1i, tc_l, ts_l, tc_r, ts_r, nc, r):
    r2 = 2 * r
    a = _dot1(m1, jnp.concatenate([zr.astype(BF16), zi.astype(BF16)], axis=0))
    ar, ai = a[:r2], a[r2:]
    br = (ar * tc_l + ai * ts_l).astype(BF16)
    bi = (ai * tc_l - ar * ts_l).astype(BF16)
    x = _dot1(_rows_from_lanes(br, bi, nc), m2)
    xr, xi = x[:, :LANES], x[:, LANES:]
    kr, ki = kspec[:, :LANES].astype(F32), kspec[:, LANES:].astype(F32)
    y = jnp.concatenate([(xr * kr - xi * ki).astype(BF16), (xr * ki + xi * kr).astype(BF16)], axis=1)
    bb = _dot1(y, m2i)
    pr, pi = bb[:, :LANES], bb[:, LANES:]
    q = jnp.concatenate([(pr * tc_r - pi * ts_r).astype(BF16), (pi * tc_r + pr * ts_r).astype(BF16)], axis=1)
    out = _dot1(m1i, _lanes_from_rows(q, nc, r2))
    return out[:r], out[r:]


def _hyena_kernel(v0, v1, a0, a1, b0, b1, k0_ref, k1_ref, cw_ref, cb_ref, bias_ref,
                  m1_ref, m2_ref, m2i_ref, m1i_ref, tcl_ref, tsl_ref, tcr_ref, tsr_ref, o_ref, *, nc, r):
    tabs = (m1_ref[...], m2_ref[...], m2i_ref[...], m1i_ref[...],
            tcl_ref[...], tsl_ref[...], tcr_ref[...], tsr_ref[...])

    def short_conv(ref, part):
        t = _lane_cat(ref[0]).astype(F32)
        w = cw_ref[part]
        return (_shift_tokens(t, r, -1) * w[0:1] + t * w[1:2] + _shift_tokens(t, r, +1) * w[2:3]
                + cb_ref[part])

    vr, vi = short_conv(v0, 0), short_conv(v1, 0)
    x1r, x1i = short_conv(a0, 1), short_conv(a1, 1)
    x2r, x2i = short_conv(b0, 2), short_conv(b1, 2)
    bias = bias_ref[...]
    r2 = 2 * r
    yr, yi = _long_conv(vr, vi, k0_ref[...].reshape(nc * r2, 2 * LANES), *tabs, nc, r)
    zr = x1r * (yr + vr * bias[0:1])
    zi = x1i * (yi + vi * bias[0:1])
    yr, yi = _long_conv(zr, zi, k1_ref[...].reshape(nc * r2, 2 * LANES), *tabs, nc, r)
    outr = x2r * (yr + zr * bias[1:2])
    outi = x2i * (yi + zi * bias[1:2])
    for j in range(nc):
        o_ref[0, j] = outr[:, j * LANES:(j + 1) * LANES].astype(o_ref.dtype)
        o_ref[1, j] = outi[:, j * LANES:(j + 1) * LANES].astype(o_ref.dtype)


def _hyena_latent(pt, ch0, width, spec, conv_w, conv_b, bias, tabs, nc):
    b, c_all, l = pt.shape
    assert b == 2, "the two batch entries ride one complex transform"
    r = l // LANES
    r2 = 2 * r
    p4 = pt.reshape(b, c_all, r, LANES)
    nblk = width // nc
    rep = lambda a: jnp.repeat(a, LANES, axis=-1)
    cw = rep(conv_w.reshape(3, 3, width).transpose(1, 0, 2))
    cb = rep(conv_b.reshape(3, 1, width))
    bs = rep(bias)

    def inp(bi, part):
        off = (ch0 + part * width) // nc
        return pl.BlockSpec((1, nc, r, LANES), lambda c: (bi, off + c, 0, 0))

    def const(a):
        return pl.BlockSpec(a.shape, lambda c: (0,) * a.ndim)

    names = ("m1", "m2", "m2i", "m1i", "tc_l", "ts_l", "tc_r", "ts_r")
    consts = [tabs[k] for k in names]
    out = pl.pallas_call(
        functools.partial(_hyena_kernel, nc=nc, r=r),
        out_shape=jax.ShapeDtypeStruct((b, width, r, LANES), BF16),
        grid=(nblk,),
        in_specs=[inp(0, 0), inp(1, 0), inp(0, 1), inp(1, 1), inp(0, 2), inp(1, 2),
                  pl.BlockSpec((nc, r2, 2 * LANES), lambda c: (c, 0, 0)),
                  pl.BlockSpec((nc, r2, 2 * LANES), lambda c: (nblk + c, 0, 0)),
                  pl.BlockSpec((3, 3, nc * LANES), lambda c: (0, 0, c)),
                  pl.BlockSpec((3, 1, nc * LANES), lambda c: (0, 0, c)),
                  pl.BlockSpec((HY_ORDER, nc * LANES), lambda c: (0, c))]
                 + [const(a) for a in consts],
        out_specs=pl.BlockSpec((b, nc, r, LANES), lambda c: (0, c, 0, 0)),
        compiler_params=_cparams(("parallel",)),
        name="hyena_latent",
    )(p4, p4, p4, p4, p4, p4, spec, spec, cw, cb, bs, *consts)
    return out.reshape(b, width, l)


def _ctx_fourier_kernel(zr_ref, zi_ref, m_ref, o_ref):
    z = jnp.concatenate([zr_ref[0], zi_ref[0]], axis=1).astype(F32)
    o_ref[0] = _dot_data_const(z, m_ref[...])


def _ctx_fourier(ptc, n_ch):
    b, _, lc = ptc.shape
    c, s = _cos_sin(lc, lc, lc)
    m = _const_rhs3(np.concatenate([c, s], axis=0) / math.sqrt(lc))
    return pl.pallas_call(
        _ctx_fourier_kernel,
        out_shape=jax.ShapeDtypeStruct((b, n_ch, lc), F32),
        grid=(b,),
        in_specs=[pl.BlockSpec((1, n_ch, lc), lambda bi: (bi, 0, 0)),
                  pl.BlockSpec((1, n_ch, lc), lambda bi: (bi, 1, 0)),
                  pl.BlockSpec(m.shape, lambda bi: (0, 0))],
        out_specs=pl.BlockSpec((1, n_ch, lc), lambda bi: (bi, 0, 0)),
        compiler_params=_cparams(("parallel",)),
        name="ctx_fourier",
    )(ptc, ptc, m)


def _ctx_hyena_kernel(v0, v1, a0, a1, b0, b1, k_ref, cw_ref, cb_ref, bias_ref, mk_ref, mf_ref, mi_ref,
                      o_ref, *, lc, width):
    def short_conv(ref, part):
        t = ref[0].astype(F32)
        lane = lax.broadcasted_iota(jnp.int32, t.shape, 1)
        prv = jnp.where(lane == 0, 0.0, pltpu.roll(t, 1, 1))
        nxt = jnp.where(lane == lc - 1, 0.0, pltpu.roll(t, lc - 1, 1))
        w = cw_ref[part]
        return prv * w[:, 0:1] + t * w[:, 1:2] + nxt * w[:, 2:3] + cb_ref[part]

    kspec = _dot_data_const(k_ref[...].astype(F32), mk_ref[...])
    n = 2 * lc

    def long_conv(zr, zi, ks):
        x = _dot_data_const(jnp.concatenate([zr, zi], axis=1), mf_ref[...])
        xr, xi = x[:, :n], x[:, n:]
        kr, ki = ks[:, :n], ks[:, n:]
        y = jnp.concatenate([xr * kr - xi * ki, xr * ki + xi * kr], axis=1)
        out = _dot_data_const(y, mi_ref[...])
        return out[:, :lc], out[:, lc:]

    vr, vi = short_conv(v0, 0), short_conv(v1, 0)
    x1r, x1i = short_conv(a0, 1), short_conv(a1, 1)
    x2r, x2i = short_conv(b0, 2), short_conv(b1, 2)
    bias = bias_ref[...]
    yr, yi = long_conv(vr, vi, kspec[:width])
    zr = x1r * (yr + vr * bias[:, 0:1])
    zi = x1i * (yi + vi * bias[:, 0:1])
    yr, yi = long_conv(zr, zi, kspec[width:])
    o_ref[0] = x2r * (yr + zr * bias[:, 1:2])
    o_ref[1] = x2i * (yi + zi * bias[:, 1:2])


def _ctx_hyena(ptc, ch0, width, k2t, conv_w, conv_b, bias):
    b, _, lc = ptc.shape
    assert b == 2
    n = 2 * lc
    c, s = _cos_sin(n, n, n)
    mk = _const_rhs3(np.concatenate([c, -s], axis=1))
    ch, sh = c[:lc], s[:lc]
    mf = _const_rhs3(np.block([[ch, -sh], [sh, ch]]))
    ci, si = c[:, :lc], s[:, :lc]
    mi = _const_rhs3(np.block([[ci, si], [-si, ci]]) / n)
    cw = conv_w.reshape(3, 3, width).transpose(1, 2, 0)
    cb = conv_b.reshape(3, width, 1)

    def inp(bi, part):
        return pl.BlockSpec((1, width, lc), lambda i: (bi, ch0 // width + part, 0))

    def const(a):
        return pl.BlockSpec(a.shape, lambda i: (0,) * a.ndim)

    args = (k2t, cw, cb, bias.T, mk, mf, mi)
    return pl.pallas_call(
        functools.partial(_ctx_hyena_kernel, lc=lc, width=width),
        out_shape=jax.ShapeDtypeStruct((b, width, lc), F32),
        grid=(1,),
        in_specs=[inp(0, 0), inp(1, 0), inp(0, 1), inp(1, 1), inp(0, 2), inp(1, 2)] + [const(a) for a in args],
        out_specs=pl.BlockSpec((b, width, lc), lambda i: (0, 0, 0)),
        compiler_params=_cparams(("arbitrary",)),
        name="ctx_hyena",
    )(ptc, ptc, ptc, ptc, ptc, ptc, *args)


def _merge_kernel(a_ref, f_ref, hy_ref, x_ref, gate_ref, ga_ref, gf_ref, gh_ref, wa_ref, wf_ref, wh_ref, o_ref):
    a = a_ref[0].astype(F32)
    ya = a * lax.rsqrt(jnp.mean(a * a, axis=-1, keepdims=True) + EPS) * ga_ref[...]
    acc = jnp.dot(ya.astype(BF16), wa_ref[...], preferred_element_type=F32)

    def cm_part(ref, g_ref, w_ref):
        t = ref[0].astype(F32)
        y = t * lax.rsqrt(jnp.mean(t * t, axis=0, keepdims=True) + EPS) * g_ref[...]
        return jnp.dot(y.T.astype(BF16), w_ref[...], preferred_element_type=F32)

    acc = acc + cm_part(f_ref, gf_ref, wf_ref) + cm_part(hy_ref, gh_ref, wh_ref)
    o_ref[0] = x_ref[0] + gate_ref[0] * acc


def _merge_out(a, ft, ht, x, gate, g, w_out, tm):
    b, l, d = x.shape
    wa_n, wf_n, wh_n = a.shape[2], ft.shape[1], ht.shape[1]
    tm = min(tm, l)
    w = w_out.astype(BF16)
    wa, wf, wh = w[:wa_n], w[wa_n:wa_n + wf_n], w[wa_n + wf_n:]
    ga = g[:wa_n].reshape(1, wa_n)
    gf = g[wa_n:wa_n + wf_n].reshape(wf_n, 1)
    gh = g[wa_n + wf_n:].reshape(wh_n, 1)

    def const(arr):
        return pl.BlockSpec(arr.shape, lambda bi, i: (0,) * arr.ndim)

    return pl.pallas_call(
        _merge_kernel,
        out_shape=jax.ShapeDtypeStruct((b, l, d), F32),
        grid=(b, l // tm),
        in_specs=[pl.BlockSpec((1, tm, wa_n), lambda bi, i: (bi, i, 0)),
                  pl.BlockSpec((1, wf_n, tm), lambda bi, i: (bi, 0, i)),
                  pl.BlockSpec((1, wh_n, tm), lambda bi, i: (bi, 0, i)),
                  pl.BlockSpec((1, tm, d), lambda bi, i: (bi, i, 0)),
                  pl.BlockSpec((1, 1, d), lambda bi, i: (bi, 0, 0)),
                  const(ga), const(gf), const(gh), const(wa), const(wf), const(wh)],
        out_specs=pl.BlockSpec((1, tm, d), lambda bi, i: (bi, i, 0)),
        compiler_params=_cparams(("parallel", "parallel")),
        name="merge_out",
    )(a, ft, ht, x, gate.reshape(b, 1, d), ga, gf, gh, wa, wf, wh)


HALO = 16


def _ffn_kernel(xp_ref, x_ref, xn_ref, g_ref, sh_ref, sc_ref, gate_ref, wg_ref, wu_ref, cw_ref, cb_ref, wd_ref,
                fg_ref, o_ref, h_ref, *, tm, final_norm):
    i = pl.program_id(1)
    c = pl.program_id(2)
    last_tile = pl.num_programs(1) - 1

    n_ext = tm + 2 * HALO

    def down_partial(ge, up):
        row = lax.broadcasted_iota(jnp.int32, ge.shape, 0)
        outside = ((row < HALO) & (i == 0)) | ((row >= HALO + tm) & (i == last_tile))
        ge = jnp.where(outside, 0.0, ge)
        gp = pltpu.roll(ge, 1, 0)[HALO:HALO + tm]
        gn = pltpu.roll(ge, n_ext - 1, 0)[HALO:HALO + tm]
        cw = cw_ref[...]
        conv = gp * cw[0:1] + ge[HALO:HALO + tm] * cw[1:2] + gn * cw[2:3] + cb_ref[...]
        inner = 0.7978845608028654 * (conv + 0.044715 * (conv * conv * conv))
        act = 0.5 * conv * (1.0 + jnp.tanh(inner)) * up
        return jnp.dot(act.astype(BF16), wd_ref[...], preferred_element_type=F32)

    @pl.when(c == 0)
    def _():
        g, sh, sc = g_ref[...], sh_ref[0], sc_ref[0]
        wg, wu = wg_ref[...], wu_ref[...]
        rs = tm // NORM_SLABS
        bounds = [(0, HALO)] + [(HALO + s * rs, HALO + (s + 1) * rs) for s in range(NORM_SLABS)] + [(HALO + tm, n_ext)]
        ge_parts, up_parts = [], []
        for lo, hi in bounds:
            if lo == 0:
                xs = xp_ref[0, 0]
            elif hi == n_ext:
                xs = xn_ref[0, 0]
            else:
                xs = x_ref[0, lo - HALO:hi - HALO, :]
            h = _norm_mod(xs, g, sh, sc).astype(BF16)
            h_ref[lo:hi, :] = h
            ge_parts.append(jnp.dot(h, wg, preferred_element_type=F32))
            if lo != 0 and hi != n_ext:
                up_parts.append(jnp.dot(h, wu, preferred_element_type=F32))
        o_ref[0] = down_partial(jnp.concatenate(ge_parts, axis=0), jnp.concatenate(up_parts, axis=0))

    @pl.when(c > 0)
    def _():
        h = h_ref[...]
        ge = jnp.dot(h, wg_ref[...], preferred_element_type=F32)
        up = jnp.dot(h[HALO:HALO + tm], wu_ref[...], preferred_element_type=F32)
        o_ref[0] += down_partial(ge, up)

    @pl.when(c == pl.num_programs(2) - 1)
    def _():
        y = x_ref[0] + gate_ref[0] * o_ref[0]
        if final_norm:
            y = y * lax.rsqrt(jnp.mean(y * y, axis=-1, keepdims=True) + EPS) * fg_ref[...]
        o_ref[0] = y


def _ffn(x, g, sh, sc, gate, w_up, conv_w, conv_b, w_down, final_g, tm, tf, *, final_norm):
    b, l, d = x.shape
    dff = w_down.shape[0]
    tm = min(tm, l)
    nch = dff // tf
    x4 = x.reshape(b, l // HALO, HALO, d)
    per_tile = tm // HALO
    nhalo = l // HALO
    vec = pl.BlockSpec((1, 1, d), lambda bi, i, c: (bi, 0, 0))
    return pl.pallas_call(
        functools.partial(_ffn_kernel, tm=tm, final_norm=final_norm),
        out_shape=jax.ShapeDtypeStruct((b, l, d), F32),
        grid=(b, l // tm, nch),
        in_specs=[
            pl.BlockSpec((1, 1, HALO, d), lambda bi, i, c: (bi, jnp.maximum(i * per_tile - 1, 0), 0, 0)),
            pl.BlockSpec((1, tm, d), lambda bi, i, c: (bi, i, 0), pipeline_mode=pl.Buffered(1)),
            pl.BlockSpec((1, 1, HALO, d), lambda bi, i, c: (bi, jnp.minimum((i + 1) * per_tile, nhalo - 1), 0, 0)),
            pl.BlockSpec((1, d), lambda bi, i, c: (0, 0)),
            vec, vec, vec,
            pl.BlockSpec((d, tf), lambda bi, i, c: (0, c)),
            pl.BlockSpec((d, tf), lambda bi, i, c: (0, nch + c)),
            pl.BlockSpec((3, tf), lambda bi, i, c: (0, c)),
            pl.BlockSpec((1, tf), lambda bi, i, c: (0, c)),
            pl.BlockSpec((tf, d), lambda bi, i, c: (c, 0)),
            pl.BlockSpec((1, d), lambda bi, i, c: (0, 0)),
        ],
        out_specs=pl.BlockSpec((1, tm, d), lambda bi, i, c: (bi, i, 0), pipeline_mode=pl.Buffered(1)),
        scratch_shapes=[pltpu.VMEM((tm + 2 * HALO, d), BF16)],
        compiler_params=_cparams(("parallel", "parallel", "arbitrary")),
        name="ffn",
    )(x4, x, x4, g.reshape(1, d), sh.reshape(b, 1, d), sc.reshape(b, 1, d), gate.reshape(b, 1, d),
      w_up, w_up, conv_w, conv_b.reshape(1, dff), w_down, final_g.reshape(1, d))


TOKEN_TILE = 512
PROJ_TOKEN_TILE = 1024
PROJ_COL_TILE = 512
FFN_TOKEN_TILE = 1024
FFN_COL_TILE = 512
MIX_CHANNELS = 8


def kernel(x, c, ctx, c_ctx, ada_w, ada_b, norm1_g, norm2_g, w_in, na_rpb, hy_conv_w, hy_conv_b, hy_w1, hy_b1,
           hy_w2, hy_b2, hy_w3, hy_freq, hy_bias, mix_norm_g, w_out, ffn_w_up, ffn_conv_w, ffn_conv_b, ffn_w_down,
           final_norm_g):
    b, l, d = x.shape
    lc = ctx.shape[1]
    depth = ada_w.shape[0]
    na_w, fn_w = d // 2, d // 4
    hy_w = d - na_w - fn_w
    n_heads = na_w // HEAD_DIM
    qkv_w = 3 * na_w
    rows = l // GRID_W
    assert b + 1 <= 8 and l % (NA_ROWS_PER_BLOCK * GRID_W) == 0 and rows >= 2 * NA_ROWS_PER_BLOCK
    r = l // LANES
    nc = MIX_CHANNELS
    conv_tabs = _conv_tables(r, nc)

    s_in = jnp.concatenate([c, c_ctx[None], jnp.zeros((8 - b - 1, d), F32)], axis=0)
    mod_all = _ada_mod(s_in, ada_w, ada_b)
    xc = ctx
    for layer in range(depth):
        update_ctx = layer < depth - 1
        mod = mod_all[layer]
        sh1, sc1, g1, sh2, sc2, g2 = jnp.split(mod[:b], N_MOD, axis=-1)
        csh1, csc1, cg1, csh2, csc2, cg2 = jnp.split(jnp.broadcast_to(mod[b:b + 1], (b, N_MOD * d)), N_MOD, axis=-1)

        wl = w_in[layer]
        w_fold = _fold_fourier_weights(wl[:, qkv_w:qkv_w + fn_w])
        wt = jnp.concatenate([(wl[:, :na_w] * (HEAD_DIM ** -0.5 * LOG2E)).T, wl[:, na_w:qkv_w].T, w_fold[0].T, w_fold[1].T,
                              wl[:, qkv_w + fn_w:].T], axis=0).astype(BF16)
        hy0 = 2 * fn_w

        qkv, pt = _proj(x, norm1_g[layer], sh1, sc1, wt, qkv_w, tm=PROJ_TOKEN_TILE, tn=PROJ_COL_TILE)
        qkv_c, ptc = _proj(xc, norm1_g[layer], csh1, csc1, wt, qkv_w, tm=PROJ_TOKEN_TILE, tn=PROJ_COL_TILE)

        a = _na_attention(qkv, qkv_c, _na_bias_table(na_rpb[layer], rows), n_heads)
        yf = _fourier_latent(pt, fn_w, nc)
        filt = (hy_w1[layer], hy_b1[layer], hy_w2[layer], hy_b2[layer], hy_w3[layer], hy_freq[layer])
        spec = _hyena_spectra(_hyena_filters(l, *filt, hy_w), conv_tabs, nc)
        yh = _hyena_latent(pt, hy0, hy_w, spec, hy_conv_w[layer], hy_conv_b[layer], hy_bias[layer], conv_tabs, nc)
        x_new = _merge_out(a, yf, yh, x, g1, mix_norm_g[layer], w_out[layer], TOKEN_TILE)

        if update_ctx:
            ac = _ctx_attention(qkv_c, n_heads)
            yfc = _ctx_fourier(ptc, fn_w)
            yhc = _ctx_hyena(ptc, hy0, hy_w, _hyena_filters(lc, *filt, hy_w), hy_conv_w[layer], hy_conv_b[layer],
                             hy_bias[layer])
            xc = _merge_out(ac, yfc, yhc, xc, cg1, mix_norm_g[layer], w_out[layer], TOKEN_TILE)
        x = x_new

        w_up = ffn_w_up[layer].astype(BF16)
        w_down = ffn_w_down[layer].astype(BF16)
        x = _ffn(x, norm2_g[layer], sh2, sc2, g2, w_up, ffn_conv_w[layer], ffn_conv_b[layer], w_down,
                 final_norm_g, FFN_TOKEN_TILE, FFN_COL_TILE, final_norm=not update_ctx)
        if update_ctx:
            xc = _ffn(xc, norm2_g[layer], csh2, csc2, cg2, w_up, ffn_conv_w[layer], ffn_conv_b[layer], w_down,
                      final_norm_g, FFN_TOKEN_TILE, FFN_COL_TILE, final_norm=False)
    return x
```

```python
import functools
import math

import numpy as np
import jax
import jax.numpy as jnp
from jax import lax
from jax.experimental import pallas as pl
from jax.experimental.pallas import tpu as pltpu

F32 = jnp.float32
BF16 = jnp.bfloat16

EPS = 1e-6
HEAD_DIM = 128
GRID_W = 64
NA_KH = 8
NA_KW = 16
NA_ROWS_PER_BLOCK = 8
FN_GROUP_DIM = 128
HY_ORDER = 2
HY_BANDS = 16
HY_FAST_DECAY = 0.3
HY_SLOW_DECAY = 1.5
HY_TARGET = 1e-2
HY_MIN_DECAY = math.log(HY_TARGET) / HY_SLOW_DECAY
HY_MAX_DECAY = math.log(HY_TARGET) / HY_FAST_DECAY
N_MOD = 6
LANES = 128
NEG_INF = -1e30
LOG2E = math.log2(math.e)
VMEM_LIMIT = 56 * 1024 * 1024

NT_DIMS = (((1,), (1,)), ((), ()))


def _cparams(sem):
    return pltpu.CompilerParams(dimension_semantics=sem, vmem_limit_bytes=VMEM_LIMIT)


def _hi_lo(a):
    hi = a.astype(BF16)
    lo = (a - hi.astype(F32)).astype(BF16)
    return hi, lo


def _np_hi_lo(m):
    m = np.asarray(m, np.float32)
    hi = m.astype(BF16)
    lo = (m - hi.astype(np.float32)).astype(BF16)
    return hi, lo


def _const_rhs3(m):
    hi, lo = _np_hi_lo(m)
    return jnp.asarray(np.concatenate([hi, hi, lo], axis=0))


def _const_lhs3(m):
    hi, lo = _np_hi_lo(m)
    return jnp.asarray(np.concatenate([hi, hi, lo], axis=1))


def _dot_data_const(a, c3):
    hi, lo = _hi_lo(a)
    return jnp.dot(jnp.concatenate([hi, lo, hi], axis=1), c3, preferred_element_type=F32)


def _dot_const_data(c3, b):
    hi, lo = _hi_lo(b)
    return jnp.dot(c3, jnp.concatenate([hi, lo, hi], axis=0), preferred_element_type=F32)


def _dot3(a, b):
    ah, al = _hi_lo(a)
    bh, bl = _hi_lo(b)
    return (jnp.dot(ah, bh, preferred_element_type=F32)
            + jnp.dot(al, bh, preferred_element_type=F32)
            + jnp.dot(ah, bl, preferred_element_type=F32))


def _dot3k(a, b):
    ah, al = _hi_lo(a)
    bh, bl = _hi_lo(b)
    return jnp.dot(jnp.concatenate([ah, al, ah], axis=1), jnp.concatenate([bh, bh, bl], axis=0),
                   preferred_element_type=F32)


def _const_bf16(m):
    return jnp.asarray(np.asarray(m, np.float32).astype(BF16))


def _dot1(a, b):
    return jnp.dot(a.astype(BF16), b.astype(BF16), preferred_element_type=F32)


def _ada_kernel(s_ref, w_ref, b_ref, o_ref):
    s = s_ref[...]
    s = s / (1.0 + jnp.exp(-s))
    o_ref[0] = _dot1(s, w_ref[0]) + b_ref[0]


def _ada_mod(s_in, ada_w, ada_b):
    depth, d, n = ada_w.shape
    tn = 1024
    return pl.pallas_call(
        _ada_kernel,
        out_shape=jax.ShapeDtypeStruct((depth, 8, n), F32),
        grid=(depth, n // tn),
        in_specs=[
            pl.BlockSpec((8, d), lambda l, j: (0, 0)),
            pl.BlockSpec((1, d, tn), lambda l, j: (l, 0, j)),
            pl.BlockSpec((1, 1, tn), lambda l, j: (l, 0, j)),
        ],
        out_specs=pl.BlockSpec((1, 8, tn), lambda l, j: (l, 0, j)),
        compiler_params=_cparams(("parallel", "parallel")),
        name="ada_mod",
    )(s_in, ada_w, ada_b.reshape(depth, 1, n))


def _norm_mod(x, g, sh, sc):
    ms = jnp.mean(x * x, axis=-1, keepdims=True)
    return (x * lax.rsqrt(ms + EPS) * g) * (1.0 + sc) + sh


NORM_SLABS = 4


def _proj_kernel(x_ref, g_ref, sh_ref, sc_ref, w_ref, otm_ref, ocm_ref, h_ref, *, n_tm):
    j = pl.program_id(2)
    tm = x_ref.shape[1]
    rs = tm // NORM_SLABS

    @pl.when(j == 0)
    def _():
        g, sh, sc = g_ref[...], sh_ref[0], sc_ref[0]
        w = w_ref[...]
        for s in range(NORM_SLABS):
            h = _norm_mod(x_ref[0, s * rs:(s + 1) * rs, :], g, sh, sc).astype(BF16)
            h_ref[s * rs:(s + 1) * rs, :] = h
            otm_ref[0, s * rs:(s + 1) * rs, :] = lax.dot_general(
                h, w, NT_DIMS, preferred_element_type=F32).astype(otm_ref.dtype)

    @pl.when((j > 0) & (j < n_tm))
    def _():
        otm_ref[0] = lax.dot_general(h_ref[...], w_ref[...], NT_DIMS,
                                     preferred_element_type=F32).astype(otm_ref.dtype)

    @pl.when(j >= n_tm)
    def _():
        y = lax.dot_general(h_ref[...], w_ref[...], NT_DIMS, preferred_element_type=F32)
        ocm_ref[0] = y.T.astype(ocm_ref.dtype)


def _proj(x, g, sh, sc, wt, n_tok, *, tm, tn):
    b, l, d = x.shape
    n = wt.shape[0]
    tm = min(tm, l)
    n_tm = n_tok // tn
    n_cm = (n - n_tok) // tn
    assert n_tm >= 1 and n_cm >= 1 and n_tm * tn == n_tok and tm % (16 * NORM_SLABS) == 0
    vec = pl.BlockSpec((1, 1, d), lambda bi, i, j: (bi, 0, 0))
    return pl.pallas_call(
        functools.partial(_proj_kernel, n_tm=n_tm),
        out_shape=(jax.ShapeDtypeStruct((b, l, n_tok), BF16), jax.ShapeDtypeStruct((b, n - n_tok, l), BF16)),
        grid=(b, l // tm, n_tm + n_cm),
        in_specs=[
            pl.BlockSpec((1, tm, d), lambda bi, i, j: (bi, i, 0)),
            pl.BlockSpec((1, d), lambda bi, i, j: (0, 0)),
            vec, vec,
            pl.BlockSpec((tn, d), lambda bi, i, j: (j, 0)),
        ],
        out_specs=(pl.BlockSpec((1, tm, tn), lambda bi, i, j: (bi, i, jnp.minimum(j, n_tm - 1))),
                   pl.BlockSpec((1, tn, tm), lambda bi, i, j: (bi, jnp.maximum(j - n_tm, 0), i))),
        scratch_shapes=[pltpu.VMEM((tm, d), BF16)],
        compiler_params=_cparams(("parallel", "parallel", "arbitrary")),
        name="proj_in",
    )(x, g.reshape(1, d), sh.reshape(b, 1, d), sc.reshape(b, 1, d), wt)


def _fold_kernel(w_ref, m_ref, o_ref):
    o_ref[0] = _dot_data_const(w_ref[...], m_ref[...])


def _fold_fourier_weights(w_f):
    d, width = w_f.shape
    gd = FN_GROUP_DIM
    idx = np.arange(gd)
    ang = 2.0 * np.pi * np.outer(idx, idx) / gd
    mat = np.concatenate([np.cos(ang), -np.sin(ang)], axis=1) / math.sqrt(gd)
    groups = width // gd
    return pl.pallas_call(
        _fold_kernel,
        out_shape=jax.ShapeDtypeStruct((2, d, width), F32),
        grid=(groups, 2),
        in_specs=[
            pl.BlockSpec((d, gd), lambda g, p: (0, g)),
            pl.BlockSpec((3 * gd, gd), lambda g, p: (0, p)),
        ],
        out_specs=pl.BlockSpec((1, d, gd), lambda g, p: (p, 0, g)),
        compiler_params=_cparams(("parallel", "parallel")),
        name="fold_fourier",
    )(w_f, _const_rhs3(mat))


def _na_bias_table(rpb, rows):
    h = rpb.shape[0]
    w, kw, kh, rb = GRID_W, NA_KW, NA_KH, NA_ROWS_PER_BLOCK
    col = np.arange(w)
    cs = np.clip(col - kw // 2, 0, w - kw)
    kc = np.arange(w)[None, :]
    col_ok = (kc >= cs[:, None]) & (kc < cs[:, None] + kw)
    padded = jnp.pad(rpb * LOG2E, ((0, 0), (0, 0), (w - kw, w - kw)))
    t1 = jnp.stack([padded[:, :, w - 1 - q:2 * w - 1 - q] for q in range(w)], axis=1)
    t1 = jnp.where(col_ok[None, :, None, :], t1, NEG_INF)
    slots = 2 * rb
    blocks = []
    for t, row0 in enumerate((0, rb, rows - rb)):
        per_row = []
        for qr in range(rb):
            r = row0 + qr
            r0 = r - kh // 2 if t == 1 else min(max(r - kh // 2, 0), rows - kh)
            valid = [kr for kr in range(slots)
                     if r0 <= row0 - rb // 2 + kr < r0 + kh and (t == 1 or 0 <= row0 - rb // 2 + kr < rows)]
            lo, hi = valid[0], valid[-1] + 1
            d0 = (row0 - rb // 2 + lo) - r + (kh - 1)
            live = [t1[:, :, d0 + k, :] for k in range(hi - lo)]
            dead = jnp.full((h, w, w), NEG_INF, t1.dtype)
            per_row.append(jnp.concatenate([dead] * lo + live + [dead] * (slots - hi), axis=-1))
        blocks.append(jnp.stack(per_row, axis=1))
    return jnp.stack(blocks, axis=0).reshape(3, h, rb * w, slots * w)


NA_HEADS_PER_STEP = 4


def _na_kernel(q_ref, kp_ref, kc_ref, kn_ref, vp_ref, vc_ref, vn_ref, kx_ref, vx_ref, bias_ref, o_ref):
    k_all = jnp.concatenate([kp_ref[0], kc_ref[0], kn_ref[0]], axis=0)
    v_all = jnp.concatenate([vp_ref[0], vc_ref[0], vn_ref[0]], axis=0)
    hq = q_ref.shape[1] // 2
    band = (NA_ROWS_PER_BLOCK // 2 + NA_KH) * GRID_W
    for h in range(NA_HEADS_PER_STEP):
        sl = slice(h * HEAD_DIM, (h + 1) * HEAD_DIM)
        kx, vx = kx_ref[0, :, sl], vx_ref[0, :, sl]
        for part in range(2):
            rows = slice(part * hq, (part + 1) * hq)
            keys = slice(part * hq, part * hq + band)
            q = q_ref[0, rows, sl]
            s = lax.dot_general(q, k_all[keys, sl], NT_DIMS, preferred_element_type=F32) + bias_ref[0, h, rows, keys]
            sx = lax.dot_general(q, kx, NT_DIMS, preferred_element_type=F32)
            m = jnp.maximum(jnp.max(s, axis=-1, keepdims=True), jnp.max(sx, axis=-1, keepdims=True))
            p = jnp.exp2(s - m)
            px = jnp.exp2(sx - m)
            den = jnp.sum(p, axis=-1, keepdims=True) + jnp.sum(px, axis=-1, keepdims=True)
            o = (jnp.dot(p.astype(BF16), v_all[keys, sl], preferred_element_type=F32)
                 + jnp.dot(px.astype(BF16), vx, preferred_element_type=F32))
            o_ref[0, rows, sl] = (o / den).astype(o_ref.dtype)


def _na_attention(qkv, qkv_c, bias_tab, n_heads):
    b, l, _ = qkv.shape
    lc = qkv_c.shape[1]
    hps = NA_HEADS_PER_STEP
    wd = hps * HEAD_DIM
    tq = NA_ROWS_PER_BLOCK * GRID_W
    th = tq // 2
    nb = l // tq
    nh = l // th
    ng = n_heads // hps
    assert nb >= 2 and l % tq == 0 and n_heads % hps == 0

    def btype(i):
        return jnp.where(i == 0, 0, jnp.where(i == nb - 1, 2, 1))

    def cur(off):
        return pl.BlockSpec((1, tq, wd), lambda h, bi, i: (bi, i, off + h))

    def prev(off):
        return pl.BlockSpec((1, th, wd), lambda h, bi, i: (bi, jnp.maximum(2 * i - 1, 0), off + h))

    def nxt(off):
        return pl.BlockSpec((1, th, wd), lambda h, bi, i: (bi, jnp.minimum(2 * i + 2, nh - 1), off + h))

    def ctx(off):
        return pl.BlockSpec((1, lc, wd), lambda h, bi, i: (bi, 0, off + h))

    ko, vo = ng, 2 * ng
    return pl.pallas_call(
        _na_kernel,
        out_shape=jax.ShapeDtypeStruct((b, l, n_heads * HEAD_DIM), BF16),
        grid=(ng, b, nb),
        in_specs=[cur(0), prev(ko), cur(ko), nxt(ko), prev(vo), cur(vo), nxt(vo), ctx(ko), ctx(vo),
                  pl.BlockSpec((1, hps, tq, 2 * tq), lambda h, bi, i: (btype(i), h, 0, 0))],
        out_specs=pl.BlockSpec((1, tq, wd), lambda h, bi, i: (bi, i, h)),
        compiler_params=_cparams(("parallel", "parallel", "arbitrary")),
        name="na_attention",
    )(qkv, qkv, qkv, qkv, qkv, qkv, qkv, qkv_c, qkv_c, bias_tab)


def _ctx_attn_kernel(q_ref, k_ref, v_ref, o_ref):
    s = lax.dot_general(q_ref[0], k_ref[0], NT_DIMS, preferred_element_type=F32)
    m = jnp.max(s, axis=-1, keepdims=True)
    p = jnp.exp2(s - m)
    den = jnp.sum(p, axis=-1, keepdims=True)
    o_ref[0] = (jnp.dot(p.astype(BF16), v_ref[0], preferred_element_type=F32) / den).astype(o_ref.dtype)


def _ctx_attention(qkv_c, n_heads):
    b, lc, _ = qkv_c.shape
    hd = HEAD_DIM

    def spec(off):
        return pl.BlockSpec((1, lc, hd), lambda bi, h: (bi, 0, off + h))

    return pl.pallas_call(
        _ctx_attn_kernel,
        out_shape=jax.ShapeDtypeStruct((b, lc, n_heads * hd), F32),
        grid=(b, n_heads),
        in_specs=[spec(0), spec(n_heads), spec(2 * n_heads)],
        out_specs=spec(0),
        compiler_params=_cparams(("parallel", "parallel")),
        name="ctx_attention",
    )(qkv_c, qkv_c, qkv_c)


def _cos_sin(n_out, n_in, period):
    ang = 2.0 * np.pi * ((np.arange(n_out)[:, None] * np.arange(n_in)[None, :]) % period) / period
    return np.cos(ang), np.sin(ang)


def _lane_cat(x3):
    return jnp.concatenate([x3[j] for j in range(x3.shape[0])], axis=1)


def _rows_from_lanes(re, im, nc):
    return jnp.concatenate(
        [jnp.concatenate([re[:, j * LANES:(j + 1) * LANES], im[:, j * LANES:(j + 1) * LANES]], axis=1)
         for j in range(nc)], axis=0)


def _lanes_from_rows(x, nc, r):
    re = jnp.concatenate([x[j * r:(j + 1) * r, :LANES] for j in range(nc)], axis=1)
    im = jnp.concatenate([x[j * r:(j + 1) * r, LANES:] for j in range(nc)], axis=1)
    return jnp.concatenate([re, im], axis=0)


def _fourier_tables(r, nc):
    l = r * LANES
    c1, s1 = _cos_sin(r, r, r)
    m1 = np.block([[c1, s1], [-s1, c1]])
    tc, ts = _cos_sin(r, LANES, l)
    c2, s2 = _cos_sin(LANES, LANES, LANES)
    m2 = np.concatenate([c2, s2], axis=0) / math.sqrt(l)
    return (_const_bf16(m1), jnp.asarray(np.tile(tc, (1, nc)), F32), jnp.asarray(np.tile(ts, (1, nc)), F32),
            _const_bf16(m2))


def _fourier_kernel(zr_ref, zi_ref, m1_ref, tc_ref, ts_ref, m2_ref, o_ref, *, nc, r):
    z = jnp.concatenate([_lane_cat(zr_ref[0]), _lane_cat(zi_ref[0])], axis=0)
    a = _dot1(m1_ref[...], z)
    ar, ai = a[:r], a[r:]
    tc, ts = tc_ref[...], ts_ref[...]
    br = (ar * tc + ai * ts).astype(BF16)
    bi = (ai * tc - ar * ts).astype(BF16)
    y = _dot1(_rows_from_lanes(br, bi, nc), m2_ref[...])
    for j in range(nc):
        o_ref[0, j] = y[j * r:(j + 1) * r].T.astype(o_ref.dtype)


def _fourier_latent(pt, n_ch, nc):
    b, _, l = pt.shape
    r = l // LANES
    p4 = pt.reshape(b, pt.shape[1], r, LANES)
    m1, tc, ts, m2 = _fourier_tables(r, nc)
    nblk = n_ch // nc

    def const(a):
        return pl.BlockSpec(a.shape, lambda bi, c: (0,) * a.ndim)

    out = pl.pallas_call(
        functools.partial(_fourier_kernel, nc=nc, r=r),
        out_shape=jax.ShapeDtypeStruct((b, n_ch, LANES, r), BF16),
        grid=(b, nblk),
        in_specs=[
            pl.BlockSpec((1, nc, r, LANES), lambda bi, c: (bi, c, 0, 0)),
            pl.BlockSpec((1, nc, r, LANES), lambda bi, c: (bi, nblk + c, 0, 0)),
            const(m1), const(tc), const(ts), const(m2),
        ],
        out_specs=pl.BlockSpec((1, nc, LANES, r), lambda bi, c: (bi, c, 0, 0)),
        compiler_params=_cparams(("parallel", "parallel")),
        name="fourier_latent",
    )(p4, p4, m1, tc, ts, m2)
    return out.reshape(b, n_ch, l)


def _filter_kernel(bands_ref, w1t_ref, w1c_ref, w1s_ref, b1_ref, w2_ref, b2_ref, w3_ref, fr_ref, dl_ref,
                   o_ref, *, l, tl):
    i = pl.program_id(0)
    n = (i * tl + lax.broadcasted_iota(jnp.int32, (1, tl), 1))
    pos_i = jnp.where(n < l, n, 2 * l - n)
    pos = pos_i.astype(F32)
    t = pos / float(max(l - 1, 1))
    ang = bands_ref[...] * (2.0 * math.pi / l) * pos
    fr = fr_ref[...]
    cos_a, sin_a = jnp.cos(ang), jnp.sin(ang)
    w1c, w1s = w1c_ref[...], w1s_ref[...]
    pre = w1t_ref[...] * t
    for k in range(HY_BANDS):
        pre = pre + w1c[:, k:k + 1] * cos_a[k:k + 1, :] - w1s[:, k:k + 1] * sin_a[k:k + 1, :]
    kpad = jnp.zeros((w2_ref.shape[1] - w2_ref.shape[0], tl), F32)
    h1 = jnp.sin(fr * (pre + b1_ref[...]))
    h2 = jnp.sin(fr * (_dot3k(w2_ref[...], jnp.concatenate([h1, kpad], axis=0)) + b2_ref[...]))
    out = _dot3k(w3_ref[0], jnp.concatenate([h2, kpad], axis=0)) * jnp.exp(-t * dl_ref[...])
    o_ref[...] = jnp.where(n == l, 0.0, out).astype(o_ref.dtype)


def _hyena_filters(l, w1, b1, w2, b2, w3, freq, width):
    hid = w1.shape[1]
    kdim = -(-hid // LANES) * LANES
    pad = kdim - hid
    tl = min(2048, l)
    rows = HY_ORDER * width
    bands = np.linspace(1e-4, HY_BANDS - 1, HY_BANDS, dtype=np.float32).reshape(HY_BANDS, 1)
    deltas = np.abs(np.linspace(HY_MIN_DECAY, HY_MAX_DECAY, width, dtype=np.float32))
    dl = np.tile(deltas, HY_ORDER).reshape(rows, 1)
    w1t = w1.T
    w2t = jnp.pad(w2.T, ((0, 0), (0, pad)))
    w3d = w3.reshape(hid, HY_ORDER, 2, width).transpose(2, 1, 3, 0).reshape(2, rows, hid)
    w3d = jnp.pad(w3d, ((0, 0), (0, 0), (0, pad)))
    col = lambda v: v.reshape(hid, 1)

    def const(shape):
        return pl.BlockSpec(shape, lambda i: (0,) * len(shape))

    return pl.pallas_call(
        functools.partial(_filter_kernel, l=l, tl=tl),
        out_shape=jax.ShapeDtypeStruct((rows, 2 * l), BF16),
        grid=(2 * l // tl,),
        in_specs=[const((HY_BANDS, 1)), const((hid, 1)), const((hid, HY_BANDS)), const((hid, HY_BANDS)),
                  const((hid, 1)), const((hid, kdim)), const((hid, 1)),
                  pl.BlockSpec((1, rows, kdim), lambda i: (i // (l // tl), 0, 0)),
                  const((hid, 1)), const((rows, 1))],
        out_specs=pl.BlockSpec((rows, tl), lambda i: (0, i)),
        compiler_params=_cparams(("parallel",)),
        name="hyena_filters",
    )(jnp.asarray(bands), w1t[:, 0:1], w1t[:, 1:1 + HY_BANDS], w1t[:, 1 + HY_BANDS:], col(b1), w2t, col(b2),
      w3d, col(freq), jnp.asarray(dl))


def _conv_tables(r, nc):
    r2 = 2 * r
    n = r2 * LANES
    c1, s1 = _cos_sin(r2, r2, r2)
    c1h, s1h = c1[:, :r], s1[:, :r]
    m1 = np.block([[c1h, s1h], [-s1h, c1h]])
    m1_real = np.concatenate([c1, -s1], axis=0)
    tc, ts = _cos_sin(r2, LANES, n)
    c2, s2 = _cos_sin(LANES, LANES, LANES)
    m2 = np.block([[c2, -s2], [s2, c2]])
    m2i = np.block([[c2, s2], [-s2, c2]])
    ct, st = c1h.T, s1h.T
    m1i = np.block([[ct, -st], [st, ct]]) / n
    f32 = lambda a: jnp.asarray(a, F32)
    return dict(
        m1=_const_bf16(m1), m1_real=_const_bf16(m1_real), m2=_const_bf16(m2), m2i=_const_bf16(m2i),
        m1i=_const_bf16(m1i),
        tc_l=f32(np.tile(tc, (1, nc))), ts_l=f32(np.tile(ts, (1, nc))),
        tc_r=f32(np.tile(tc, (nc, 1))), ts_r=f32(np.tile(ts, (nc, 1))))


def _kernel_spectrum(k3, m1_real, tc_l, ts_l, m2, nc, r2):
    a = _dot1(m1_real, _lane_cat(k3))
    ar, ai = a[:r2], a[r2:]
    br = (ar * tc_l + ai * ts_l).astype(BF16)
    bi = (ai * tc_l - ar * ts_l).astype(BF16)
    return _dot1(_rows_from_lanes(br, bi, nc), m2)


def _shift_tokens(t, r, direction):
    w = t.shape[1]
    lane = lax.broadcasted_iota(jnp.int32, t.shape, 1) & (LANES - 1)
    row = lax.broadcasted_iota(jnp.int32, t.shape, 0)
    if direction < 0:
        near = pltpu.roll(t, 1, 1)
        wrap = pltpu.roll(pltpu.roll(t, w - (LANES - 1), 1), 1, 0)
        edge = lane == 0
        dead = edge & (row == 0)
    else:
        near = pltpu.roll(t, w - 1, 1)
        wrap = pltpu.roll(pltpu.roll(t, LANES - 1, 1), r - 1, 0)
        edge = lane == LANES - 1
        dead = edge & (row == r - 1)
    return jnp.where(dead, 0.0, jnp.where(edge, wrap, near))


def _long_conv(zr, zi, kspec, m1, m2, m2i, m1i, tc_l, ts_l, tc_r, ts_r, nc, r):
    r2 = 2 * r
    a = _dot1(m1, jnp.concatenate([zr.astype(BF16), zi.astype(BF16)], axis=0))
    ar, ai = a[:r2], a[r2:]
    br = (ar * tc_l + ai * ts_l).astype(BF16)
    bi = (ai * tc_l - ar * ts_l).astype(BF16)
    x = _dot1(_rows_from_lanes(br, bi, nc), m2)
    xr, xi = x[:, :LANES], x[:, LANES:]
    kr, ki = kspec[:, :LANES], kspec[:, LANES:]
    y = jnp.concatenate([(xr * kr - xi * ki).astype(BF16), (xr * ki + xi * kr).astype(BF16)], axis=1)
    bb = _dot1(y, m2i)
    pr, pi = bb[:, :LANES], bb[:, LANES:]
    q = jnp.concatenate([(pr * tc_r - pi * ts_r).astype(BF16), (pi * tc_r + pr * ts_r).astype(BF16)], axis=1)
    out = _dot1(m1i, _lanes_from_rows(q, nc, r2))
    return out[:r], out[r:]


def _hyena_kernel(v0, v1, a0, a1, b0, b1, k0_ref, k1_ref, cw_ref, cb_ref, bias_ref,
                  m1_ref, m2_ref, m2i_ref, m1i_ref, tcl_ref, tsl_ref, tcr_ref, tsr_ref, m1r_ref, o_ref, *, nc, r):
    tabs = (m1_ref[...], m2_ref[...], m2i_ref[...], m1i_ref[...],
            tcl_ref[...], tsl_ref[...], tcr_ref[...], tsr_ref[...])

    def spectrum(k_ref):
        return _kernel_spectrum(k_ref[...], m1r_ref[...], tcl_ref[...], tsl_ref[...], m2_ref[...], nc, 2 * r)

    def short_conv(ref, part):
        t = _lane_cat(ref[0]).astype(F32)
        w = cw_ref[part]
        return (_shift_tokens(t, r, -1) * w[0:1] + t * w[1:2] + _shift_tokens(t, r, +1) * w[2:3]
                + cb_ref[part])

    vr, vi = short_conv(v0, 0), short_conv(v1, 0)
    x1r, x1i = short_conv(a0, 1), short_conv(a1, 1)
    x2r, x2i = short_conv(b0, 2), short_conv(b1, 2)
    bias = bias_ref[...]
    yr, yi = _long_conv(vr, vi, spectrum(k0_ref), *tabs, nc, r)
    zr = x1r * (yr + vr * bias[0:1])
    zi = x1i * (yi + vi * bias[0:1])
    yr, yi = _long_conv(zr, zi, spectrum(k1_ref), *tabs, nc, r)
    outr = x2r * (yr + zr * bias[1:2])
    outi = x2i * (yi + zi * bias[1:2])
    for j in range(nc):
        o_ref[0, j] = outr[:, j * LANES:(j + 1) * LANES].astype(o_ref.dtype)
        o_ref[1, j] = outi[:, j * LANES:(j + 1) * LANES].astype(o_ref.dtype)


def _hyena_latent(pt, ch0, width, k2t, conv_w, conv_b, bias, tabs, nc):
    b, c_all, l = pt.shape
    assert b == 2, "the two batch entries ride one complex transform"
    r = l // LANES
    r2 = 2 * r
    p4 = pt.reshape(b, c_all, r, LANES)
    k3 = k2t.reshape(k2t.shape[0], r2, LANES)
    nblk = width // nc
    rep = lambda a: jnp.repeat(a, LANES, axis=-1)
    cw = rep(conv_w.reshape(3, 3, width).transpose(1, 0, 2))
    cb = rep(conv_b.reshape(3, 1, width))
    bs = rep(bias)

    def inp(bi, part):
        off = (ch0 + part * width) // nc
        return pl.BlockSpec((1, nc, r, LANES), lambda c: (bi, off + c, 0, 0))

    def const(a):
        return pl.BlockSpec(a.shape, lambda c: (0,) * a.ndim)

    names = ("m1", "m2", "m2i", "m1i", "tc_l", "ts_l", "tc_r", "ts_r", "m1_real")
    consts = [tabs[k] for k in names]
    out = pl.pallas_call(
        functools.partial(_hyena_kernel, nc=nc, r=r),
        out_shape=jax.ShapeDtypeStruct((b, width, r, LANES), BF16),
        grid=(nblk,),
        in_specs=[inp(0, 0), inp(1, 0), inp(0, 1), inp(1, 1), inp(0, 2), inp(1, 2),
                  pl.BlockSpec((nc, r2, LANES), lambda c: (c, 0, 0)),
                  pl.BlockSpec((nc, r2, LANES), lambda c: (nblk + c, 0, 0)),
                  pl.BlockSpec((3, 3, nc * LANES), lambda c: (0, 0, c)),
                  pl.BlockSpec((3, 1, nc * LANES), lambda c: (0, 0, c)),
                  pl.BlockSpec((HY_ORDER, nc * LANES), lambda c: (0, c))]
                 + [const(a) for a in consts],
        out_specs=pl.BlockSpec((b, nc, r, LANES), lambda c: (0, c, 0, 0)),
        compiler_params=_cparams(("parallel",)),
        name="hyena_latent",
    )(p4, p4, p4, p4, p4, p4, k3, k3, cw, cb, bs, *consts)
    return out.reshape(b, width, l)


def _ctx_fourier_kernel(zr_ref, zi_ref, m_ref, o_ref):
    z = jnp.concatenate([zr_ref[0], zi_ref[0]], axis=1).astype(F32)
    o_ref[0] = _dot_data_const(z, m_ref[...])


def _ctx_fourier(ptc, n_ch):
    b, _, lc = ptc.shape
    c, s = _cos_sin(lc, lc, lc)
    m = _const_rhs3(np.concatenate([c, s], axis=0) / math.sqrt(lc))
    return pl.pallas_call(
        _ctx_fourier_kernel,
        out_shape=jax.ShapeDtypeStruct((b, n_ch, lc), F32),
        grid=(b,),
        in_specs=[pl.BlockSpec((1, n_ch, lc), lambda bi: (bi, 0, 0)),
                  pl.BlockSpec((1, n_ch, lc), lambda bi: (bi, 1, 0)),
                  pl.BlockSpec(m.shape, lambda bi: (0, 0))],
        out_specs=pl.BlockSpec((1, n_ch, lc), lambda bi: (bi, 0, 0)),
        compiler_params=_cparams(("parallel",)),
        name="ctx_fourier",
    )(ptc, ptc, m)


def _ctx_hyena_kernel(v0, v1, a0, a1, b0, b1, k_ref, cw_ref, cb_ref, bias_ref, mk_ref, mf_ref, mi_ref,
                      o_ref, *, lc, width):
    def short_conv(ref, part):
        t = ref[0].astype(F32)
        lane = lax.broadcasted_iota(jnp.int32, t.shape, 1)
        prv = jnp.where(lane == 0, 0.0, pltpu.roll(t, 1, 1))
        nxt = jnp.where(lane == lc - 1, 0.0, pltpu.roll(t, lc - 1, 1))
        w = cw_ref[part]
        return prv * w[:, 0:1] + t * w[:, 1:2] + nxt * w[:, 2:3] + cb_ref[part]

    kspec = _dot_data_const(k_ref[...].astype(F32), mk_ref[...])
    n = 2 * lc

    def long_conv(zr, zi, ks):
        x = _dot_data_const(jnp.concatenate([zr, zi], axis=1), mf_ref[...])
        xr, xi = x[:, :n], x[:, n:]
        kr, ki = ks[:, :n], ks[:, n:]
        y = jnp.concatenate([xr * kr - xi * ki, xr * ki + xi * kr], axis=1)
        out = _dot_data_const(y, mi_ref[...])
        return out[:, :lc], out[:, lc:]

    vr, vi = short_conv(v0, 0), short_conv(v1, 0)
    x1r, x1i = short_conv(a0, 1), short_conv(a1, 1)
    x2r, x2i = short_conv(b0, 2), short_conv(b1, 2)
    bias = bias_ref[...]
    yr, yi = long_conv(vr, vi, kspec[:width])
    zr = x1r * (yr + vr * bias[:, 0:1])
    zi = x1i * (yi + vi * bias[:, 0:1])
    yr, yi = long_conv(zr, zi, kspec[width:])
    o_ref[0] = x2r * (yr + zr * bias[:, 1:2])
    o_ref[1] = x2i * (yi + zi * bias[:, 1:2])


def _ctx_hyena(ptc, ch0, width, k2t, conv_w, conv_b, bias):
    b, _, lc = ptc.shape
    assert b == 2
    n = 2 * lc
    c, s = _cos_sin(n, n, n)
    mk = _const_rhs3(np.concatenate([c, -s], axis=1))
    ch, sh = c[:lc], s[:lc]
    mf = _const_rhs3(np.block([[ch, -sh], [sh, ch]]))
    ci, si = c[:, :lc], s[:, :lc]
    mi = _const_rhs3(np.block([[ci, si], [-si, ci]]) / n)
    cw = conv_w.reshape(3, 3, width).transpose(1, 2, 0)
    cb = conv_b.reshape(3, width, 1)

    def inp(bi, part):
        return pl.BlockSpec((1, width, lc), lambda i: (bi, ch0 // width + part, 0))

    def const(a):
        return pl.BlockSpec(a.shape, lambda i: (0,) * a.ndim)

    args = (k2t, cw, cb, bias.T, mk, mf, mi)
    return pl.pallas_call(
        functools.partial(_ctx_hyena_kernel, lc=lc, width=width),
        out_shape=jax.ShapeDtypeStruct((b, width, lc), F32),
        grid=(1,),
        in_specs=[inp(0, 0), inp(1, 0), inp(0, 1), inp(1, 1), inp(0, 2), inp(1, 2)] + [const(a) for a in args],
        out_specs=pl.BlockSpec((b, width, lc), lambda i: (0, 0, 0)),
        compiler_params=_cparams(("arbitrary",)),
        name="ctx_hyena",
    )(ptc, ptc, ptc, ptc, ptc, ptc, *args)


def _merge_kernel(a_ref, f_ref, hy_ref, x_ref, gate_ref, ga_ref, gf_ref, gh_ref, wa_ref, wf_ref, wh_ref, o_ref):
    a = a_ref[0].astype(F32)
    ya = a * lax.rsqrt(jnp.mean(a * a, axis=-1, keepdims=True) + EPS) * ga_ref[...]
    acc = jnp.dot(ya.astype(BF16), wa_ref[...], preferred_element_type=F32)

    def cm_part(ref, g_ref, w_ref):
        t = ref[0].astype(F32)
        y = t * lax.rsqrt(jnp.mean(t * t, axis=0, keepdims=True) + EPS) * g_ref[...]
        return jnp.dot(y.T.astype(BF16), w_ref[...], preferred_element_type=F32)

    acc = acc + cm_part(f_ref, gf_ref, wf_ref) + cm_part(hy_ref, gh_ref, wh_ref)
    o_ref[0] = x_ref[0] + gate_ref[0] * acc


def _merge_out(a, ft, ht, x, gate, g, w_all, layer, tm):
    b, l, d = x.shape
    wa_n, wf_n, wh_n = a.shape[2], ft.shape[1], ht.shape[1]
    assert wa_n % wf_n == 0 and wf_n == wh_n
    tm = min(tm, l)
    ga = g[:wa_n].reshape(1, wa_n)
    gf = g[wa_n:wa_n + wf_n].reshape(wf_n, 1)
    gh = g[wa_n + wf_n:].reshape(wh_n, 1)

    def const(arr):
        return pl.BlockSpec(arr.shape, lambda bi, i: (0,) * arr.ndim)

    def w_rows(n, blk):
        return pl.BlockSpec((None, n, d), lambda bi, i: (layer, blk, 0))

    return pl.pallas_call(
        _merge_kernel,
        out_shape=jax.ShapeDtypeStruct((b, l, d), F32),
        grid=(b, l // tm),
        in_specs=[pl.BlockSpec((1, tm, wa_n), lambda bi, i: (bi, i, 0)),
                  pl.BlockSpec((1, wf_n, tm), lambda bi, i: (bi, 0, i)),
                  pl.BlockSpec((1, wh_n, tm), lambda bi, i: (bi, 0, i)),
                  pl.BlockSpec((1, tm, d), lambda bi, i: (bi, i, 0)),
                  pl.BlockSpec((1, 1, d), lambda bi, i: (bi, 0, 0)),
                  const(ga), const(gf), const(gh),
                  w_rows(wa_n, 0), w_rows(wf_n, wa_n // wf_n), w_rows(wh_n, wa_n // wf_n + 1)],
        out_specs=pl.BlockSpec((1, tm, d), lambda bi, i: (bi, i, 0)),
        compiler_params=_cparams(("parallel", "parallel")),
        name="merge_out",
    )(a, ft, ht, x, gate.reshape(b, 1, d), ga, gf, gh, w_all, w_all, w_all)


HALO = 16


def _ffn_kernel(xp_ref, x_ref, xn_ref, g_ref, sh_ref, sc_ref, gate_ref, wg_ref, wu_ref, cw_ref, cb_ref, wd_ref,
                fg_ref, o_ref, h_ref, *, tm, final_norm):
    i = pl.program_id(1)
    c = pl.program_id(2)
    last_tile = pl.num_programs(1) - 1

    n_ext = tm + 2 * HALO

    def down_partial(ge, up):
        row = lax.broadcasted_iota(jnp.int32, ge.shape, 0)
        outside = ((row < HALO) & (i == 0)) | ((row >= HALO + tm) & (i == last_tile))
        ge = jnp.where(outside, 0.0, ge)
        gp = pltpu.roll(ge, 1, 0)[HALO:HALO + tm]
        gn = pltpu.roll(ge, n_ext - 1, 0)[HALO:HALO + tm]
        cw = cw_ref[...]
        conv = gp * cw[0:1] + ge[HALO:HALO + tm] * cw[1:2] + gn * cw[2:3] + cb_ref[...]
        inner = 0.7978845608028654 * (conv + 0.044715 * (conv * conv * conv))
        act = 0.5 * conv * (1.0 + jnp.tanh(inner)) * up
        return jnp.dot(act.astype(BF16), wd_ref[...], preferred_element_type=F32)

    @pl.when(c == 0)
    def _():
        g, sh, sc = g_ref[...], sh_ref[0], sc_ref[0]
        wg, wu = wg_ref[...], wu_ref[...]
        rs = tm // NORM_SLABS
        bounds = [(0, HALO)] + [(HALO + s * rs, HALO + (s + 1) * rs) for s in range(NORM_SLABS)] + [(HALO + tm, n_ext)]
        ge_parts, up_parts = [], []
        for lo, hi in bounds:
            if lo == 0:
                xs = xp_ref[0, 0]
            elif hi == n_ext:
                xs = xn_ref[0, 0]
            else:
                xs = x_ref[0, lo - HALO:hi - HALO, :]
            h = _norm_mod(xs, g, sh, sc).astype(BF16)
            h_ref[lo:hi, :] = h
            ge_parts.append(jnp.dot(h, wg, preferred_element_type=F32))
            if lo != 0 and hi != n_ext:
                up_parts.append(jnp.dot(h, wu, preferred_element_type=F32))
        o_ref[0] = down_partial(jnp.concatenate(ge_parts, axis=0), jnp.concatenate(up_parts, axis=0))

    @pl.when(c > 0)
    def _():
        h = h_ref[...]
        ge = jnp.dot(h, wg_ref[...], preferred_element_type=F32)
        up = jnp.dot(h[HALO:HALO + tm], wu_ref[...], preferred_element_type=F32)
        o_ref[0] += down_partial(ge, up)

    @pl.when(c == pl.num_programs(2) - 1)
    def _():
        y = x_ref[0] + gate_ref[0] * o_ref[0]
        if final_norm:
            y = y * lax.rsqrt(jnp.mean(y * y, axis=-1, keepdims=True) + EPS) * fg_ref[...]
        o_ref[0] = y


def _ffn(x, g, sh, sc, gate, w_up, conv_w, conv_b, w_down, layer, final_g, tm, tf, *, final_norm):
    b, l, d = x.shape
    dff = w_down.shape[1]
    tm = min(tm, l)
    nch = dff // tf
    x4 = x.reshape(b, l // HALO, HALO, d)
    per_tile = tm // HALO
    nhalo = l // HALO
    vec = pl.BlockSpec((1, 1, d), lambda bi, i, c: (bi, 0, 0))
    return pl.pallas_call(
        functools.partial(_ffn_kernel, tm=tm, final_norm=final_norm),
        out_shape=jax.ShapeDtypeStruct((b, l, d), F32),
        grid=(b, l // tm, nch),
        in_specs=[
            pl.BlockSpec((1, 1, HALO, d), lambda bi, i, c: (bi, jnp.maximum(i * per_tile - 1, 0), 0, 0)),
            pl.BlockSpec((1, tm, d), lambda bi, i, c: (bi, i, 0), pipeline_mode=pl.Buffered(1)),
            pl.BlockSpec((1, 1, HALO, d), lambda bi, i, c: (bi, jnp.minimum((i + 1) * per_tile, nhalo - 1), 0, 0)),
            pl.BlockSpec((1, d), lambda bi, i, c: (0, 0)),
            vec, vec, vec,
            pl.BlockSpec((None, d, tf), lambda bi, i, c: (layer, 0, c)),
            pl.BlockSpec((None, d, tf), lambda bi, i, c: (layer, 0, nch + c)),
            pl.BlockSpec((3, tf), lambda bi, i, c: (0, c)),
            pl.BlockSpec((1, tf), lambda bi, i, c: (0, c)),
            pl.BlockSpec((None, tf, d), lambda bi, i, c: (layer, c, 0)),
            pl.BlockSpec((1, d), lambda bi, i, c: (0, 0)),
        ],
        out_specs=pl.BlockSpec((1, tm, d), lambda bi, i, c: (bi, i, 0), pipeline_mode=pl.Buffered(1)),
        scratch_shapes=[pltpu.VMEM((tm + 2 * HALO, d), BF16)],
        compiler_params=_cparams(("parallel", "parallel", "arbitrary")),
        name="ffn",
    )(x4, x, x4, g.reshape(1, d), sh.reshape(b, 1, d), sc.reshape(b, 1, d), gate.reshape(b, 1, d),
      w_up, w_up, conv_w, conv_b.reshape(1, dff), w_down, final_g.reshape(1, d))


TOKEN_TILE = 512
PROJ_TOKEN_TILE = 1024
PROJ_COL_TILE = 512
FFN_TOKEN_TILE = 1024
FFN_COL_TILE = 512
MIX_CHANNELS = 8


def kernel(x, c, ctx, c_ctx, ada_w, ada_b, norm1_g, norm2_g, w_in, na_rpb, hy_conv_w, hy_conv_b, hy_w1, hy_b1,
           hy_w2, hy_b2, hy_w3, hy_freq, hy_bias, mix_norm_g, w_out, ffn_w_up, ffn_conv_w, ffn_conv_b, ffn_w_down,
           final_norm_g):
    b, l, d = x.shape
    lc = ctx.shape[1]
    depth = ada_w.shape[0]
    na_w, fn_w = d // 2, d // 4
    hy_w = d - na_w - fn_w
    n_heads = na_w // HEAD_DIM
    qkv_w = 3 * na_w
    rows = l // GRID_W
    assert b + 1 <= 8 and l % (NA_ROWS_PER_BLOCK * GRID_W) == 0 and rows >= 2 * NA_ROWS_PER_BLOCK
    r = l // LANES
    nc = MIX_CHANNELS
    conv_tabs = _conv_tables(r, nc)

    s_in = jnp.concatenate([c, c_ctx[None], jnp.zeros((8 - b - 1, d), F32)], axis=0)
    mod_all = _ada_mod(s_in, ada_w, ada_b)
    xc = ctx
    w_out_bf, w_up_bf, w_down_bf = w_out.astype(BF16), ffn_w_up.astype(BF16), ffn_w_down.astype(BF16)
    for layer in range(depth):
        update_ctx = layer < depth - 1
        mod = mod_all[layer]
        sh1, sc1, g1, sh2, sc2, g2 = jnp.split(mod[:b], N_MOD, axis=-1)
        csh1, csc1, cg1, csh2, csc2, cg2 = jnp.split(jnp.broadcast_to(mod[b:b + 1], (b, N_MOD * d)), N_MOD, axis=-1)

        wl = w_in[layer]
        w_fold = _fold_fourier_weights(wl[:, qkv_w:qkv_w + fn_w])
        wt = jnp.concatenate([(wl[:, :na_w] * (HEAD_DIM ** -0.5 * LOG2E)).T, wl[:, na_w:qkv_w].T, w_fold[0].T, w_fold[1].T,
                              wl[:, qkv_w + fn_w:].T], axis=0).astype(BF16)
        hy0 = 2 * fn_w

        qkv, pt = _proj(x, norm1_g[layer], sh1, sc1, wt, qkv_w, tm=PROJ_TOKEN_TILE, tn=PROJ_COL_TILE)
        qkv_c, ptc = _proj(xc, norm1_g[layer], csh1, csc1, wt, qkv_w, tm=PROJ_TOKEN_TILE, tn=PROJ_COL_TILE)

        a = _na_attention(qkv, qkv_c, _na_bias_table(na_rpb[layer], rows), n_heads)
        yf = _fourier_latent(pt, fn_w, nc)
        filt = (hy_w1[layer], hy_b1[layer], hy_w2[layer], hy_b2[layer], hy_w3[layer], hy_freq[layer])
        yh = _hyena_latent(pt, hy0, hy_w, _hyena_filters(l, *filt, hy_w), hy_conv_w[layer], hy_conv_b[layer],
                           hy_bias[layer], conv_tabs, nc)
        x_new = _merge_out(a, yf, yh, x, g1, mix_norm_g[layer], w_out_bf, layer, TOKEN_TILE)

        if update_ctx:
            ac = _ctx_attention(qkv_c, n_heads)
            yfc = _ctx_fourier(ptc, fn_w)
            yhc = _ctx_hyena(ptc, hy0, hy_w, _hyena_filters(lc, *filt, hy_w), hy_conv_w[layer], hy_conv_b[layer],
                             hy_bias[layer])
            xc = _merge_out(ac, yfc, yhc, xc, cg1, mix_norm_g[layer], w_out_bf, layer, TOKEN_TILE)
        x = x_new

        x = _ffn(x, norm2_g[layer], sh2, sc2, g2, w_up_bf, ffn_conv_w[layer], ffn_conv_b[layer], w_down_bf, layer,
                 final_norm_g, FFN_TOKEN_TILE, FFN_COL_TILE, final_norm=not update_ctx)
        if update_ctx:
            xc = _ffn(xc, norm2_g[layer], csh2, csc2, cg2, w_up_bf, ffn_conv_w[layer], ffn_conv_b[layer], w_down_bf,
                      layer, final_norm_g, FFN_TOKEN_TILE, FFN_COL_TILE, final_norm=False)
    return x
```

```python
import functools
import math

import numpy as np
import jax
import jax.numpy as jnp
from jax import lax
from jax.experimental import pallas as pl
from jax.experimental.pallas import tpu as pltpu

F32 = jnp.float32
BF16 = jnp.bfloat16

EPS = 1e-6
HEAD_DIM = 128
GRID_W = 64
NA_KH = 8
NA_KW = 16
NA_ROWS_PER_BLOCK = 8
FN_GROUP_DIM = 128
HY_ORDER = 2
HY_BANDS = 16
HY_FAST_DECAY = 0.3
HY_SLOW_DECAY = 1.5
HY_TARGET = 1e-2
HY_MIN_DECAY = math.log(HY_TARGET) / HY_SLOW_DECAY
HY_MAX_DECAY = math.log(HY_TARGET) / HY_FAST_DECAY
N_MOD = 6
LANES = 128
NEG_INF = -1e30
LOG2E = math.log2(math.e)
VMEM_LIMIT = 60 * 1024 * 1024

NT_DIMS = (((1,), (1,)), ((), ()))


def _cparams(sem):
    return pltpu.CompilerParams(dimension_semantics=sem, vmem_limit_bytes=VMEM_LIMIT)


def _hi_lo(a):
    hi = a.astype(BF16)
    lo = (a - hi.astype(F32)).astype(BF16)
    return hi, lo


def _np_hi_lo(m):
    m = np.asarray(m, np.float32)
    hi = m.astype(BF16)
    lo = (m - hi.astype(np.float32)).astype(BF16)
    return hi, lo


def _const_rhs3(m):
    hi, lo = _np_hi_lo(m)
    return jnp.asarray(np.concatenate([hi, hi, lo], axis=0))


def _const_lhs3(m):
    hi, lo = _np_hi_lo(m)
    return jnp.asarray(np.concatenate([hi, hi, lo], axis=1))


def _dot_data_const(a, c3):
    hi, lo = _hi_lo(a)
    return jnp.dot(jnp.concatenate([hi, lo, hi], axis=1), c3, preferred_element_type=F32)


def _dot_const_data(c3, b):
    hi, lo = _hi_lo(b)
    return jnp.dot(c3, jnp.concatenate([hi, lo, hi], axis=0), preferred_element_type=F32)


def _dot3(a, b):
    ah, al = _hi_lo(a)
    bh, bl = _hi_lo(b)
    return (jnp.dot(ah, bh, preferred_element_type=F32)
            + jnp.dot(al, bh, preferred_element_type=F32)
            + jnp.dot(ah, bl, preferred_element_type=F32))


def _dot3k(a, b):
    ah, al = _hi_lo(a)
    bh, bl = _hi_lo(b)
    return jnp.dot(jnp.concatenate([ah, al, ah], axis=1), jnp.concatenate([bh, bh, bl], axis=0),
                   preferred_element_type=F32)


def _const_bf16(m):
    return jnp.asarray(np.asarray(m, np.float32).astype(BF16))


def _dot1(a, b):
    return jnp.dot(a.astype(BF16), b.astype(BF16), preferred_element_type=F32)


def _ada_kernel(s_ref, w_ref, b_ref, o_ref):
    s = s_ref[...]
    s = s / (1.0 + jnp.exp(-s))
    o_ref[0] = _dot1(s, w_ref[0]) + b_ref[0]


def _ada_mod(s_in, ada_w, ada_b):
    depth, d, n = ada_w.shape
    tn = 1024
    return pl.pallas_call(
        _ada_kernel,
        out_shape=jax.ShapeDtypeStruct((depth, 8, n), F32),
        grid=(depth, n // tn),
        in_specs=[
            pl.BlockSpec((8, d), lambda l, j: (0, 0)),
            pl.BlockSpec((1, d, tn), lambda l, j: (l, 0, j)),
            pl.BlockSpec((1, 1, tn), lambda l, j: (l, 0, j)),
        ],
        out_specs=pl.BlockSpec((1, 8, tn), lambda l, j: (l, 0, j)),
        compiler_params=_cparams(("parallel", "parallel")),
        name="ada_mod",
    )(s_in, ada_w, ada_b.reshape(depth, 1, n))


def _norm_mod(x, g, sh, sc):
    ms = jnp.mean(x * x, axis=-1, keepdims=True)
    return (x * lax.rsqrt(ms + EPS) * g) * (1.0 + sc) + sh


NORM_SLABS = 4


def _proj_kernel(*refs, n_tm):
    x_refs = refs[:NORM_SLABS]
    g_ref, sh_ref, sc_ref, w_ref, otm_ref, ocm_ref, h_ref = refs[NORM_SLABS:]
    j = pl.program_id(2)
    rs = x_refs[0].shape[1]

    @pl.when(j == 0)
    def _():
        g, sh, sc = g_ref[...], sh_ref[0], sc_ref[0]
        w = w_ref[...]
        for s in range(NORM_SLABS):
            h = _norm_mod(x_refs[s][0], g, sh, sc).astype(BF16)
            h_ref[s * rs:(s + 1) * rs, :] = h
            otm_ref[0, s * rs:(s + 1) * rs, :] = lax.dot_general(
                h, w, NT_DIMS, preferred_element_type=F32).astype(otm_ref.dtype)

    @pl.when((j > 0) & (j < n_tm))
    def _():
        otm_ref[0] = lax.dot_general(h_ref[...], w_ref[...], NT_DIMS,
                                     preferred_element_type=F32).astype(otm_ref.dtype)

    @pl.when(j >= n_tm)
    def _():
        y = lax.dot_general(h_ref[...], w_ref[...], NT_DIMS, preferred_element_type=F32)
        ocm_ref[0] = y.T.astype(ocm_ref.dtype)


def _proj(x, g, sh, sc, wt, n_tok, *, tm, tn):
    b, l, d = x.shape
    n = wt.shape[0]
    tm = min(tm, l)
    n_tm = n_tok // tn
    n_cm = (n - n_tok) // tn
    assert n_tm >= 1 and n_cm >= 1 and n_tm * tn == n_tok and tm % (16 * NORM_SLABS) == 0
    vec = pl.BlockSpec((1, 1, d), lambda bi, i, j: (bi, 0, 0))
    rs = tm // NORM_SLABS
    nt, nj = l // tm, n_tm + n_cm
    n_slabs = b * l // rs
    assert nj > NORM_SLABS

    def slab(s):
        def index(bi, i, j):
            tile = bi * nt + i + (j >= nj - s).astype(jnp.int32)
            return (jnp.minimum(tile * NORM_SLABS + s, n_slabs - 1), 0, 0)
        return pl.BlockSpec((1, rs, d), index)

    return pl.pallas_call(
        functools.partial(_proj_kernel, n_tm=n_tm),
        out_shape=(jax.ShapeDtypeStruct((b, l, n_tok), BF16), jax.ShapeDtypeStruct((b, n - n_tok, l), BF16)),
        grid=(b, nt, nj),
        in_specs=[slab(s) for s in range(NORM_SLABS)] + [
            pl.BlockSpec((1, d), lambda bi, i, j: (0, 0)),
            vec, vec,
            pl.BlockSpec((tn, d), lambda bi, i, j: (j, 0)),
        ],
        out_specs=(pl.BlockSpec((1, tm, tn), lambda bi, i, j: (bi, i, jnp.minimum(j, n_tm - 1))),
                   pl.BlockSpec((1, tn, tm), lambda bi, i, j: (bi, jnp.maximum(j - n_tm, 0), i))),
        scratch_shapes=[pltpu.VMEM((tm, d), BF16)],
        compiler_params=_cparams(("parallel", "parallel", "arbitrary")),
        name="proj_in",
    )(*([x.reshape(n_slabs, rs, d)] * NORM_SLABS), g.reshape(1, d), sh.reshape(b, 1, d), sc.reshape(b, 1, d), wt)


def _fold_kernel(w_ref, m_ref, o_ref):
    o_ref[0] = _dot_data_const(w_ref[...], m_ref[...])


def _fold_fourier_weights(w_f):
    d, width = w_f.shape
    gd = FN_GROUP_DIM
    idx = np.arange(gd)
    ang = 2.0 * np.pi * np.outer(idx, idx) / gd
    mat = np.concatenate([np.cos(ang), -np.sin(ang)], axis=1) / math.sqrt(gd)
    groups = width // gd
    return pl.pallas_call(
        _fold_kernel,
        out_shape=jax.ShapeDtypeStruct((2, d, width), F32),
        grid=(groups, 2),
        in_specs=[
            pl.BlockSpec((d, gd), lambda g, p: (0, g)),
            pl.BlockSpec((3 * gd, gd), lambda g, p: (0, p)),
        ],
        out_specs=pl.BlockSpec((1, d, gd), lambda g, p: (p, 0, g)),
        compiler_params=_cparams(("parallel", "parallel")),
        name="fold_fourier",
    )(w_f, _const_rhs3(mat))


def _na_bias_table(rpb, rows):
    h = rpb.shape[0]
    w, kw, kh, rb = GRID_W, NA_KW, NA_KH, NA_ROWS_PER_BLOCK
    col = np.arange(w)
    cs = np.clip(col - kw // 2, 0, w - kw)
    kc = np.arange(w)[None, :]
    col_ok = (kc >= cs[:, None]) & (kc < cs[:, None] + kw)
    padded = jnp.pad(rpb * LOG2E, ((0, 0), (0, 0), (w - kw, w - kw)))
    t1 = jnp.stack([padded[:, :, w - 1 - q:2 * w - 1 - q] for q in range(w)], axis=1)
    t1 = jnp.where(col_ok[None, :, None, :], t1, NEG_INF)
    slots = 2 * rb
    blocks = []
    for t, row0 in enumerate((0, rb, rows - rb)):
        per_row = []
        for qr in range(rb):
            r = row0 + qr
            r0 = r - kh // 2 if t == 1 else min(max(r - kh // 2, 0), rows - kh)
            valid = [kr for kr in range(slots)
                     if r0 <= row0 - rb // 2 + kr < r0 + kh and (t == 1 or 0 <= row0 - rb // 2 + kr < rows)]
            lo, hi = valid[0], valid[-1] + 1
            d0 = (row0 - rb // 2 + lo) - r + (kh - 1)
            live = [t1[:, :, d0 + k, :] for k in range(hi - lo)]
            dead = jnp.full((h, w, w), NEG_INF, t1.dtype)
            per_row.append(jnp.concatenate([dead] * lo + live + [dead] * (slots - hi), axis=-1))
        blocks.append(jnp.stack(per_row, axis=1))
    return jnp.stack(blocks, axis=0).reshape(3, h, rb * w, slots * w)


NA_HEADS_PER_STEP = 4


def _na_kernel(q_ref, kp_ref, kc_ref, kn_ref, vp_ref, vc_ref, vn_ref, kx_ref, vx_ref, bias_ref, o_ref):
    k_all = jnp.concatenate([kp_ref[0], kc_ref[0], kn_ref[0]], axis=0)
    v_all = jnp.concatenate([vp_ref[0], vc_ref[0], vn_ref[0]], axis=0)
    hq = q_ref.shape[1] // 2
    band = (NA_ROWS_PER_BLOCK // 2 + NA_KH) * GRID_W
    for h in range(NA_HEADS_PER_STEP):
        sl = slice(h * HEAD_DIM, (h + 1) * HEAD_DIM)
        kx, vx = kx_ref[0, :, sl], vx_ref[0, :, sl]
        for part in range(2):
            rows = slice(part * hq, (part + 1) * hq)
            keys = slice(part * hq, part * hq + band)
            q = q_ref[0, rows, sl]
            s = lax.dot_general(q, k_all[keys, sl], NT_DIMS, preferred_element_type=F32) + bias_ref[0, h, rows, keys]
            sx = lax.dot_general(q, kx, NT_DIMS, preferred_element_type=F32)
            m = jnp.maximum(jnp.max(s, axis=-1, keepdims=True), jnp.max(sx, axis=-1, keepdims=True))
            p = jnp.exp2(s - m)
            px = jnp.exp2(sx - m)
            den = jnp.sum(p, axis=-1, keepdims=True) + jnp.sum(px, axis=-1, keepdims=True)
            o = (jnp.dot(p.astype(BF16), v_all[keys, sl], preferred_element_type=F32)
                 + jnp.dot(px.astype(BF16), vx, preferred_element_type=F32))
            o_ref[0, rows, sl] = (o / den).astype(o_ref.dtype)


def _na_attention(qkv, qkv_c, bias_tab, n_heads):
    b, l, _ = qkv.shape
    lc = qkv_c.shape[1]
    hps = NA_HEADS_PER_STEP
    wd = hps * HEAD_DIM
    tq = NA_ROWS_PER_BLOCK * GRID_W
    th = tq // 2
    nb = l // tq
    nh = l // th
    ng = n_heads // hps
    assert nb >= 2 and l % tq == 0 and n_heads % hps == 0

    def btype(i):
        return jnp.where(i == 0, 0, jnp.where(i == nb - 1, 2, 1))

    def cur(off):
        return pl.BlockSpec((1, tq, wd), lambda h, bi, i: (bi, i, off + h))

    def prev(off):
        return pl.BlockSpec((1, th, wd), lambda h, bi, i: (bi, jnp.maximum(2 * i - 1, 0), off + h))

    def nxt(off):
        return pl.BlockSpec((1, th, wd), lambda h, bi, i: (bi, jnp.minimum(2 * i + 2, nh - 1), off + h))

    def ctx(off):
        return pl.BlockSpec((1, lc, wd), lambda h, bi, i: (bi, 0, off + h))

    ko, vo = ng, 2 * ng
    return pl.pallas_call(
        _na_kernel,
        out_shape=jax.ShapeDtypeStruct((b, l, n_heads * HEAD_DIM), BF16),
        grid=(ng, b, nb),
        in_specs=[cur(0), prev(ko), cur(ko), nxt(ko), prev(vo), cur(vo), nxt(vo), ctx(ko), ctx(vo),
                  pl.BlockSpec((1, hps, tq, 2 * tq), lambda h, bi, i: (btype(i), h, 0, 0))],
        out_specs=pl.BlockSpec((1, tq, wd), lambda h, bi, i: (bi, i, h)),
        compiler_params=_cparams(("parallel", "parallel", "arbitrary")),
        name="na_attention",
    )(qkv, qkv, qkv, qkv, qkv, qkv, qkv, qkv_c, qkv_c, bias_tab)


def _ctx_attn_kernel(q_ref, k_ref, v_ref, o_ref):
    s = lax.dot_general(q_ref[0], k_ref[0], NT_DIMS, preferred_element_type=F32)
    m = jnp.max(s, axis=-1, keepdims=True)
    p = jnp.exp2(s - m)
    den = jnp.sum(p, axis=-1, keepdims=True)
    o_ref[0] = (jnp.dot(p.astype(BF16), v_ref[0], preferred_element_type=F32) / den).astype(o_ref.dtype)


def _ctx_attention(qkv_c, n_heads):
    b, lc, _ = qkv_c.shape
    hd = HEAD_DIM

    def spec(off):
        return pl.BlockSpec((1, lc, hd), lambda bi, h: (bi, 0, off + h))

    return pl.pallas_call(
        _ctx_attn_kernel,
        out_shape=jax.ShapeDtypeStruct((b, lc, n_heads * hd), F32),
        grid=(b, n_heads),
        in_specs=[spec(0), spec(n_heads), spec(2 * n_heads)],
        out_specs=spec(0),
        compiler_params=_cparams(("parallel", "parallel")),
        name="ctx_attention",
    )(qkv_c, qkv_c, qkv_c)


def _cos_sin(n_out, n_in, period):
    ang = 2.0 * np.pi * ((np.arange(n_out)[:, None] * np.arange(n_in)[None, :]) % period) / period
    return np.cos(ang), np.sin(ang)


def _lane_cat(x3):
    return jnp.concatenate([x3[j] for j in range(x3.shape[0])], axis=1)


def _rows_from_lanes(re, im, nc):
    return jnp.concatenate(
        [jnp.concatenate([re[:, j * LANES:(j + 1) * LANES], im[:, j * LANES:(j + 1) * LANES]], axis=1)
         for j in range(nc)], axis=0)


def _lanes_from_rows(x, nc, r):
    re = jnp.concatenate([x[j * r:(j + 1) * r, :LANES] for j in range(nc)], axis=1)
    im = jnp.concatenate([x[j * r:(j + 1) * r, LANES:] for j in range(nc)], axis=1)
    return jnp.concatenate([re, im], axis=0)


def _fourier_tables(r, nc):
    l = r * LANES
    c1, s1 = _cos_sin(r, r, r)
    m1 = np.block([[c1, s1], [-s1, c1]])
    tc, ts = _cos_sin(r, LANES, l)
    c2, s2 = _cos_sin(LANES, LANES, LANES)
    m2 = np.concatenate([c2, s2], axis=0) / math.sqrt(l)
    return (_const_bf16(m1), jnp.asarray(np.tile(tc, (1, nc)), F32), jnp.asarray(np.tile(ts, (1, nc)), F32),
            _const_bf16(m2))


def _fourier_kernel(zr_ref, zi_ref, m1_ref, tc_ref, ts_ref, m2_ref, o_ref, *, nc, r):
    z = jnp.concatenate([_lane_cat(zr_ref[0]), _lane_cat(zi_ref[0])], axis=0)
    a = _dot1(m1_ref[...], z)
    ar, ai = a[:r], a[r:]
    tc, ts = tc_ref[...], ts_ref[...]
    br = (ar * tc + ai * ts).astype(BF16)
    bi = (ai * tc - ar * ts).astype(BF16)
    y = _dot1(_rows_from_lanes(br, bi, nc), m2_ref[...])
    for j in range(nc):
        o_ref[0, j] = y[j * r:(j + 1) * r].T.astype(o_ref.dtype)


def _fourier_latent(pt, n_ch, nc):
    b, _, l = pt.shape
    r = l // LANES
    p4 = pt.reshape(b, pt.shape[1], r, LANES)
    m1, tc, ts, m2 = _fourier_tables(r, nc)
    nblk = n_ch // nc

    def const(a):
        return pl.BlockSpec(a.shape, lambda bi, c: (0,) * a.ndim)

    out = pl.pallas_call(
        functools.partial(_fourier_kernel, nc=nc, r=r),
        out_shape=jax.ShapeDtypeStruct((b, n_ch, LANES, r), BF16),
        grid=(b, nblk),
        in_specs=[
            pl.BlockSpec((1, nc, r, LANES), lambda bi, c: (bi, c, 0, 0)),
            pl.BlockSpec((1, nc, r, LANES), lambda bi, c: (bi, nblk + c, 0, 0)),
            const(m1), const(tc), const(ts), const(m2),
        ],
        out_specs=pl.BlockSpec((1, nc, LANES, r), lambda bi, c: (bi, c, 0, 0)),
        compiler_params=_cparams(("parallel", "parallel")),
        name="fourier_latent",
    )(p4, p4, m1, tc, ts, m2)
    return out.reshape(b, n_ch, l)


def _filter_kernel(bands_ref, w1t_ref, w1c_ref, w1s_ref, b1_ref, w2_ref, b2_ref, w3_ref, fr_ref, dl_ref,
                   o_ref, *, l, tl):
    i = pl.program_id(0)
    n = (i * tl + lax.broadcasted_iota(jnp.int32, (1, tl), 1))
    pos_i = jnp.where(n < l, n, 2 * l - n)
    pos = pos_i.astype(F32)
    t = pos / float(max(l - 1, 1))
    ang = bands_ref[...] * (2.0 * math.pi / l) * pos
    fr = fr_ref[...]
    cos_a, sin_a = jnp.cos(ang), jnp.sin(ang)
    w1c, w1s = w1c_ref[...], w1s_ref[...]
    pre = w1t_ref[...] * t
    for k in range(HY_BANDS):
        pre = pre + w1c[:, k:k + 1] * cos_a[k:k + 1, :] - w1s[:, k:k + 1] * sin_a[k:k + 1, :]
    kpad = jnp.zeros((w2_ref.shape[1] - w2_ref.shape[0], tl), F32)
    h1 = jnp.sin(fr * (pre + b1_ref[...]))
    h2 = jnp.sin(fr * (_dot3k(w2_ref[...], jnp.concatenate([h1, kpad], axis=0)) + b2_ref[...]))
    out = _dot3k(w3_ref[0], jnp.concatenate([h2, kpad], axis=0)) * jnp.exp(-t * dl_ref[...])
    o_ref[...] = jnp.where(n == l, 0.0, out).astype(o_ref.dtype)


def _hyena_filters(l, w1, b1, w2, b2, w3, freq, width):
    hid = w1.shape[1]
    kdim = -(-hid // LANES) * LANES
    pad = kdim - hid
    tl = min(2048, l)
    rows = HY_ORDER * width
    bands = np.linspace(1e-4, HY_BANDS - 1, HY_BANDS, dtype=np.float32).reshape(HY_BANDS, 1)
    deltas = np.abs(np.linspace(HY_MIN_DECAY, HY_MAX_DECAY, width, dtype=np.float32))
    dl = np.tile(deltas, HY_ORDER).reshape(rows, 1)
    w1t = w1.T
    w2t = jnp.pad(w2.T, ((0, 0), (0, pad)))
    w3d = w3.reshape(hid, HY_ORDER, 2, width).transpose(2, 1, 3, 0).reshape(2, rows, hid)
    w3d = jnp.pad(w3d, ((0, 0), (0, 0), (0, pad)))
    col = lambda v: v.reshape(hid, 1)

    def const(shape):
        return pl.BlockSpec(shape, lambda i: (0,) * len(shape))

    return pl.pallas_call(
        functools.partial(_filter_kernel, l=l, tl=tl),
        out_shape=jax.ShapeDtypeStruct((rows, 2 * l), BF16),
        grid=(2 * l // tl,),
        in_specs=[const((HY_BANDS, 1)), const((hid, 1)), const((hid, HY_BANDS)), const((hid, HY_BANDS)),
                  const((hid, 1)), const((hid, kdim)), const((hid, 1)),
                  pl.BlockSpec((1, rows, kdim), lambda i: (i // (l // tl), 0, 0)),
                  const((hid, 1)), const((rows, 1))],
        out_specs=pl.BlockSpec((rows, tl), lambda i: (0, i)),
        compiler_params=_cparams(("parallel",)),
        name="hyena_filters",
    )(jnp.asarray(bands), w1t[:, 0:1], w1t[:, 1:1 + HY_BANDS], w1t[:, 1 + HY_BANDS:], col(b1), w2t, col(b2),
      w3d, col(freq), jnp.asarray(dl))


def _conv_tables(r, nc):
    r2 = 2 * r
    n = r2 * LANES
    c1, s1 = _cos_sin(r2, r2, r2)
    c1h, s1h = c1[:, :r], s1[:, :r]
    m1 = np.block([[c1h, s1h], [-s1h, c1h]])
    m1_real = np.concatenate([c1, -s1], axis=0)
    tc, ts = _cos_sin(r2, LANES, n)
    c2, s2 = _cos_sin(LANES, LANES, LANES)
    m2 = np.block([[c2, -s2], [s2, c2]])
    m2i = np.block([[c2, s2], [-s2, c2]])
    ct, st = c1h.T, s1h.T
    m1i = np.block([[ct, -st], [st, ct]]) / n
    f32 = lambda a: jnp.asarray(a, F32)
    return dict(
        m1=_const_bf16(m1), m1_real=_const_bf16(m1_real), m2=_const_bf16(m2), m2i=_const_bf16(m2i),
        m1i=_const_bf16(m1i),
        tc_l=f32(np.tile(tc, (1, nc))), ts_l=f32(np.tile(ts, (1, nc))),
        tc_r=f32(np.tile(tc, (nc, 1))), ts_r=f32(np.tile(ts, (nc, 1))))


def _kernel_spectrum(k3, m1_real, tc_l, ts_l, m2, nc, r2):
    a = _dot1(m1_real, _lane_cat(k3))
    ar, ai = a[:r2], a[r2:]
    br = (ar * tc_l + ai * ts_l).astype(BF16)
    bi = (ai * tc_l - ar * ts_l).astype(BF16)
    return _dot1(_rows_from_lanes(br, bi, nc), m2)


def _shift_tokens(t, r, direction):
    w = t.shape[1]
    lane = lax.broadcasted_iota(jnp.int32, t.shape, 1) & (LANES - 1)
    row = lax.broadcasted_iota(jnp.int32, t.shape, 0)
    if direction < 0:
        near = pltpu.roll(t, 1, 1)
        wrap = pltpu.roll(pltpu.roll(t, w - (LANES - 1), 1), 1, 0)
        edge = lane == 0
        dead = edge & (row == 0)
    else:
        near = pltpu.roll(t, w - 1, 1)
        wrap = pltpu.roll(pltpu.roll(t, LANES - 1, 1), r - 1, 0)
        edge = lane == LANES - 1
        dead = edge & (row == r - 1)
    return jnp.where(dead, 0.0, jnp.where(edge, wrap, near))


def _long_conv(zr, zi, kspec, m1, m2, m2i, m1i, tc_l, ts_l, tc_r, ts_r, nc, r):
    r2 = 2 * r
    a = _dot1(m1, jnp.concatenate([zr.astype(BF16), zi.astype(BF16)], axis=0))
    ar, ai = a[:r2], a[r2:]
    br = (ar * tc_l + ai * ts_l).astype(BF16)
    bi = (ai * tc_l - ar * ts_l).astype(BF16)
    x = _dot1(_rows_from_lanes(br, bi, nc), m2)
    xr, xi = x[:, :LANES], x[:, LANES:]
    kr, ki = kspec[:, :LANES], kspec[:, LANES:]
    y = jnp.concatenate([(xr * kr - xi * ki).astype(BF16), (xr * ki + xi * kr).astype(BF16)], axis=1)
    bb = _dot1(y, m2i)
    pr, pi = bb[:, :LANES], bb[:, LANES:]
    q = jnp.concatenate([(pr * tc_r - pi * ts_r).astype(BF16), (pi * tc_r + pr * ts_r).astype(BF16)], axis=1)
    out = _dot1(m1i, _lanes_from_rows(q, nc, r2))
    return out[:r], out[r:]


def _hyena_kernel(v0, v1, a0, a1, b0, b1, k0_ref, k1_ref, cw_ref, cb_ref, bias_ref,
                  m1_ref, m2_ref, m2i_ref, m1i_ref, tcl_ref, tsl_ref, tcr_ref, tsr_ref, m1r_ref, o_ref, *, nc, r):
    tabs = (m1_ref[...], m2_ref[...], m2i_ref[...], m1i_ref[...],
            tcl_ref[...], tsl_ref[...], tcr_ref[...], tsr_ref[...])

    def spectrum(k_ref):
        return _kernel_spectrum(k_ref[...], m1r_ref[...], tcl_ref[...], tsl_ref[...], m2_ref[...], nc, 2 * r)

    def short_conv(ref, part):
        t = _lane_cat(ref[0]).astype(F32)
        w = cw_ref[part]
        return (_shift_tokens(t, r, -1) * w[0:1] + t * w[1:2] + _shift_tokens(t, r, +1) * w[2:3]
                + cb_ref[part])

    vr, vi = short_conv(v0, 0), short_conv(v1, 0)
    x1r, x1i = short_conv(a0, 1), short_conv(a1, 1)
    x2r, x2i = short_conv(b0, 2), short_conv(b1, 2)
    bias = bias_ref[...]
    yr, yi = _long_conv(vr, vi, spectrum(k0_ref), *tabs, nc, r)
    zr = x1r * (yr + vr * bias[0:1])
    zi = x1i * (yi + vi * bias[0:1])
    yr, yi = _long_conv(zr, zi, spectrum(k1_ref), *tabs, nc, r)
    outr = x2r * (yr + zr * bias[1:2])
    outi = x2i * (yi + zi * bias[1:2])
    for j in range(nc):
        o_ref[0, j] = outr[:, j * LANES:(j + 1) * LANES].astype(o_ref.dtype)
        o_ref[1, j] = outi[:, j * LANES:(j + 1) * LANES].astype(o_ref.dtype)


def _hyena_latent(pt, ch0, width, k2t, conv_w, conv_b, bias, tabs, nc):
    b, c_all, l = pt.shape
    assert b == 2, "the two batch entries ride one complex transform"
    r = l // LANES
    r2 = 2 * r
    p4 = pt.reshape(b, c_all, r, LANES)
    k3 = k2t.reshape(k2t.shape[0], r2, LANES)
    nblk = width // nc
    rep = lambda a: jnp.repeat(a, LANES, axis=-1)
    cw = rep(conv_w.reshape(3, 3, width).transpose(1, 0, 2))
    cb = rep(conv_b.reshape(3, 1, width))
    bs = rep(bias)

    def inp(bi, part):
        off = (ch0 + part * width) // nc
        return pl.BlockSpec((1, nc, r, LANES), lambda c: (bi, off + c, 0, 0))

    def const(a):
        return pl.BlockSpec(a.shape, lambda c: (0,) * a.ndim)

    names = ("m1", "m2", "m2i", "m1i", "tc_l", "ts_l", "tc_r", "ts_r", "m1_real")
    consts = [tabs[k] for k in names]
    out = pl.pallas_call(
        functools.partial(_hyena_kernel, nc=nc, r=r),
        out_shape=jax.ShapeDtypeStruct((b, width, r, LANES), BF16),
        grid=(nblk,),
        in_specs=[inp(0, 0), inp(1, 0), inp(0, 1), inp(1, 1), inp(0, 2), inp(1, 2),
                  pl.BlockSpec((nc, r2, LANES), lambda c: (c, 0, 0)),
                  pl.BlockSpec((nc, r2, LANES), lambda c: (nblk + c, 0, 0)),
                  pl.BlockSpec((3, 3, nc * LANES), lambda c: (0, 0, c)),
                  pl.BlockSpec((3, 1, nc * LANES), lambda c: (0, 0, c)),
                  pl.BlockSpec((HY_ORDER, nc * LANES), lambda c: (0, c))]
                 + [const(a) for a in consts],
        out_specs=pl.BlockSpec((b, nc, r, LANES), lambda c: (0, c, 0, 0)),
        compiler_params=_cparams(("parallel",)),
        name="hyena_latent",
    )(p4, p4, p4, p4, p4, p4, k3, k3, cw, cb, bs, *consts)
    return out.reshape(b, width, l)


def _ctx_fourier_kernel(zr_ref, zi_ref, m_ref, o_ref):
    z = jnp.concatenate([zr_ref[0], zi_ref[0]], axis=1).astype(F32)
    o_ref[0] = _dot_data_const(z, m_ref[...])


def _ctx_fourier(ptc, n_ch):
    b, _, lc = ptc.shape
    c, s = _cos_sin(lc, lc, lc)
    m = _const_rhs3(np.concatenate([c, s], axis=0) / math.sqrt(lc))
    return pl.pallas_call(
        _ctx_fourier_kernel,
        out_shape=jax.ShapeDtypeStruct((b, n_ch, lc), F32),
        grid=(b,),
        in_specs=[pl.BlockSpec((1, n_ch, lc), lambda bi: (bi, 0, 0)),
                  pl.BlockSpec((1, n_ch, lc), lambda bi: (bi, 1, 0)),
                  pl.BlockSpec(m.shape, lambda bi: (0, 0))],
        out_specs=pl.BlockSpec((1, n_ch, lc), lambda bi: (bi, 0, 0)),
        compiler_params=_cparams(("parallel",)),
        name="ctx_fourier",
    )(ptc, ptc, m)


def _ctx_hyena_kernel(v0, v1, a0, a1, b0, b1, k_ref, cw_ref, cb_ref, bias_ref, mk_ref, mf_ref, mi_ref,
                      o_ref, *, lc, width):
    def short_conv(ref, part):
        t = ref[0].astype(F32)
        lane = lax.broadcasted_iota(jnp.int32, t.shape, 1)
        prv = jnp.where(lane == 0, 0.0, pltpu.roll(t, 1, 1))
        nxt = jnp.where(lane == lc - 1, 0.0, pltpu.roll(t, lc - 1, 1))
        w = cw_ref[part]
        return prv * w[:, 0:1] + t * w[:, 1:2] + nxt * w[:, 2:3] + cb_ref[part]

    kspec = _dot_data_const(k_ref[...].astype(F32), mk_ref[...])
    n = 2 * lc

    def long_conv(zr, zi, ks):
        x = _dot_data_const(jnp.concatenate([zr, zi], axis=1), mf_ref[...])
        xr, xi = x[:, :n], x[:, n:]
        kr, ki = ks[:, :n], ks[:, n:]
        y = jnp.concatenate([xr * kr - xi * ki, xr * ki + xi * kr], axis=1)
        out = _dot_data_const(y, mi_ref[...])
        return out[:, :lc], out[:, lc:]

    vr, vi = short_conv(v0, 0), short_conv(v1, 0)
    x1r, x1i = short_conv(a0, 1), short_conv(a1, 1)
    x2r, x2i = short_conv(b0, 2), short_conv(b1, 2)
    bias = bias_ref[...]
    yr, yi = long_conv(vr, vi, kspec[:width])
    zr = x1r * (yr + vr * bias[:, 0:1])
    zi = x1i * (yi + vi * bias[:, 0:1])
    yr, yi = long_conv(zr, zi, kspec[width:])
    o_ref[0] = x2r * (yr + zr * bias[:, 1:2])
    o_ref[1] = x2i * (yi + zi * bias[:, 1:2])


def _ctx_hyena(ptc, ch0, width, k2t, conv_w, conv_b, bias):
    b, _, lc = ptc.shape
    assert b == 2
    n = 2 * lc
    c, s = _cos_sin(n, n, n)
    mk = _const_rhs3(np.concatenate([c, -s], axis=1))
    ch, sh = c[:lc], s[:lc]
    mf = _const_rhs3(np.block([[ch, -sh], [sh, ch]]))
    ci, si = c[:, :lc], s[:, :lc]
    mi = _const_rhs3(np.block([[ci, si], [-si, ci]]) / n)
    cw = conv_w.reshape(3, 3, width).transpose(1, 2, 0)
    cb = conv_b.reshape(3, width, 1)

    def inp(bi, part):
        return pl.BlockSpec((1, width, lc), lambda i: (bi, ch0 // width + part, 0))

    def const(a):
        return pl.BlockSpec(a.shape, lambda i: (0,) * a.ndim)

    args = (k2t, cw, cb, bias.T, mk, mf, mi)
    return pl.pallas_call(
        functools.partial(_ctx_hyena_kernel, lc=lc, width=width),
        out_shape=jax.ShapeDtypeStruct((b, width, lc), F32),
        grid=(1,),
        in_specs=[inp(0, 0), inp(1, 0), inp(0, 1), inp(1, 1), inp(0, 2), inp(1, 2)] + [const(a) for a in args],
        out_specs=pl.BlockSpec((b, width, lc), lambda i: (0, 0, 0)),
        compiler_params=_cparams(("arbitrary",)),
        name="ctx_hyena",
    )(ptc, ptc, ptc, ptc, ptc, ptc, *args)


def _merge_kernel(a_ref, f_ref, hy_ref, x_ref, gate_ref, ga_ref, gf_ref, gh_ref, wa_ref, wf_ref, wh_ref, o_ref):
    a = a_ref[0].astype(F32)
    ya = a * lax.rsqrt(jnp.mean(a * a, axis=-1, keepdims=True) + EPS) * ga_ref[...]
    acc = jnp.dot(ya.astype(BF16), wa_ref[...], preferred_element_type=F32)

    def cm_part(ref, g_ref, w_ref):
        t = ref[0].astype(F32)
        y = t * lax.rsqrt(jnp.mean(t * t, axis=0, keepdims=True) + EPS) * g_ref[...]
        return jnp.dot(y.T.astype(BF16), w_ref[...], preferred_element_type=F32)

    acc = acc + cm_part(f_ref, gf_ref, wf_ref) + cm_part(hy_ref, gh_ref, wh_ref)
    o_ref[0] = x_ref[0] + gate_ref[0] * acc


def _merge_out(a, ft, ht, x, gate, g, w_all, layer, tm):
    b, l, d = x.shape
    wa_n, wf_n, wh_n = a.shape[2], ft.shape[1], ht.shape[1]
    assert wa_n % wf_n == 0 and wf_n == wh_n
    tm = min(tm, l)
    ga = g[:wa_n].reshape(1, wa_n)
    gf = g[wa_n:wa_n + wf_n].reshape(wf_n, 1)
    gh = g[wa_n + wf_n:].reshape(wh_n, 1)

    def const(arr):
        return pl.BlockSpec(arr.shape, lambda bi, i: (0,) * arr.ndim)

    def w_rows(n, blk):
        return pl.BlockSpec((None, n, d), lambda bi, i: (layer, blk, 0))

    return pl.pallas_call(
        _merge_kernel,
        out_shape=jax.ShapeDtypeStruct((b, l, d), F32),
        grid=(b, l // tm),
        in_specs=[pl.BlockSpec((1, tm, wa_n), lambda bi, i: (bi, i, 0)),
                  pl.BlockSpec((1, wf_n, tm), lambda bi, i: (bi, 0, i)),
                  pl.BlockSpec((1, wh_n, tm), lambda bi, i: (bi, 0, i)),
                  pl.BlockSpec((1, tm, d), lambda bi, i: (bi, i, 0)),
                  pl.BlockSpec((1, 1, d), lambda bi, i: (bi, 0, 0)),
                  const(ga), const(gf), const(gh),
                  w_rows(wa_n, 0), w_rows(wf_n, wa_n // wf_n), w_rows(wh_n, wa_n // wf_n + 1)],
        out_specs=pl.BlockSpec((1, tm, d), lambda bi, i: (bi, i, 0)),
        compiler_params=_cparams(("parallel", "parallel")),
        name="merge_out",
    )(a, ft, ht, x, gate.reshape(b, 1, d), ga, gf, gh, w_all, w_all, w_all)


HALO = 16


def _ffn_kernel(xp_ref, x_ref, xn_ref, g_ref, sh_ref, sc_ref, gate_ref, wg_ref, wu_ref, cw_ref, cb_ref, wd_ref,
                fg_ref, o_ref, h_ref, *, tm, final_norm):
    i = pl.program_id(1)
    c = pl.program_id(2)
    last_tile = pl.num_programs(1) - 1

    n_ext = tm + 2 * HALO

    def down_partial(ge, up):
        row = lax.broadcasted_iota(jnp.int32, ge.shape, 0)
        outside = ((row < HALO) & (i == 0)) | ((row >= HALO + tm) & (i == last_tile))
        ge = jnp.where(outside, 0.0, ge)
        gp = pltpu.roll(ge, 1, 0)[HALO:HALO + tm]
        gn = pltpu.roll(ge, n_ext - 1, 0)[HALO:HALO + tm]
        cw = cw_ref[...]
        conv = gp * cw[0:1] + ge[HALO:HALO + tm] * cw[1:2] + gn * cw[2:3] + cb_ref[...]
        inner = 0.7978845608028654 * (conv + 0.044715 * (conv * conv * conv))
        act = 0.5 * conv * (1.0 + jnp.tanh(inner)) * up
        return jnp.dot(act.astype(BF16), wd_ref[...], preferred_element_type=F32)

    @pl.when(c == 0)
    def _():
        g, sh, sc = g_ref[...], sh_ref[0], sc_ref[0]
        wg, wu = wg_ref[...], wu_ref[...]
        rs = tm // NORM_SLABS
        bounds = [(0, HALO)] + [(HALO + s * rs, HALO + (s + 1) * rs) for s in range(NORM_SLABS)] + [(HALO + tm, n_ext)]
        ge_parts, up_parts = [], []
        for lo, hi in bounds:
            if lo == 0:
                xs = xp_ref[0, 0]
            elif hi == n_ext:
                xs = xn_ref[0, 0]
            else:
                xs = x_ref[0, lo - HALO:hi - HALO, :]
            h = _norm_mod(xs, g, sh, sc).astype(BF16)
            h_ref[lo:hi, :] = h
            ge_parts.append(jnp.dot(h, wg, preferred_element_type=F32))
            if lo != 0 and hi != n_ext:
                up_parts.append(jnp.dot(h, wu, preferred_element_type=F32))
        o_ref[0] = down_partial(jnp.concatenate(ge_parts, axis=0), jnp.concatenate(up_parts, axis=0))

    @pl.when(c > 0)
    def _():
        h = h_ref[...]
        ge = jnp.dot(h, wg_ref[...], preferred_element_type=F32)
        up = jnp.dot(h[HALO:HALO + tm], wu_ref[...], preferred_element_type=F32)
        o_ref[0] += down_partial(ge, up)

    @pl.when(c == pl.num_programs(2) - 1)
    def _():
        y = x_ref[0] + gate_ref[0] * o_ref[0]
        if final_norm:
            y = y * lax.rsqrt(jnp.mean(y * y, axis=-1, keepdims=True) + EPS) * fg_ref[...]
        o_ref[0] = y


def _ffn(x, g, sh, sc, gate, w_up, conv_w, conv_b, w_down, layer, final_g, tm, tf, *, final_norm):
    b, l, d = x.shape
    dff = w_down.shape[1]
    tm = min(tm, l)
    nch = dff // tf
    x4 = x.reshape(b, l // HALO, HALO, d)
    per_tile = tm // HALO
    nhalo = l // HALO
    vec = pl.BlockSpec((1, 1, d), lambda bi, i, c: (bi, 0, 0))
    return pl.pallas_call(
        functools.partial(_ffn_kernel, tm=tm, final_norm=final_norm),
        out_shape=jax.ShapeDtypeStruct((b, l, d), F32),
        grid=(b, l // tm, nch),
        in_specs=[
            pl.BlockSpec((1, 1, HALO, d), lambda bi, i, c: (bi, jnp.maximum(i * per_tile - 1, 0), 0, 0)),
            pl.BlockSpec((1, tm, d), lambda bi, i, c: (bi, i, 0)),
            pl.BlockSpec((1, 1, HALO, d), lambda bi, i, c: (bi, jnp.minimum((i + 1) * per_tile, nhalo - 1), 0, 0)),
            pl.BlockSpec((1, d), lambda bi, i, c: (0, 0)),
            vec, vec, vec,
            pl.BlockSpec((None, d, tf), lambda bi, i, c: (layer, 0, c)),
            pl.BlockSpec((None, d, tf), lambda bi, i, c: (layer, 0, nch + c)),
            pl.BlockSpec((3, tf), lambda bi, i, c: (0, c)),
            pl.BlockSpec((1, tf), lambda bi, i, c: (0, c)),
            pl.BlockSpec((None, tf, d), lambda bi, i, c: (layer, c, 0)),
            pl.BlockSpec((1, d), lambda bi, i, c: (0, 0)),
        ],
        out_specs=pl.BlockSpec((1, tm, d), lambda bi, i, c: (bi, i, 0)),
        scratch_shapes=[pltpu.VMEM((tm + 2 * HALO, d), BF16)],
        compiler_params=_cparams(("parallel", "parallel", "arbitrary")),
        name="ffn",
    )(x4, x, x4, g.reshape(1, d), sh.reshape(b, 1, d), sc.reshape(b, 1, d), gate.reshape(b, 1, d),
      w_up, w_up, conv_w, conv_b.reshape(1, dff), w_down, final_g.reshape(1, d))


TOKEN_TILE = 512
PROJ_TOKEN_TILE = 1024
PROJ_COL_TILE = 512
FFN_TOKEN_TILE = 1024
FFN_COL_TILE = 512
MIX_CHANNELS = 8
FOURIER_CHANNELS = 32


def kernel(x, c, ctx, c_ctx, ada_w, ada_b, norm1_g, norm2_g, w_in, na_rpb, hy_conv_w, hy_conv_b, hy_w1, hy_b1,
           hy_w2, hy_b2, hy_w3, hy_freq, hy_bias, mix_norm_g, w_out, ffn_w_up, ffn_conv_w, ffn_conv_b, ffn_w_down,
           final_norm_g):
    b, l, d = x.shape
    lc = ctx.shape[1]
    depth = ada_w.shape[0]
    na_w, fn_w = d // 2, d // 4
    hy_w = d - na_w - fn_w
    n_heads = na_w // HEAD_DIM
    qkv_w = 3 * na_w
    rows = l // GRID_W
    assert b + 1 <= 8 and l % (NA_ROWS_PER_BLOCK * GRID_W) == 0 and rows >= 2 * NA_ROWS_PER_BLOCK
    r = l // LANES
    nc = MIX_CHANNELS
    conv_tabs = _conv_tables(r, nc)

    s_in = jnp.concatenate([c, c_ctx[None], jnp.zeros((8 - b - 1, d), F32)], axis=0)
    mod_all = _ada_mod(s_in, ada_w, ada_b)
    xc = ctx
    w_out_bf, w_up_bf, w_down_bf = w_out.astype(BF16), ffn_w_up.astype(BF16), ffn_w_down.astype(BF16)
    for layer in range(depth):
        update_ctx = layer < depth - 1
        mod = mod_all[layer]
        sh1, sc1, g1, sh2, sc2, g2 = jnp.split(mod[:b], N_MOD, axis=-1)
        csh1, csc1, cg1, csh2, csc2, cg2 = jnp.split(jnp.broadcast_to(mod[b:b + 1], (b, N_MOD * d)), N_MOD, axis=-1)

        wl = w_in[layer]
        w_fold = _fold_fourier_weights(wl[:, qkv_w:qkv_w + fn_w])
        wt = jnp.concatenate([(wl[:, :na_w] * (HEAD_DIM ** -0.5 * LOG2E)).T, wl[:, na_w:qkv_w].T, w_fold[0].T, w_fold[1].T,
                              wl[:, qkv_w + fn_w:].T], axis=0).astype(BF16)
        hy0 = 2 * fn_w

        qkv, pt = _proj(x, norm1_g[layer], sh1, sc1, wt, qkv_w, tm=PROJ_TOKEN_TILE, tn=PROJ_COL_TILE)
        qkv_c, ptc = _proj(xc, norm1_g[layer], csh1, csc1, wt, qkv_w, tm=PROJ_TOKEN_TILE, tn=PROJ_COL_TILE)

        a = _na_attention(qkv, qkv_c, _na_bias_table(na_rpb[layer], rows), n_heads)
        yf = _fourier_latent(pt, fn_w, FOURIER_CHANNELS)
        filt = (hy_w1[layer], hy_b1[layer], hy_w2[layer], hy_b2[layer], hy_w3[layer], hy_freq[layer])
        yh = _hyena_latent(pt, hy0, hy_w, _hyena_filters(l, *filt, hy_w), hy_conv_w[layer], hy_conv_b[layer],
                           hy_bias[layer], conv_tabs, nc)
        x_new = _merge_out(a, yf, yh, x, g1, mix_norm_g[layer], w_out_bf, layer, TOKEN_TILE)

        if update_ctx:
            ac = _ctx_attention(qkv_c, n_heads)
            yfc = _ctx_fourier(ptc, fn_w)
            yhc = _ctx_hyena(ptc, hy0, hy_w, _hyena_filters(lc, *filt, hy_w), hy_conv_w[layer], hy_conv_b[layer],
                             hy_bias[layer])
            xc = _merge_out(ac, yfc, yhc, xc, cg1, mix_norm_g[layer], w_out_bf, layer, TOKEN_TILE)
        x = x_new

        x = _ffn(x, norm2_g[layer], sh2, sc2, g2, w_up_bf, ffn_conv_w[layer], ffn_conv_b[layer], w_down_bf, layer,
                 final_norm_g, FFN_TOKEN_TILE, FFN_COL_TILE, final_norm=not update_ctx)
        if update_ctx:
            xc = _ffn(xc, norm2_g[layer], csh2, csc2, cg2, w_up_bf, ffn_conv_w[layer], ffn_conv_b[layer], w_down_bf,
                      layer, final_norm_g, FFN_TOKEN_TILE, FFN_COL_TILE, final_norm=False)
    return x
```

```python
import functools
import math

import numpy as np
import jax
import jax.numpy as jnp
from jax import lax
from jax.experimental import pallas as pl
from jax.experimental.pallas import tpu as pltpu

F32 = jnp.float32
BF16 = jnp.bfloat16

EPS = 1e-6
HEAD_DIM = 128
GRID_W = 64
NA_KH = 8
NA_KW = 16
NA_ROWS_PER_BLOCK = 8
FN_GROUP_DIM = 128
HY_ORDER = 2
HY_BANDS = 16
HY_FAST_DECAY = 0.3
HY_SLOW_DECAY = 1.5
HY_TARGET = 1e-2
HY_MIN_DECAY = math.log(HY_TARGET) / HY_SLOW_DECAY
HY_MAX_DECAY = math.log(HY_TARGET) / HY_FAST_DECAY
N_MOD = 6
LANES = 128
NEG_INF = -1e30
LOG2E = math.log2(math.e)
VMEM_LIMIT = 60 * 1024 * 1024

NT_DIMS = (((1,), (1,)), ((), ()))


def _cparams(sem):
    return pltpu.CompilerParams(dimension_semantics=sem, vmem_limit_bytes=VMEM_LIMIT)


def _hi_lo(a):
    hi = a.astype(BF16)
    lo = (a - hi.astype(F32)).astype(BF16)
    return hi, lo


def _np_hi_lo(m):
    m = np.asarray(m, np.float32)
    hi = m.astype(BF16)
    lo = (m - hi.astype(np.float32)).astype(BF16)
    return hi, lo


def _const_rhs3(m):
    hi, lo = _np_hi_lo(m)
    return jnp.asarray(np.concatenate([hi, hi, lo], axis=0))


def _const_lhs3(m):
    hi, lo = _np_hi_lo(m)
    return jnp.asarray(np.concatenate([hi, hi, lo], axis=1))


def _dot_data_const(a, c3):
    hi, lo = _hi_lo(a)
    return jnp.dot(jnp.concatenate([hi, lo, hi], axis=1), c3, preferred_element_type=F32)


def _dot_const_data(c3, b):
    hi, lo = _hi_lo(b)
    return jnp.dot(c3, jnp.concatenate([hi, lo, hi], axis=0), preferred_element_type=F32)


def _dot3(a, b):
    ah, al = _hi_lo(a)
    bh, bl = _hi_lo(b)
    return (jnp.dot(ah, bh, preferred_element_type=F32)
            + jnp.dot(al, bh, preferred_element_type=F32)
            + jnp.dot(ah, bl, preferred_element_type=F32))


def _dot3k(a, b):
    ah, al = _hi_lo(a)
    bh, bl = _hi_lo(b)
    return jnp.dot(jnp.concatenate([ah, al, ah], axis=1), jnp.concatenate([bh, bh, bl], axis=0),
                   preferred_element_type=F32)


def _const_bf16(m):
    return jnp.asarray(np.asarray(m, np.float32).astype(BF16))


def _dot1(a, b):
    return jnp.dot(a.astype(BF16), b.astype(BF16), preferred_element_type=F32)


def _ada_kernel(s_ref, w_ref, b_ref, o_ref):
    s = s_ref[...]
    s = s / (1.0 + jnp.exp(-s))
    o_ref[0] = _dot1(s, w_ref[0]) + b_ref[0]


def _ada_mod(s_in, ada_w, ada_b):
    depth, d, n = ada_w.shape
    tn = 1024
    return pl.pallas_call(
        _ada_kernel,
        out_shape=jax.ShapeDtypeStruct((depth, 8, n), F32),
        grid=(depth, n // tn),
        in_specs=[
            pl.BlockSpec((8, d), lambda l, j: (0, 0)),
            pl.BlockSpec((1, d, tn), lambda l, j: (l, 0, j)),
            pl.BlockSpec((1, 1, tn), lambda l, j: (l, 0, j)),
        ],
        out_specs=pl.BlockSpec((1, 8, tn), lambda l, j: (l, 0, j)),
        compiler_params=_cparams(("parallel", "parallel")),
        name="ada_mod",
    )(s_in, ada_w, ada_b.reshape(depth, 1, n))


def _norm_mod(x, g, sh, sc):
    ms = jnp.mean(x * x, axis=-1, keepdims=True)
    return (x * lax.rsqrt(ms + EPS) * g) * (1.0 + sc) + sh


NORM_SLABS = 4


def _proj_kernel(*refs, n_tm):
    x_refs = refs[:NORM_SLABS]
    g_ref, sh_ref, sc_ref, wa_ref, wb_ref, otm_ref, ocm_ref, h_ref = refs[NORM_SLABS:]
    j = pl.program_id(2)
    rs = x_refs[0].shape[1]

    @pl.when(j == 0)
    def _():
        g, sh, sc = g_ref[...], sh_ref[0], sc_ref[0]
        w = wa_ref[...]
        for s in range(NORM_SLABS):
            h = _norm_mod(x_refs[s][0], g, sh, sc).astype(BF16)
            h_ref[s * rs:(s + 1) * rs, :] = h
            otm_ref[0, s * rs:(s + 1) * rs, :] = lax.dot_general(
                h, w, NT_DIMS, preferred_element_type=F32).astype(otm_ref.dtype)

    @pl.when((j > 0) & (j < n_tm))
    def _():
        otm_ref[0] = lax.dot_general(h_ref[...], wa_ref[...], NT_DIMS,
                                     preferred_element_type=F32).astype(otm_ref.dtype)

    @pl.when(j >= n_tm)
    def _():
        y = lax.dot_general(h_ref[...], wb_ref[...], NT_DIMS, preferred_element_type=F32)
        ocm_ref[0] = y.T.astype(ocm_ref.dtype)


def _proj(x, g, sh, sc, wt_tok, wt_cm, *, tm, tn_tok, tn_cm):
    b, l, d = x.shape
    n_tok, n_ch = wt_tok.shape[0], wt_cm.shape[0]
    tm = min(tm, l)
    n_tm, n_cm = n_tok // tn_tok, n_ch // tn_cm
    assert n_tm * tn_tok == n_tok and n_cm * tn_cm == n_ch and tm % (16 * NORM_SLABS) == 0
    vec = pl.BlockSpec((1, 1, d), lambda bi, i, j: (bi, 0, 0))
    rs = tm // NORM_SLABS
    nt, nj = l // tm, n_tm + n_cm
    n_slabs = b * l // rs
    assert nj > NORM_SLABS

    def slab(s):
        def index(bi, i, j):
            tile = bi * nt + i + (j >= nj - s).astype(jnp.int32)
            return (jnp.minimum(tile * NORM_SLABS + s, n_slabs - 1), 0, 0)
        return pl.BlockSpec((1, rs, d), index)

    tok_j = lambda j: jnp.minimum(j, n_tm - 1)
    cm_j = lambda j: jnp.maximum(j - n_tm, 0)
    return pl.pallas_call(
        functools.partial(_proj_kernel, n_tm=n_tm),
        out_shape=(jax.ShapeDtypeStruct((b, l, n_tok), BF16), jax.ShapeDtypeStruct((b, n_ch, l), BF16)),
        grid=(b, nt, nj),
        in_specs=[slab(s) for s in range(NORM_SLABS)] + [
            pl.BlockSpec((1, d), lambda bi, i, j: (0, 0)),
            vec, vec,
            pl.BlockSpec((tn_tok, d), lambda bi, i, j: (tok_j(j), 0)),
            pl.BlockSpec((tn_cm, d), lambda bi, i, j: (cm_j(j), 0)),
        ],
        out_specs=(pl.BlockSpec((1, tm, tn_tok), lambda bi, i, j: (bi, i, tok_j(j))),
                   pl.BlockSpec((1, tn_cm, tm), lambda bi, i, j: (bi, cm_j(j), i))),
        scratch_shapes=[pltpu.VMEM((tm, d), BF16)],
        compiler_params=_cparams(("parallel", "parallel", "arbitrary")),
        name="proj_in",
    )(*([x.reshape(n_slabs, rs, d)] * NORM_SLABS), g.reshape(1, d), sh.reshape(b, 1, d), sc.reshape(b, 1, d),
      wt_tok, wt_cm)


def _fold_kernel(w_ref, m_ref, o_ref):
    o_ref[0] = _dot_data_const(w_ref[...], m_ref[...])


def _fold_fourier_weights(w_f):
    d, width = w_f.shape
    gd = FN_GROUP_DIM
    idx = np.arange(gd)
    ang = 2.0 * np.pi * np.outer(idx, idx) / gd
    mat = np.concatenate([np.cos(ang), -np.sin(ang)], axis=1) / math.sqrt(gd)
    groups = width // gd
    return pl.pallas_call(
        _fold_kernel,
        out_shape=jax.ShapeDtypeStruct((2, d, width), F32),
        grid=(groups, 2),
        in_specs=[
            pl.BlockSpec((d, gd), lambda g, p: (0, g)),
            pl.BlockSpec((3 * gd, gd), lambda g, p: (0, p)),
        ],
        out_specs=pl.BlockSpec((1, d, gd), lambda g, p: (p, 0, g)),
        compiler_params=_cparams(("parallel", "parallel")),
        name="fold_fourier",
    )(w_f, _const_rhs3(mat))


def _na_bias_table(rpb, rows):
    h = rpb.shape[0]
    w, kw, kh, rb = GRID_W, NA_KW, NA_KH, NA_ROWS_PER_BLOCK
    col = np.arange(w)
    cs = np.clip(col - kw // 2, 0, w - kw)
    kc = np.arange(w)[None, :]
    col_ok = (kc >= cs[:, None]) & (kc < cs[:, None] + kw)
    padded = jnp.pad((rpb * LOG2E).astype(BF16), ((0, 0), (0, 0), (w - kw, w - kw)))
    t1 = jnp.stack([padded[:, :, w - 1 - q:2 * w - 1 - q] for q in range(w)], axis=1)
    t1 = jnp.where(col_ok[None, :, None, :], t1, NEG_INF)
    slots = 2 * rb
    blocks = []
    for t, row0 in enumerate((0, rb, rows - rb)):
        per_row = []
        for qr in range(rb):
            r = row0 + qr
            r0 = r - kh // 2 if t == 1 else min(max(r - kh // 2, 0), rows - kh)
            valid = [kr for kr in range(slots)
                     if r0 <= row0 - rb // 2 + kr < r0 + kh and (t == 1 or 0 <= row0 - rb // 2 + kr < rows)]
            lo, hi = valid[0], valid[-1] + 1
            d0 = (row0 - rb // 2 + lo) - r + (kh - 1)
            live = [t1[:, :, d0 + k, :] for k in range(hi - lo)]
            dead = jnp.full((h, w, w), NEG_INF, t1.dtype)
            per_row.append(jnp.concatenate([dead] * lo + live + [dead] * (slots - hi), axis=-1))
        blocks.append(jnp.stack(per_row, axis=1))
    return jnp.stack(blocks, axis=0).reshape(3, h, rb * w, slots * w)


NA_HEADS_PER_STEP = 4


def _na_kernel(q_ref, kp_ref, kc_ref, kn_ref, vp_ref, vc_ref, vn_ref, kx_ref, vx_ref, bias_ref, o_ref):
    k_all = jnp.concatenate([kp_ref[0], kc_ref[0], kn_ref[0]], axis=0)
    v_all = jnp.concatenate([vp_ref[0], vc_ref[0], vn_ref[0]], axis=0)
    hq = q_ref.shape[1] // 2
    band = (NA_ROWS_PER_BLOCK // 2 + NA_KH) * GRID_W
    for h in range(NA_HEADS_PER_STEP):
        sl = slice(h * HEAD_DIM, (h + 1) * HEAD_DIM)
        kx, vx = kx_ref[0, :, sl], vx_ref[0, :, sl]
        for part in range(2):
            rows = slice(part * hq, (part + 1) * hq)
            keys = slice(part * hq, part * hq + band)
            q = q_ref[0, rows, sl]
            s = (lax.dot_general(q, k_all[keys, sl], NT_DIMS, preferred_element_type=F32)
                 + bias_ref[0, h, rows, keys].astype(F32))
            sx = lax.dot_general(q, kx, NT_DIMS, preferred_element_type=F32)
            m = jnp.maximum(jnp.max(s, axis=-1, keepdims=True), jnp.max(sx, axis=-1, keepdims=True))
            p = jnp.exp2(s - m)
            px = jnp.exp2(sx - m)
            den = jnp.sum(p, axis=-1, keepdims=True) + jnp.sum(px, axis=-1, keepdims=True)
            o = (jnp.dot(p.astype(BF16), v_all[keys, sl], preferred_element_type=F32)
                 + jnp.dot(px.astype(BF16), vx, preferred_element_type=F32))
            o_ref[0, rows, sl] = (o / den).astype(o_ref.dtype)


def _na_attention(qkv, qkv_c, bias_tab, n_heads):
    b, l, _ = qkv.shape
    lc = qkv_c.shape[1]
    hps = NA_HEADS_PER_STEP
    wd = hps * HEAD_DIM
    tq = NA_ROWS_PER_BLOCK * GRID_W
    th = tq // 2
    nb = l // tq
    nh = l // th
    ng = n_heads // hps
    assert nb >= 2 and l % tq == 0 and n_heads % hps == 0

    def btype(i):
        return jnp.where(i == 0, 0, jnp.where(i == nb - 1, 2, 1))

    def cur(off):
        return pl.BlockSpec((1, tq, wd), lambda h, bi, i: (bi, i, off + h))

    def prev(off):
        return pl.BlockSpec((1, th, wd), lambda h, bi, i: (bi, jnp.maximum(2 * i - 1, 0), off + h))

    def nxt(off):
        return pl.BlockSpec((1, th, wd), lambda h, bi, i: (bi, jnp.minimum(2 * i + 2, nh - 1), off + h))

    def ctx(off):
        return pl.BlockSpec((1, lc, wd), lambda h, bi, i: (bi, 0, off + h))

    ko, vo = ng, 2 * ng
    return pl.pallas_call(
        _na_kernel,
        out_shape=jax.ShapeDtypeStruct((b, l, n_heads * HEAD_DIM), BF16),
        grid=(ng, b, nb),
        in_specs=[cur(0), prev(ko), cur(ko), nxt(ko), prev(vo), cur(vo), nxt(vo), ctx(ko), ctx(vo),
                  pl.BlockSpec((1, hps, tq, 2 * tq), lambda h, bi, i: (btype(i), h, 0, 0))],
        out_specs=pl.BlockSpec((1, tq, wd), lambda h, bi, i: (bi, i, h)),
        compiler_params=_cparams(("parallel", "parallel", "arbitrary")),
        name="na_attention",
    )(qkv, qkv, qkv, qkv, qkv, qkv, qkv, qkv_c, qkv_c, bias_tab)


def _ctx_attn_kernel(q_ref, k_ref, v_ref, o_ref):
    s = lax.dot_general(q_ref[0], k_ref[0], NT_DIMS, preferred_element_type=F32)
    m = jnp.max(s, axis=-1, keepdims=True)
    p = jnp.exp2(s - m)
    den = jnp.sum(p, axis=-1, keepdims=True)
    o_ref[0] = (jnp.dot(p.astype(BF16), v_ref[0], preferred_element_type=F32) / den).astype(o_ref.dtype)


def _ctx_attention(qkv_c, n_heads):
    b, lc, _ = qkv_c.shape
    hd = HEAD_DIM

    def spec(off):
        return pl.BlockSpec((1, lc, hd), lambda bi, h: (bi, 0, off + h))

    return pl.pallas_call(
        _ctx_attn_kernel,
        out_shape=jax.ShapeDtypeStruct((b, lc, n_heads * hd), F32),
        grid=(b, n_heads),
        in_specs=[spec(0), spec(n_heads), spec(2 * n_heads)],
        out_specs=spec(0),
        compiler_params=_cparams(("parallel", "parallel")),
        name="ctx_attention",
    )(qkv_c, qkv_c, qkv_c)


def _cos_sin(n_out, n_in, period):
    ang = 2.0 * np.pi * ((np.arange(n_out)[:, None] * np.arange(n_in)[None, :]) % period) / period
    return np.cos(ang), np.sin(ang)


def _lane_cat(x3):
    return jnp.concatenate([x3[j] for j in range(x3.shape[0])], axis=1)


def _rows_from_lanes(re, im, nc):
    return jnp.concatenate(
        [jnp.concatenate([re[:, j * LANES:(j + 1) * LANES], im[:, j * LANES:(j + 1) * LANES]], axis=1)
         for j in range(nc)], axis=0)


def _lanes_from_rows(x, nc, r):
    re = jnp.concatenate([x[j * r:(j + 1) * r, :LANES] for j in range(nc)], axis=1)
    im = jnp.concatenate([x[j * r:(j + 1) * r, LANES:] for j in range(nc)], axis=1)
    return jnp.concatenate([re, im], axis=0)


def _fourier_tables(r, nc):
    l = r * LANES
    c1, s1 = _cos_sin(r, r, r)
    m1 = np.block([[c1, s1], [-s1, c1]])
    tc, ts = _cos_sin(r, LANES, l)
    c2, s2 = _cos_sin(LANES, LANES, LANES)
    m2 = np.concatenate([c2, s2], axis=0) / math.sqrt(l)
    return (_const_bf16(m1), jnp.asarray(np.tile(tc, (1, nc)), F32), jnp.asarray(np.tile(ts, (1, nc)), F32),
            _const_bf16(m2))


def _fourier_kernel(zr_ref, zi_ref, m1_ref, tc_ref, ts_ref, m2_ref, o_ref, *, nc, r):
    z = jnp.concatenate([_lane_cat(zr_ref[0]), _lane_cat(zi_ref[0])], axis=0)
    a = _dot1(m1_ref[...], z)
    ar, ai = a[:r], a[r:]
    tc, ts = tc_ref[...], ts_ref[...]
    br = (ar * tc + ai * ts).astype(BF16)
    bi = (ai * tc - ar * ts).astype(BF16)
    y = _dot1(_rows_from_lanes(br, bi, nc), m2_ref[...])
    for j in range(nc):
        o_ref[0, j] = y[j * r:(j + 1) * r].T.astype(o_ref.dtype)


def _fourier_latent(pt, n_ch, nc):
    b, _, l = pt.shape
    r = l // LANES
    p4 = pt.reshape(b, pt.shape[1], r, LANES)
    m1, tc, ts, m2 = _fourier_tables(r, nc)
    nblk = n_ch // nc

    def const(a):
        return pl.BlockSpec(a.shape, lambda bi, c: (0,) * a.ndim)

    out = pl.pallas_call(
        functools.partial(_fourier_kernel, nc=nc, r=r),
        out_shape=jax.ShapeDtypeStruct((b, n_ch, LANES, r), BF16),
        grid=(b, nblk),
        in_specs=[
            pl.BlockSpec((1, nc, r, LANES), lambda bi, c: (bi, c, 0, 0)),
            pl.BlockSpec((1, nc, r, LANES), lambda bi, c: (bi, nblk + c, 0, 0)),
            const(m1), const(tc), const(ts), const(m2),
        ],
        out_specs=pl.BlockSpec((1, nc, LANES, r), lambda bi, c: (bi, c, 0, 0)),
        compiler_params=_cparams(("parallel", "parallel")),
        name="fourier_latent",
    )(p4, p4, m1, tc, ts, m2)
    return out.reshape(b, n_ch, l)


def _filter_kernel(bands_ref, w1t_ref, w1c_ref, w1s_ref, b1_ref, w2_ref, b2_ref, w3_ref, fr_ref, dl_ref,
                   o_ref, *, l, tl):
    i = pl.program_id(0)
    n = (i * tl + lax.broadcasted_iota(jnp.int32, (1, tl), 1))
    pos_i = jnp.where(n < l, n, 2 * l - n)
    pos = pos_i.astype(F32)
    t = pos / float(max(l - 1, 1))
    ang = bands_ref[...] * (2.0 * math.pi / l) * pos
    fr = fr_ref[...]
    cos_a, sin_a = jnp.cos(ang), jnp.sin(ang)
    w1c, w1s = w1c_ref[...], w1s_ref[...]
    pre = w1t_ref[...] * t
    for k in range(HY_BANDS):
        pre = pre + w1c[:, k:k + 1] * cos_a[k:k + 1, :] - w1s[:, k:k + 1] * sin_a[k:k + 1, :]
    kpad = jnp.zeros((w2_ref.shape[1] - w2_ref.shape[0], tl), F32)
    h1 = jnp.sin(fr * (pre + b1_ref[...]))
    h2 = jnp.sin(fr * (_dot3k(w2_ref[...], jnp.concatenate([h1, kpad], axis=0)) + b2_ref[...]))
    out = _dot3k(w3_ref[0], jnp.concatenate([h2, kpad], axis=0)) * jnp.exp(-t * dl_ref[...])
    o_ref[...] = jnp.where(n == l, 0.0, out).astype(o_ref.dtype)


def _hyena_filters(l, w1, b1, w2, b2, w3, freq, width):
    hid = w1.shape[1]
    kdim = -(-hid // LANES) * LANES
    pad = kdim - hid
    tl = min(2048, l)
    rows = HY_ORDER * width
    bands = np.linspace(1e-4, HY_BANDS - 1, HY_BANDS, dtype=np.float32).reshape(HY_BANDS, 1)
    deltas = np.abs(np.linspace(HY_MIN_DECAY, HY_MAX_DECAY, width, dtype=np.float32))
    dl = np.tile(deltas, HY_ORDER).reshape(rows, 1)
    w1t = w1.T
    w2t = jnp.pad(w2.T, ((0, 0), (0, pad)))
    w3d = w3.reshape(hid, HY_ORDER, 2, width).transpose(2, 1, 3, 0).reshape(2, rows, hid)
    w3d = jnp.pad(w3d, ((0, 0), (0, 0), (0, pad)))
    col = lambda v: v.reshape(hid, 1)

    def const(shape):
        return pl.BlockSpec(shape, lambda i: (0,) * len(shape))

    return pl.pallas_call(
        functools.partial(_filter_kernel, l=l, tl=tl),
        out_shape=jax.ShapeDtypeStruct((rows, 2 * l), BF16),
        grid=(2 * l // tl,),
        in_specs=[const((HY_BANDS, 1)), const((hid, 1)), const((hid, HY_BANDS)), const((hid, HY_BANDS)),
                  const((hid, 1)), const((hid, kdim)), const((hid, 1)),
                  pl.BlockSpec((1, rows, kdim), lambda i: (i // (l // tl), 0, 0)),
                  const((hid, 1)), const((rows, 1))],
        out_specs=pl.BlockSpec((rows, tl), lambda i: (0, i)),
        compiler_params=_cparams(("parallel",)),
        name="hyena_filters",
    )(jnp.asarray(bands), w1t[:, 0:1], w1t[:, 1:1 + HY_BANDS], w1t[:, 1 + HY_BANDS:], col(b1), w2t, col(b2),
      w3d, col(freq), jnp.asarray(dl))


def _conv_tables(r, nc):
    r2 = 2 * r
    n = r2 * LANES
    c1, s1 = _cos_sin(r2, r2, r2)
    c1h, s1h = c1[:, :r], s1[:, :r]
    m1 = np.block([[c1h, s1h], [-s1h, c1h]])
    m1_real = np.concatenate([c1, -s1], axis=0)
    tc, ts = _cos_sin(r2, LANES, n)
    c2, s2 = _cos_sin(LANES, LANES, LANES)
    m2 = np.block([[c2, -s2], [s2, c2]])
    m2i = np.block([[c2, s2], [-s2, c2]])
    ct, st = c1h.T, s1h.T
    m1i = np.block([[ct, -st], [st, ct]]) / n
    f32 = lambda a: jnp.asarray(a, F32)
    return dict(
        m1=_const_bf16(m1), m1_real=_const_bf16(m1_real), m2=_const_bf16(m2), m2i=_const_bf16(m2i),
        m1i=_const_bf16(m1i),
        tc_l=f32(np.tile(tc, (1, nc))), ts_l=f32(np.tile(ts, (1, nc))),
        tc_r=f32(np.tile(tc, (nc, 1))), ts_r=f32(np.tile(ts, (nc, 1))))


def _kernel_spectrum(k3, m1_real, tc_l, ts_l, m2, nc, r2):
    a = _dot1(m1_real, _lane_cat(k3))
    ar, ai = a[:r2], a[r2:]
    br = (ar * tc_l + ai * ts_l).astype(BF16)
    bi = (ai * tc_l - ar * ts_l).astype(BF16)
    return _dot1(_rows_from_lanes(br, bi, nc), m2)


def _shift_tokens(t, r, direction):
    w = t.shape[1]
    lane = lax.broadcasted_iota(jnp.int32, t.shape, 1) & (LANES - 1)
    row = lax.broadcasted_iota(jnp.int32, t.shape, 0)
    if direction < 0:
        near = pltpu.roll(t, 1, 1)
        wrap = pltpu.roll(pltpu.roll(t, w - (LANES - 1), 1), 1, 0)
        edge = lane == 0
        dead = edge & (row == 0)
    else:
        near = pltpu.roll(t, w - 1, 1)
        wrap = pltpu.roll(pltpu.roll(t, LANES - 1, 1), r - 1, 0)
        edge = lane == LANES - 1
        dead = edge & (row == r - 1)
    return jnp.where(dead, 0.0, jnp.where(edge, wrap, near))


def _long_conv(zr, zi, kspec, m1, m2, m2i, m1i, tc_l, ts_l, tc_r, ts_r, nc, r):
    r2 = 2 * r
    a = _dot1(m1, jnp.concatenate([zr.astype(BF16), zi.astype(BF16)], axis=0))
    ar, ai = a[:r2], a[r2:]
    br = (ar * tc_l + ai * ts_l).astype(BF16)
    bi = (ai * tc_l - ar * ts_l).astype(BF16)
    x = _dot1(_rows_from_lanes(br, bi, nc), m2)
    xr, xi = x[:, :LANES], x[:, LANES:]
    kr, ki = kspec[:, :LANES], kspec[:, LANES:]
    y = jnp.concatenate([(xr * kr - xi * ki).astype(BF16), (xr * ki + xi * kr).astype(BF16)], axis=1)
    bb = _dot1(y, m2i)
    pr, pi = bb[:, :LANES], bb[:, LANES:]
    q = jnp.concatenate([(pr * tc_r - pi * ts_r).astype(BF16), (pi * tc_r + pr * ts_r).astype(BF16)], axis=1)
    out = _dot1(m1i, _lanes_from_rows(q, nc, r2))
    return out[:r], out[r:]


def _hyena_kernel(v0, v1, a0, a1, b0, b1, k0_ref, k1_ref, cw_ref, cb_ref, bias_ref,
                  m1_ref, m2_ref, m2i_ref, m1i_ref, tcl_ref, tsl_ref, tcr_ref, tsr_ref, m1r_ref, o_ref, *, nc, r):
    tabs = (m1_ref[...], m2_ref[...], m2i_ref[...], m1i_ref[...],
            tcl_ref[...], tsl_ref[...], tcr_ref[...], tsr_ref[...])

    def spectrum(k_ref):
        return _kernel_spectrum(k_ref[...], m1r_ref[...], tcl_ref[...], tsl_ref[...], m2_ref[...], nc, 2 * r)

    def short_conv(ref, part):
        t = _lane_cat(ref[0]).astype(F32)
        w = cw_ref[part]
        return (_shift_tokens(t, r, -1) * w[0:1] + t * w[1:2] + _shift_tokens(t, r, +1) * w[2:3]
                + cb_ref[part])

    vr, vi = short_conv(v0, 0), short_conv(v1, 0)
    x1r, x1i = short_conv(a0, 1), short_conv(a1, 1)
    x2r, x2i = short_conv(b0, 2), short_conv(b1, 2)
    bias = bias_ref[...]
    yr, yi = _long_conv(vr, vi, spectrum(k0_ref), *tabs, nc, r)
    zr = x1r * (yr + vr * bias[0:1])
    zi = x1i * (yi + vi * bias[0:1])
    yr, yi = _long_conv(zr, zi, spectrum(k1_ref), *tabs, nc, r)
    outr = x2r * (yr + zr * bias[1:2])
    outi = x2i * (yi + zi * bias[1:2])
    for j in range(nc):
        o_ref[0, j] = outr[:, j * LANES:(j + 1) * LANES].astype(o_ref.dtype)
        o_ref[1, j] = outi[:, j * LANES:(j + 1) * LANES].astype(o_ref.dtype)


def _hyena_latent(pt, ch0, width, k2t, conv_w, conv_b, bias, tabs, nc):
    b, c_all, l = pt.shape
    assert b == 2, "the two batch entries ride one complex transform"
    r = l // LANES
    r2 = 2 * r
    p4 = pt.reshape(b, c_all, r, LANES)
    k3 = k2t.reshape(k2t.shape[0], r2, LANES)
    nblk = width // nc
    rep = lambda a: jnp.repeat(a, LANES, axis=-1)
    cw = rep(conv_w.reshape(3, 3, width).transpose(1, 0, 2))
    cb = rep(conv_b.reshape(3, 1, width))
    bs = rep(bias)

    def inp(bi, part):
        off = (ch0 + part * width) // nc
        return pl.BlockSpec((1, nc, r, LANES), lambda c: (bi, off + c, 0, 0))

    def const(a):
        return pl.BlockSpec(a.shape, lambda c: (0,) * a.ndim)

    names = ("m1", "m2", "m2i", "m1i", "tc_l", "ts_l", "tc_r", "ts_r", "m1_real")
    consts = [tabs[k] for k in names]
    out = pl.pallas_call(
        functools.partial(_hyena_kernel, nc=nc, r=r),
        out_shape=jax.ShapeDtypeStruct((b, width, r, LANES), BF16),
        grid=(nblk,),
        in_specs=[inp(0, 0), inp(1, 0), inp(0, 1), inp(1, 1), inp(0, 2), inp(1, 2),
                  pl.BlockSpec((nc, r2, LANES), lambda c: (c, 0, 0)),
                  pl.BlockSpec((nc, r2, LANES), lambda c: (nblk + c, 0, 0)),
                  pl.BlockSpec((3, 3, nc * LANES), lambda c: (0, 0, c)),
                  pl.BlockSpec((3, 1, nc * LANES), lambda c: (0, 0, c)),
                  pl.BlockSpec((HY_ORDER, nc * LANES), lambda c: (0, c))]
                 + [const(a) for a in consts],
        out_specs=pl.BlockSpec((b, nc, r, LANES), lambda c: (0, c, 0, 0)),
        compiler_params=_cparams(("parallel",)),
        name="hyena_latent",
    )(p4, p4, p4, p4, p4, p4, k3, k3, cw, cb, bs, *consts)
    return out.reshape(b, width, l)


def _ctx_fourier_kernel(zr_ref, zi_ref, m_ref, o_ref):
    z = jnp.concatenate([zr_ref[0], zi_ref[0]], axis=1).astype(F32)
    o_ref[0] = _dot_data_const(z, m_ref[...])


def _ctx_fourier(ptc, n_ch):
    b, _, lc = ptc.shape
    c, s = _cos_sin(lc, lc, lc)
    m = _const_rhs3(np.concatenate([c, s], axis=0) / math.sqrt(lc))
    return pl.pallas_call(
        _ctx_fourier_kernel,
        out_shape=jax.ShapeDtypeStruct((b, n_ch, lc), F32),
        grid=(b,),
        in_specs=[pl.BlockSpec((1, n_ch, lc), lambda bi: (bi, 0, 0)),
                  pl.BlockSpec((1, n_ch, lc), lambda bi: (bi, 1, 0)),
                  pl.BlockSpec(m.shape, lambda bi: (0, 0))],
        out_specs=pl.BlockSpec((1, n_ch, lc), lambda bi: (bi, 0, 0)),
        compiler_params=_cparams(("parallel",)),
        name="ctx_fourier",
    )(ptc, ptc, m)


def _ctx_hyena_kernel(v0, v1, a0, a1, b0, b1, k_ref, cw_ref, cb_ref, bias_ref, mk_ref, mf_ref, mi_ref,
                      o_ref, *, lc, width):
    def short_conv(ref, part):
        t = ref[0].astype(F32)
        lane = lax.broadcasted_iota(jnp.int32, t.shape, 1)
        prv = jnp.where(lane == 0, 0.0, pltpu.roll(t, 1, 1))
        nxt = jnp.where(lane == lc - 1, 0.0, pltpu.roll(t, lc - 1, 1))
        w = cw_ref[part]
        return prv * w[:, 0:1] + t * w[:, 1:2] + nxt * w[:, 2:3] + cb_ref[part]

    kspec = _dot_data_const(k_ref[...].astype(F32), mk_ref[...])
    n = 2 * lc

    def long_conv(zr, zi, ks):
        x = _dot_data_const(jnp.concatenate([zr, zi], axis=1), mf_ref[...])
        xr, xi = x[:, :n], x[:, n:]
        kr, ki = ks[:, :n], ks[:, n:]
        y = jnp.concatenate([xr * kr - xi * ki, xr * ki + xi * kr], axis=1)
        out = _dot_data_const(y, mi_ref[...])
        return out[:, :lc], out[:, lc:]

    vr, vi = short_conv(v0, 0), short_conv(v1, 0)
    x1r, x1i = short_conv(a0, 1), short_conv(a1, 1)
    x2r, x2i = short_conv(b0, 2), short_conv(b1, 2)
    bias = bias_ref[...]
    yr, yi = long_conv(vr, vi, kspec[:width])
    zr = x1r * (yr + vr * bias[:, 0:1])
    zi = x1i * (yi + vi * bias[:, 0:1])
    yr, yi = long_conv(zr, zi, kspec[width:])
    o_ref[0] = x2r * (yr + zr * bias[:, 1:2])
    o_ref[1] = x2i * (yi + zi * bias[:, 1:2])


def _ctx_hyena(ptc, ch0, width, k2t, conv_w, conv_b, bias):
    b, _, lc = ptc.shape
    assert b == 2
    n = 2 * lc
    c, s = _cos_sin(n, n, n)
    mk = _const_rhs3(np.concatenate([c, -s], axis=1))
    ch, sh = c[:lc], s[:lc]
    mf = _const_rhs3(np.block([[ch, -sh], [sh, ch]]))
    ci, si = c[:, :lc], s[:, :lc]
    mi = _const_rhs3(np.block([[ci, si], [-si, ci]]) / n)
    cw = conv_w.reshape(3, 3, width).transpose(1, 2, 0)
    cb = conv_b.reshape(3, width, 1)

    def inp(bi, part):
        return pl.BlockSpec((1, width, lc), lambda i: (bi, ch0 // width + part, 0))

    def const(a):
        return pl.BlockSpec(a.shape, lambda i: (0,) * a.ndim)

    args = (k2t, cw, cb, bias.T, mk, mf, mi)
    return pl.pallas_call(
        functools.partial(_ctx_hyena_kernel, lc=lc, width=width),
        out_shape=jax.ShapeDtypeStruct((b, width, lc), F32),
        grid=(1,),
        in_specs=[inp(0, 0), inp(1, 0), inp(0, 1), inp(1, 1), inp(0, 2), inp(1, 2)] + [const(a) for a in args],
        out_specs=pl.BlockSpec((b, width, lc), lambda i: (0, 0, 0)),
        compiler_params=_cparams(("arbitrary",)),
        name="ctx_hyena",
    )(ptc, ptc, ptc, ptc, ptc, ptc, *args)


def _merge_kernel(a_ref, f_ref, hy_ref, x_ref, gate_ref, ga_ref, gf_ref, gh_ref, wa_ref, wf_ref, wh_ref, o_ref):
    a = a_ref[0].astype(F32)
    ya = a * lax.rsqrt(jnp.mean(a * a, axis=-1, keepdims=True) + EPS) * ga_ref[...]
    acc = jnp.dot(ya.astype(BF16), wa_ref[...], preferred_element_type=F32)

    def cm_part(ref, g_ref, w_ref):
        t = ref[0].astype(F32)
        y = t * lax.rsqrt(jnp.mean(t * t, axis=0, keepdims=True) + EPS) * g_ref[...]
        return jnp.dot(y.T.astype(BF16), w_ref[...], preferred_element_type=F32)

    acc = acc + cm_part(f_ref, gf_ref, wf_ref) + cm_part(hy_ref, gh_ref, wh_ref)
    o_ref[0] = x_ref[0] + gate_ref[0] * acc


def _merge_out(a, ft, ht, x, gate, g, w_all, layer, tm):
    b, l, d = x.shape
    wa_n, wf_n, wh_n = a.shape[2], ft.shape[1], ht.shape[1]
    assert wa_n % wf_n == 0 and wf_n == wh_n
    tm = min(tm, l)
    ga = g[:wa_n].reshape(1, wa_n)
    gf = g[wa_n:wa_n + wf_n].reshape(wf_n, 1)
    gh = g[wa_n + wf_n:].reshape(wh_n, 1)

    def const(arr):
        return pl.BlockSpec(arr.shape, lambda bi, i: (0,) * arr.ndim)

    def w_rows(n, blk):
        return pl.BlockSpec((None, n, d), lambda bi, i: (layer, blk, 0))

    return pl.pallas_call(
        _merge_kernel,
        out_shape=jax.ShapeDtypeStruct((b, l, d), F32),
        grid=(b, l // tm),
        in_specs=[pl.BlockSpec((1, tm, wa_n), lambda bi, i: (bi, i, 0)),
                  pl.BlockSpec((1, wf_n, tm), lambda bi, i: (bi, 0, i)),
                  pl.BlockSpec((1, wh_n, tm), lambda bi, i: (bi, 0, i)),
                  pl.BlockSpec((1, tm, d), lambda bi, i: (bi, i, 0)),
                  pl.BlockSpec((1, 1, d), lambda bi, i: (bi, 0, 0)),
                  const(ga), const(gf), const(gh),
                  w_rows(wa_n, 0), w_rows(wf_n, wa_n // wf_n), w_rows(wh_n, wa_n // wf_n + 1)],
        out_specs=pl.BlockSpec((1, tm, d), lambda bi, i: (bi, i, 0)),
        compiler_params=_cparams(("parallel", "parallel")),
        name="merge_out",
    )(a, ft, ht, x, gate.reshape(b, 1, d), ga, gf, gh, w_all, w_all, w_all)


HALO = 16


def _ffn_kernel(xp_ref, x_ref, xn_ref, g_ref, sh_ref, sc_ref, gate_ref, wg_ref, wu_ref, cw_ref, cb_ref, wd_ref,
                fg_ref, o_ref, h_ref, *, tm, final_norm):
    i = pl.program_id(1)
    c = pl.program_id(2)
    last_tile = pl.num_programs(1) - 1

    n_ext = tm + 2 * HALO

    def down_partial(ge, up):
        row = lax.broadcasted_iota(jnp.int32, ge.shape, 0)
        outside = ((row < HALO) & (i == 0)) | ((row >= HALO + tm) & (i == last_tile))
        ge = jnp.where(outside, 0.0, ge)
        gp = pltpu.roll(ge, 1, 0)[HALO:HALO + tm]
        gn = pltpu.roll(ge, n_ext - 1, 0)[HALO:HALO + tm]
        cw = cw_ref[...]
        conv = gp * cw[0:1] + ge[HALO:HALO + tm] * cw[1:2] + gn * cw[2:3] + cb_ref[...]
        inner = 0.7978845608028654 * (conv + 0.044715 * (conv * conv * conv))
        act = 0.5 * conv * (1.0 + jnp.tanh(inner)) * up
        return jnp.dot(act.astype(BF16), wd_ref[...], preferred_element_type=F32)

    @pl.when(c == 0)
    def _():
        g, sh, sc = g_ref[...], sh_ref[0], sc_ref[0]
        wg, wu = wg_ref[...], wu_ref[...]
        rs = tm // NORM_SLABS
        bounds = [(0, HALO)] + [(HALO + s * rs, HALO + (s + 1) * rs) for s in range(NORM_SLABS)] + [(HALO + tm, n_ext)]
        ge_parts, up_parts = [], []
        for lo, hi in bounds:
            if lo == 0:
                xs = xp_ref[0, 0]
            elif hi == n_ext:
                xs = xn_ref[0, 0]
            else:
                xs = x_ref[0, lo - HALO:hi - HALO, :]
            h = _norm_mod(xs, g, sh, sc).astype(BF16)
            h_ref[lo:hi, :] = h
            ge_parts.append(jnp.dot(h, wg, preferred_element_type=F32))
            if lo != 0 and hi != n_ext:
                up_parts.append(jnp.dot(h, wu, preferred_element_type=F32))
        o_ref[0] = down_partial(jnp.concatenate(ge_parts, axis=0), jnp.concatenate(up_parts, axis=0))

    @pl.when(c > 0)
    def _():
        h = h_ref[...]
        ge = jnp.dot(h, wg_ref[...], preferred_element_type=F32)
        up = jnp.dot(h[HALO:HALO + tm], wu_ref[...], preferred_element_type=F32)
        o_ref[0] += down_partial(ge, up)

    @pl.when(c == pl.num_programs(2) - 1)
    def _():
        y = x_ref[0] + gate_ref[0] * o_ref[0]
        if final_norm:
            y = y * lax.rsqrt(jnp.mean(y * y, axis=-1, keepdims=True) + EPS) * fg_ref[...]
        o_ref[0] = y


def _ffn(x, g, sh, sc, gate, w_up, conv_w, conv_b, w_down, layer, final_g, tm, tf, *, final_norm):
    b, l, d = x.shape
    dff = w_down.shape[1]
    tm = min(tm, l)
    nch = dff // tf
    x4 = x.reshape(b, l // HALO, HALO, d)
    per_tile = tm // HALO
    nhalo = l // HALO
    vec = pl.BlockSpec((1, 1, d), lambda bi, i, c: (bi, 0, 0))
    return pl.pallas_call(
        functools.partial(_ffn_kernel, tm=tm, final_norm=final_norm),
        out_shape=jax.ShapeDtypeStruct((b, l, d), F32),
        grid=(b, l // tm, nch),
        in_specs=[
            pl.BlockSpec((1, 1, HALO, d), lambda bi, i, c: (bi, jnp.maximum(i * per_tile - 1, 0), 0, 0)),
            pl.BlockSpec((1, tm, d), lambda bi, i, c: (bi, i, 0)),
            pl.BlockSpec((1, 1, HALO, d), lambda bi, i, c: (bi, jnp.minimum((i + 1) * per_tile, nhalo - 1), 0, 0)),
            pl.BlockSpec((1, d), lambda bi, i, c: (0, 0)),
            vec, vec, vec,
            pl.BlockSpec((None, d, tf), lambda bi, i, c: (layer, 0, c)),
            pl.BlockSpec((None, d, tf), lambda bi, i, c: (layer, 0, nch + c)),
            pl.BlockSpec((3, tf), lambda bi, i, c: (0, c)),
            pl.BlockSpec((1, tf), lambda bi, i, c: (0, c)),
            pl.BlockSpec((None, tf, d), lambda bi, i, c: (layer, c, 0)),
            pl.BlockSpec((1, d), lambda bi, i, c: (0, 0)),
        ],
        out_specs=pl.BlockSpec((1, tm, d), lambda bi, i, c: (bi, i, 0)),
        scratch_shapes=[pltpu.VMEM((tm + 2 * HALO, d), BF16)],
        compiler_params=_cparams(("parallel", "parallel", "arbitrary")),
        name="ffn",
    )(x4, x, x4, g.reshape(1, d), sh.reshape(b, 1, d), sc.reshape(b, 1, d), gate.reshape(b, 1, d),
      w_up, w_up, conv_w, conv_b.reshape(1, dff), w_down, final_g.reshape(1, d))


TOKEN_TILE = 512
PROJ_TOKEN_TILE = 1024
PROJ_TOK_COL_TILE = 1024
PROJ_CM_COL_TILE = 512
FFN_TOKEN_TILE = 1024
FFN_COL_TILE = 512
MIX_CHANNELS = 8
FOURIER_CHANNELS = 32


def kernel(x, c, ctx, c_ctx, ada_w, ada_b, norm1_g, norm2_g, w_in, na_rpb, hy_conv_w, hy_conv_b, hy_w1, hy_b1,
           hy_w2, hy_b2, hy_w3, hy_freq, hy_bias, mix_norm_g, w_out, ffn_w_up, ffn_conv_w, ffn_conv_b, ffn_w_down,
           final_norm_g):
    b, l, d = x.shape
    lc = ctx.shape[1]
    depth = ada_w.shape[0]
    na_w, fn_w = d // 2, d // 4
    hy_w = d - na_w - fn_w
    n_heads = na_w // HEAD_DIM
    qkv_w = 3 * na_w
    rows = l // GRID_W
    assert b + 1 <= 8 and l % (NA_ROWS_PER_BLOCK * GRID_W) == 0 and rows >= 2 * NA_ROWS_PER_BLOCK
    r = l // LANES
    nc = MIX_CHANNELS
    conv_tabs = _conv_tables(r, nc)

    s_in = jnp.concatenate([c, c_ctx[None], jnp.zeros((8 - b - 1, d), F32)], axis=0)
    mod_all = _ada_mod(s_in, ada_w, ada_b)
    xc = ctx
    w_out_bf, w_up_bf, w_down_bf = w_out.astype(BF16), ffn_w_up.astype(BF16), ffn_w_down.astype(BF16)
    for layer in range(depth):
        update_ctx = layer < depth - 1
        mod = mod_all[layer]
        sh1, sc1, g1, sh2, sc2, g2 = jnp.split(mod[:b], N_MOD, axis=-1)
        csh1, csc1, cg1, csh2, csc2, cg2 = jnp.split(jnp.broadcast_to(mod[b:b + 1], (b, N_MOD * d)), N_MOD, axis=-1)

        wl = w_in[layer]
        w_fold = _fold_fourier_weights(wl[:, qkv_w:qkv_w + fn_w])
        wt_tok = jnp.concatenate([(wl[:, :na_w] * (HEAD_DIM ** -0.5 * LOG2E)).T, wl[:, na_w:qkv_w].T],
                                 axis=0).astype(BF16)
        wt_cm = jnp.concatenate([w_fold[0].T, w_fold[1].T, wl[:, qkv_w + fn_w:].T], axis=0).astype(BF16)
        hy0 = 2 * fn_w

        tiles = dict(tm=PROJ_TOKEN_TILE, tn_tok=PROJ_TOK_COL_TILE, tn_cm=PROJ_CM_COL_TILE)
        qkv, pt = _proj(x, norm1_g[layer], sh1, sc1, wt_tok, wt_cm, **tiles)
        qkv_c, ptc = _proj(xc, norm1_g[layer], csh1, csc1, wt_tok, wt_cm, **tiles)

        a = _na_attention(qkv, qkv_c, _na_bias_table(na_rpb[layer], rows), n_heads)
        yf = _fourier_latent(pt, fn_w, FOURIER_CHANNELS)
        filt = (hy_w1[layer], hy_b1[layer], hy_w2[layer], hy_b2[layer], hy_w3[layer], hy_freq[layer])
        yh = _hyena_latent(pt, hy0, hy_w, _hyena_filters(l, *filt, hy_w), hy_conv_w[layer], hy_conv_b[layer],
                           hy_bias[layer], conv_tabs, nc)
        x_new = _merge_out(a, yf, yh, x, g1, mix_norm_g[layer], w_out_bf, layer, TOKEN_TILE)

        if update_ctx:
            ac = _ctx_attention(qkv_c, n_heads)
            yfc = _ctx_fourier(ptc, fn_w)
            yhc = _ctx_hyena(ptc, hy0, hy_w, _hyena_filters(lc, *filt, hy_w), hy_conv_w[layer], hy_conv_b[layer],
                             hy_bias[layer])
            xc = _merge_out(ac, yfc, yhc, xc, cg1, mix_norm_g[layer], w_out_bf, layer, TOKEN_TILE)
        x = x_new

        x = _ffn(x, norm2_g[layer], sh2, sc2, g2, w_up_bf, ffn_conv_w[layer], ffn_conv_b[layer], w_down_bf, layer,
                 final_norm_g, FFN_TOKEN_TILE, FFN_COL_TILE, final_norm=not update_ctx)
        if update_ctx:
            xc = _ffn(xc, norm2_g[layer], csh2, csc2, cg2, w_up_bf, ffn_conv_w[layer], ffn_conv_b[layer], w_down_bf,
                      layer, final_norm_g, FFN_TOKEN_TILE, FFN_COL_TILE, final_norm=False)
    return x
```

```python
import functools
import math

import numpy as np
import jax
import jax.numpy as jnp
from jax import lax
from jax.experimental import pallas as pl
from jax.experimental.pallas import tpu as pltpu

F32 = jnp.float32
BF16 = jnp.bfloat16

EPS = 1e-6
HEAD_DIM = 128
GRID_W = 64
NA_KH = 8
NA_KW = 16
NA_ROWS_PER_BLOCK = 8
FN_GROUP_DIM = 128
HY_ORDER = 2
HY_BANDS = 16
HY_FAST_DECAY = 0.3
HY_SLOW_DECAY = 1.5
HY_TARGET = 1e-2
HY_MIN_DECAY = math.log(HY_TARGET) / HY_SLOW_DECAY
HY_MAX_DECAY = math.log(HY_TARGET) / HY_FAST_DECAY
N_MOD = 6
LANES = 128
NEG_INF = -1e30
LOG2E = math.log2(math.e)
VMEM_LIMIT = 60 * 1024 * 1024

NT_DIMS = (((1,), (1,)), ((), ()))


def _cparams(sem):
    return pltpu.CompilerParams(dimension_semantics=sem, vmem_limit_bytes=VMEM_LIMIT)


def _hi_lo(a):
    hi = a.astype(BF16)
    lo = (a - hi.astype(F32)).astype(BF16)
    return hi, lo


def _np_hi_lo(m):
    m = np.asarray(m, np.float32)
    hi = m.astype(BF16)
    lo = (m - hi.astype(np.float32)).astype(BF16)
    return hi, lo


def _const_rhs3(m):
    hi, lo = _np_hi_lo(m)
    return jnp.asarray(np.concatenate([hi, hi, lo], axis=0))


def _const_lhs3(m):
    hi, lo = _np_hi_lo(m)
    return jnp.asarray(np.concatenate([hi, hi, lo], axis=1))


def _dot_data_const(a, c3):
    hi, lo = _hi_lo(a)
    return jnp.dot(jnp.concatenate([hi, lo, hi], axis=1), c3, preferred_element_type=F32)


def _dot_const_data(c3, b):
    hi, lo = _hi_lo(b)
    return jnp.dot(c3, jnp.concatenate([hi, lo, hi], axis=0), preferred_element_type=F32)


def _dot3(a, b):
    ah, al = _hi_lo(a)
    bh, bl = _hi_lo(b)
    return (jnp.dot(ah, bh, preferred_element_type=F32)
            + jnp.dot(al, bh, preferred_element_type=F32)
            + jnp.dot(ah, bl, preferred_element_type=F32))


def _dot3k(a, b):
    ah, al = _hi_lo(a)
    bh, bl = _hi_lo(b)
    return jnp.dot(jnp.concatenate([ah, al, ah], axis=1), jnp.concatenate([bh, bh, bl], axis=0),
                   preferred_element_type=F32)


def _const_bf16(m):
    return jnp.asarray(np.asarray(m, np.float32).astype(BF16))


def _dot1(a, b):
    return jnp.dot(a.astype(BF16), b.astype(BF16), preferred_element_type=F32)


def _ada_kernel(s_ref, w_ref, b_ref, o_ref):
    s = s_ref[...]
    s = s / (1.0 + jnp.exp(-s))
    o_ref[0] = _dot1(s, w_ref[0]) + b_ref[0]


def _ada_mod(s_in, ada_w, ada_b):
    depth, d, n = ada_w.shape
    tn = 1024
    return pl.pallas_call(
        _ada_kernel,
        out_shape=jax.ShapeDtypeStruct((depth, 8, n), F32),
        grid=(depth, n // tn),
        in_specs=[
            pl.BlockSpec((8, d), lambda l, j: (0, 0)),
            pl.BlockSpec((1, d, tn), lambda l, j: (l, 0, j)),
            pl.BlockSpec((1, 1, tn), lambda l, j: (l, 0, j)),
        ],
        out_specs=pl.BlockSpec((1, 8, tn), lambda l, j: (l, 0, j)),
        compiler_params=_cparams(("parallel", "parallel")),
        name="ada_mod",
    )(s_in, ada_w, ada_b.reshape(depth, 1, n))


def _norm_mod(x, g, sh, sc):
    ms = jnp.mean(x * x, axis=-1, keepdims=True)
    return (x * lax.rsqrt(ms + EPS) * g) * (1.0 + sc) + sh


NORM_SLABS = 4


def _proj_kernel(*refs, n_tm):
    x_refs = refs[:NORM_SLABS]
    g_ref, sh_ref, sc_ref, wa_ref, wb_ref, otm_ref, ocm_ref, h_ref = refs[NORM_SLABS:]
    j = pl.program_id(2)
    rs = x_refs[0].shape[1]

    @pl.when(j == 0)
    def _():
        g, sh, sc = g_ref[...], sh_ref[0], sc_ref[0]
        w = wa_ref[...]
        for s in range(NORM_SLABS):
            h = _norm_mod(x_refs[s][0], g, sh, sc).astype(BF16)
            h_ref[s * rs:(s + 1) * rs, :] = h
            otm_ref[0, s * rs:(s + 1) * rs, :] = jnp.dot(h, w, preferred_element_type=F32).astype(otm_ref.dtype)

    @pl.when((j > 0) & (j < n_tm))
    def _():
        otm_ref[0] = jnp.dot(h_ref[...], wa_ref[...], preferred_element_type=F32).astype(otm_ref.dtype)

    @pl.when(j >= n_tm)
    def _():
        y = jnp.dot(h_ref[...], wb_ref[...], preferred_element_type=F32)
        ocm_ref[0] = y.T.astype(ocm_ref.dtype)


def _proj(x, g, sh, sc, w_tok, w_cm, *, tm, tn_tok, tn_cm):
    b, l, d = x.shape
    n_tok, n_ch = w_tok.shape[1], w_cm.shape[1]
    tm = min(tm, l)
    n_tm, n_cm = n_tok // tn_tok, n_ch // tn_cm
    assert n_tm * tn_tok == n_tok and n_cm * tn_cm == n_ch and tm % (16 * NORM_SLABS) == 0
    vec = pl.BlockSpec((1, 1, d), lambda bi, i, j: (bi, 0, 0))
    rs = tm // NORM_SLABS
    nt, nj = l // tm, n_tm + n_cm
    n_slabs = b * l // rs
    assert nj > NORM_SLABS

    def slab(s):
        def index(bi, i, j):
            tile = bi * nt + i + (j >= nj - s).astype(jnp.int32)
            return (jnp.minimum(tile * NORM_SLABS + s, n_slabs - 1), 0, 0)
        return pl.BlockSpec((1, rs, d), index)

    tok_j = lambda j: jnp.minimum(j, n_tm - 1)
    cm_j = lambda j: jnp.maximum(j - n_tm, 0)
    return pl.pallas_call(
        functools.partial(_proj_kernel, n_tm=n_tm),
        out_shape=(jax.ShapeDtypeStruct((b, l, n_tok), BF16), jax.ShapeDtypeStruct((b, n_ch, l), BF16)),
        grid=(b, nt, nj),
        in_specs=[slab(s) for s in range(NORM_SLABS)] + [
            pl.BlockSpec((1, d), lambda bi, i, j: (0, 0)),
            vec, vec,
            pl.BlockSpec((d, tn_tok), lambda bi, i, j: (0, tok_j(j))),
            pl.BlockSpec((d, tn_cm), lambda bi, i, j: (0, cm_j(j))),
        ],
        out_specs=(pl.BlockSpec((1, tm, tn_tok), lambda bi, i, j: (bi, i, tok_j(j))),
                   pl.BlockSpec((1, tn_cm, tm), lambda bi, i, j: (bi, cm_j(j), i))),
        scratch_shapes=[pltpu.VMEM((tm, d), BF16)],
        compiler_params=_cparams(("parallel", "parallel", "arbitrary")),
        name="proj_in",
    )(*([x.reshape(n_slabs, rs, d)] * NORM_SLABS), g.reshape(1, d), sh.reshape(b, 1, d), sc.reshape(b, 1, d),
      w_tok, w_cm)


def _fold_kernel(w_ref, m_ref, o_ref):
    o_ref[0] = _dot_data_const(w_ref[...], m_ref[...])


def _fold_fourier_weights(w_f):
    d, width = w_f.shape
    gd = FN_GROUP_DIM
    idx = np.arange(gd)
    ang = 2.0 * np.pi * np.outer(idx, idx) / gd
    mat = np.concatenate([np.cos(ang), -np.sin(ang)], axis=1) / math.sqrt(gd)
    groups = width // gd
    return pl.pallas_call(
        _fold_kernel,
        out_shape=jax.ShapeDtypeStruct((2, d, width), F32),
        grid=(groups, 2),
        in_specs=[
            pl.BlockSpec((d, gd), lambda g, p: (0, g)),
            pl.BlockSpec((3 * gd, gd), lambda g, p: (0, p)),
        ],
        out_specs=pl.BlockSpec((1, d, gd), lambda g, p: (p, 0, g)),
        compiler_params=_cparams(("parallel", "parallel")),
        name="fold_fourier",
    )(w_f, _const_rhs3(mat))


def _na_bias_table(rpb, rows):
    h = rpb.shape[0]
    w, kw, kh, rb = GRID_W, NA_KW, NA_KH, NA_ROWS_PER_BLOCK
    col = np.arange(w)
    cs = np.clip(col - kw // 2, 0, w - kw)
    kc = np.arange(w)[None, :]
    col_ok = (kc >= cs[:, None]) & (kc < cs[:, None] + kw)
    padded = jnp.pad((rpb * LOG2E).astype(BF16), ((0, 0), (0, 0), (w - kw, w - kw)))
    t1 = jnp.stack([padded[:, :, w - 1 - q:2 * w - 1 - q] for q in range(w)], axis=1)
    t1 = jnp.where(col_ok[None, :, None, :], t1, NEG_INF)
    slots = 2 * rb
    blocks = []
    for t, row0 in enumerate((0, rb, rows - rb)):
        per_row = []
        for qr in range(rb):
            r = row0 + qr
            r0 = r - kh // 2 if t == 1 else min(max(r - kh // 2, 0), rows - kh)
            valid = [kr for kr in range(slots)
                     if r0 <= row0 - rb // 2 + kr < r0 + kh and (t == 1 or 0 <= row0 - rb // 2 + kr < rows)]
            lo, hi = valid[0], valid[-1] + 1
            d0 = (row0 - rb // 2 + lo) - r + (kh - 1)
            live = [t1[:, :, d0 + k, :] for k in range(hi - lo)]
            dead = jnp.full((h, w, w), NEG_INF, t1.dtype)
            per_row.append(jnp.concatenate([dead] * lo + live + [dead] * (slots - hi), axis=-1))
        blocks.append(jnp.stack(per_row, axis=1))
    return jnp.stack(blocks, axis=0).reshape(3, h, rb * w, slots * w)


NA_HEADS_PER_STEP = 8


def _na_kernel(q_ref, kp_ref, kc_ref, kn_ref, vp_ref, vc_ref, vn_ref, kx_ref, vx_ref, bias_ref, o_ref):
    k_all = jnp.concatenate([kp_ref[0], kc_ref[0], kn_ref[0]], axis=0)
    v_all = jnp.concatenate([vp_ref[0], vc_ref[0], vn_ref[0]], axis=0)
    hq = q_ref.shape[1] // 2
    band = (NA_ROWS_PER_BLOCK // 2 + NA_KH) * GRID_W
    for h in range(NA_HEADS_PER_STEP):
        sl = slice(h * HEAD_DIM, (h + 1) * HEAD_DIM)
        kx, vx = kx_ref[0, :, sl], vx_ref[0, :, sl]
        for part in range(2):
            rows = slice(part * hq, (part + 1) * hq)
            keys = slice(part * hq, part * hq + band)
            q = q_ref[0, rows, sl]
            s = (lax.dot_general(q, k_all[keys, sl], NT_DIMS, preferred_element_type=F32)
                 + bias_ref[0, h, rows, keys].astype(F32))
            sx = lax.dot_general(q, kx, NT_DIMS, preferred_element_type=F32)
            m = jnp.maximum(jnp.max(s, axis=-1, keepdims=True), jnp.max(sx, axis=-1, keepdims=True))
            p = jnp.exp2(s - m)
            px = jnp.exp2(sx - m)
            den = jnp.sum(p, axis=-1, keepdims=True) + jnp.sum(px, axis=-1, keepdims=True)
            o = (jnp.dot(p.astype(BF16), v_all[keys, sl], preferred_element_type=F32)
                 + jnp.dot(px.astype(BF16), vx, preferred_element_type=F32))
            o_ref[0, rows, sl] = (o / den).astype(o_ref.dtype)


def _na_attention(qkv, qkv_c, bias_tab, n_heads):
    b, l, _ = qkv.shape
    lc = qkv_c.shape[1]
    hps = NA_HEADS_PER_STEP
    wd = hps * HEAD_DIM
    tq = NA_ROWS_PER_BLOCK * GRID_W
    th = tq // 2
    nb = l // tq
    nh = l // th
    ng = n_heads // hps
    assert nb >= 2 and l % tq == 0 and n_heads % hps == 0

    def btype(i):
        return jnp.where(i == 0, 0, jnp.where(i == nb - 1, 2, 1))

    def cur(off):
        return pl.BlockSpec((1, tq, wd), lambda h, bi, i: (bi, i, off + h))

    def prev(off):
        return pl.BlockSpec((1, th, wd), lambda h, bi, i: (bi, jnp.maximum(2 * i - 1, 0), off + h))

    def nxt(off):
        return pl.BlockSpec((1, th, wd), lambda h, bi, i: (bi, jnp.minimum(2 * i + 2, nh - 1), off + h))

    def ctx(off):
        return pl.BlockSpec((1, lc, wd), lambda h, bi, i: (bi, 0, off + h))

    ko, vo = ng, 2 * ng
    return pl.pallas_call(
        _na_kernel,
        out_shape=jax.ShapeDtypeStruct((b, l, n_heads * HEAD_DIM), BF16),
        grid=(ng, b, nb),
        in_specs=[cur(0), prev(ko), cur(ko), nxt(ko), prev(vo), cur(vo), nxt(vo), ctx(ko), ctx(vo),
                  pl.BlockSpec((1, hps, tq, 2 * tq), lambda h, bi, i: (btype(i), h, 0, 0))],
        out_specs=pl.BlockSpec((1, tq, wd), lambda h, bi, i: (bi, i, h)),
        compiler_params=_cparams(("parallel", "parallel", "arbitrary")),
        name="na_attention",
    )(qkv, qkv, qkv, qkv, qkv, qkv, qkv, qkv_c, qkv_c, bias_tab)


def _ctx_attn_kernel(q_ref, k_ref, v_ref, o_ref):
    s = lax.dot_general(q_ref[0], k_ref[0], NT_DIMS, preferred_element_type=F32)
    m = jnp.max(s, axis=-1, keepdims=True)
    p = jnp.exp2(s - m)
    den = jnp.sum(p, axis=-1, keepdims=True)
    o_ref[0] = (jnp.dot(p.astype(BF16), v_ref[0], preferred_element_type=F32) / den).astype(o_ref.dtype)


def _ctx_attention(qkv_c, n_heads):
    b, lc, _ = qkv_c.shape
    hd = HEAD_DIM

    def spec(off):
        return pl.BlockSpec((1, lc, hd), lambda bi, h: (bi, 0, off + h))

    return pl.pallas_call(
        _ctx_attn_kernel,
        out_shape=jax.ShapeDtypeStruct((b, lc, n_heads * hd), F32),
        grid=(b, n_heads),
        in_specs=[spec(0), spec(n_heads), spec(2 * n_heads)],
        out_specs=spec(0),
        compiler_params=_cparams(("parallel", "parallel")),
        name="ctx_attention",
    )(qkv_c, qkv_c, qkv_c)


def _cos_sin(n_out, n_in, period):
    ang = 2.0 * np.pi * ((np.arange(n_out)[:, None] * np.arange(n_in)[None, :]) % period) / period
    return np.cos(ang), np.sin(ang)


def _lane_cat(x3):
    return jnp.concatenate([x3[j] for j in range(x3.shape[0])], axis=1)


def _rows_from_lanes(re, im, nc):
    return jnp.concatenate(
        [jnp.concatenate([re[:, j * LANES:(j + 1) * LANES], im[:, j * LANES:(j + 1) * LANES]], axis=1)
         for j in range(nc)], axis=0)


def _lanes_from_rows(x, nc, r):
    re = jnp.concatenate([x[j * r:(j + 1) * r, :LANES] for j in range(nc)], axis=1)
    im = jnp.concatenate([x[j * r:(j + 1) * r, LANES:] for j in range(nc)], axis=1)
    return jnp.concatenate([re, im], axis=0)


def _fourier_tables(r, nc):
    l = r * LANES
    c1, s1 = _cos_sin(r, r, r)
    m1 = np.block([[c1, s1], [-s1, c1]])
    tc, ts = _cos_sin(r, LANES, l)
    c2, s2 = _cos_sin(LANES, LANES, LANES)
    m2 = np.concatenate([c2, s2], axis=0) / math.sqrt(l)
    return (_const_bf16(m1), jnp.asarray(np.tile(tc, (1, nc)), F32), jnp.asarray(np.tile(ts, (1, nc)), F32),
            _const_bf16(m2))


def _fourier_kernel(zr_ref, zi_ref, m1_ref, tc_ref, ts_ref, m2_ref, o_ref, *, nc, r):
    z = jnp.concatenate([_lane_cat(zr_ref[0]), _lane_cat(zi_ref[0])], axis=0)
    a = _dot1(m1_ref[...], z)
    ar, ai = a[:r], a[r:]
    tc, ts = tc_ref[...], ts_ref[...]
    br = (ar * tc + ai * ts).astype(BF16)
    bi = (ai * tc - ar * ts).astype(BF16)
    y = _dot1(_rows_from_lanes(br, bi, nc), m2_ref[...])
    for j in range(nc):
        o_ref[0, j] = y[j * r:(j + 1) * r].T.astype(o_ref.dtype)


def _fourier_latent(pt, n_ch, nc):
    b, _, l = pt.shape
    r = l // LANES
    p4 = pt.reshape(b, pt.shape[1], r, LANES)
    m1, tc, ts, m2 = _fourier_tables(r, nc)
    nblk = n_ch // nc

    def const(a):
        return pl.BlockSpec(a.shape, lambda bi, c: (0,) * a.ndim)

    out = pl.pallas_call(
        functools.partial(_fourier_kernel, nc=nc, r=r),
        out_shape=jax.ShapeDtypeStruct((b, n_ch, LANES, r), BF16),
        grid=(b, nblk),
        in_specs=[
            pl.BlockSpec((1, nc, r, LANES), lambda bi, c: (bi, c, 0, 0)),
            pl.BlockSpec((1, nc, r, LANES), lambda bi, c: (bi, nblk + c, 0, 0)),
            const(m1), const(tc), const(ts), const(m2),
        ],
        out_specs=pl.BlockSpec((1, nc, LANES, r), lambda bi, c: (bi, c, 0, 0)),
        compiler_params=_cparams(("parallel", "parallel")),
        name="fourier_latent",
    )(p4, p4, m1, tc, ts, m2)
    return out.reshape(b, n_ch, l)


def _filter_kernel(bands_ref, w1t_ref, w1c_ref, w1s_ref, b1_ref, w2_ref, b2_ref, w3_ref, fr_ref, dl_ref,
                   o_ref, *, l, tl):
    i = pl.program_id(0)
    n = (i * tl + lax.broadcasted_iota(jnp.int32, (1, tl), 1))
    pos_i = jnp.where(n < l, n, 2 * l - n)
    pos = pos_i.astype(F32)
    t = pos / float(max(l - 1, 1))
    ang = bands_ref[...] * (2.0 * math.pi / l) * pos
    fr = fr_ref[...]
    cos_a, sin_a = jnp.cos(ang), jnp.sin(ang)
    w1c, w1s = w1c_ref[...], w1s_ref[...]
    pre = w1t_ref[...] * t
    for k in range(HY_BANDS):
        pre = pre + w1c[:, k:k + 1] * cos_a[k:k + 1, :] - w1s[:, k:k + 1] * sin_a[k:k + 1, :]
    kpad = jnp.zeros((w2_ref.shape[1] - w2_ref.shape[0], tl), F32)
    h1 = jnp.sin(fr * (pre + b1_ref[...]))
    h2 = jnp.sin(fr * (_dot3k(w2_ref[...], jnp.concatenate([h1, kpad], axis=0)) + b2_ref[...]))
    out = _dot3k(w3_ref[0], jnp.concatenate([h2, kpad], axis=0)) * jnp.exp(-t * dl_ref[...])
    o_ref[...] = jnp.where(n == l, 0.0, out).astype(o_ref.dtype)


def _hyena_filters(l, w1, b1, w2, b2, w3, freq, width):
    hid = w1.shape[1]
    kdim = -(-hid // LANES) * LANES
    pad = kdim - hid
    tl = min(2048, l)
    rows = HY_ORDER * width
    bands = np.linspace(1e-4, HY_BANDS - 1, HY_BANDS, dtype=np.float32).reshape(HY_BANDS, 1)
    deltas = np.abs(np.linspace(HY_MIN_DECAY, HY_MAX_DECAY, width, dtype=np.float32))
    dl = np.tile(deltas, HY_ORDER).reshape(rows, 1)
    w1t = w1.T
    w2t = jnp.pad(w2.T, ((0, 0), (0, pad)))
    w3d = w3.reshape(hid, HY_ORDER, 2, width).transpose(2, 1, 3, 0).reshape(2, rows, hid)
    w3d = jnp.pad(w3d, ((0, 0), (0, 0), (0, pad)))
    col = lambda v: v.reshape(hid, 1)

    def const(shape):
        return pl.BlockSpec(shape, lambda i: (0,) * len(shape))

    return pl.pallas_call(
        functools.partial(_filter_kernel, l=l, tl=tl),
        out_shape=jax.ShapeDtypeStruct((rows, 2 * l), BF16),
        grid=(2 * l // tl,),
        in_specs=[const((HY_BANDS, 1)), const((hid, 1)), const((hid, HY_BANDS)), const((hid, HY_BANDS)),
                  const((hid, 1)), const((hid, kdim)), const((hid, 1)),
                  pl.BlockSpec((1, rows, kdim), lambda i: (i // (l // tl), 0, 0)),
                  const((hid, 1)), const((rows, 1))],
        out_specs=pl.BlockSpec((rows, tl), lambda i: (0, i)),
        compiler_params=_cparams(("parallel",)),
        name="hyena_filters",
    )(jnp.asarray(bands), w1t[:, 0:1], w1t[:, 1:1 + HY_BANDS], w1t[:, 1 + HY_BANDS:], col(b1), w2t, col(b2),
      w3d, col(freq), jnp.asarray(dl))


def _conv_tables(r, nc):
    r2 = 2 * r
    n = r2 * LANES
    c1, s1 = _cos_sin(r2, r2, r2)
    c1h, s1h = c1[:, :r], s1[:, :r]
    m1 = np.block([[c1h, s1h], [-s1h, c1h]])
    m1_real = np.concatenate([c1, -s1], axis=0)
    tc, ts = _cos_sin(r2, LANES, n)
    c2, s2 = _cos_sin(LANES, LANES, LANES)
    m2 = np.block([[c2, -s2], [s2, c2]])
    m2i = np.block([[c2, s2], [-s2, c2]])
    ct, st = c1h.T, s1h.T
    m1i = np.block([[ct, -st], [st, ct]]) / n
    f32 = lambda a: jnp.asarray(a, F32)
    return dict(
        m1=_const_bf16(m1), m1_real=_const_bf16(m1_real), m2=_const_bf16(m2), m2i=_const_bf16(m2i),
        m1i=_const_bf16(m1i),
        tc_l=f32(np.tile(tc, (1, nc))), ts_l=f32(np.tile(ts, (1, nc))),
        tc_r=f32(np.tile(tc, (nc, 1))), ts_r=f32(np.tile(ts, (nc, 1))))


def _kernel_spectrum(k3, m1_real, tc_l, ts_l, m2, nc, r2):
    a = _dot1(m1_real, _lane_cat(k3))
    ar, ai = a[:r2], a[r2:]
    br = (ar * tc_l + ai * ts_l).astype(BF16)
    bi = (ai * tc_l - ar * ts_l).astype(BF16)
    return _dot1(_rows_from_lanes(br, bi, nc), m2)


def _shift_tokens(t, r, direction):
    lane = lax.broadcasted_iota(jnp.int32, t.shape, 2)
    row = lax.broadcasted_iota(jnp.int32, t.shape, 1)
    if direction < 0:
        near = pltpu.roll(t, 1, 2)
        wrap = pltpu.roll(near, 1, 1)
        edge = lane == 0
        dead = edge & (row == 0)
    else:
        near = pltpu.roll(t, LANES - 1, 2)
        wrap = pltpu.roll(near, r - 1, 1)
        edge = lane == LANES - 1
        dead = edge & (row == r - 1)
    return jnp.where(dead, 0.0, jnp.where(edge, wrap, near))


def _long_conv(zr, zi, kspec, m1, m2, m2i, m1i, tc_l, ts_l, tc_r, ts_r, nc, r):
    r2 = 2 * r
    a = _dot1(m1, jnp.concatenate([zr.astype(BF16), zi.astype(BF16)], axis=0))
    ar, ai = a[:r2], a[r2:]
    br = (ar * tc_l + ai * ts_l).astype(BF16)
    bi = (ai * tc_l - ar * ts_l).astype(BF16)
    x = _dot1(_rows_from_lanes(br, bi, nc), m2)
    xr, xi = x[:, :LANES], x[:, LANES:]
    kr, ki = kspec[:, :LANES], kspec[:, LANES:]
    y = jnp.concatenate([(xr * kr - xi * ki).astype(BF16), (xr * ki + xi * kr).astype(BF16)], axis=1)
    bb = _dot1(y, m2i)
    pr, pi = bb[:, :LANES], bb[:, LANES:]
    q = jnp.concatenate([(pr * tc_r - pi * ts_r).astype(BF16), (pi * tc_r + pr * ts_r).astype(BF16)], axis=1)
    out = _dot1(m1i, _lanes_from_rows(q, nc, r2))
    return out[:r], out[r:]


def _hyena_kernel(v0, v1, a0, a1, b0, b1, k0_ref, k1_ref, cw_ref, cb_ref, bias_ref,
                  m1_ref, m2_ref, m2i_ref, m1i_ref, tcl_ref, tsl_ref, tcr_ref, tsr_ref, m1r_ref, o_ref, *, nc, r):
    tabs = (m1_ref[...], m2_ref[...], m2i_ref[...], m1i_ref[...],
            tcl_ref[...], tsl_ref[...], tcr_ref[...], tsr_ref[...])

    def spectrum(k_ref):
        return _kernel_spectrum(k_ref[...], m1r_ref[...], tcl_ref[...], tsl_ref[...], m2_ref[...], nc, 2 * r)

    def short_conv(ref, part):
        t = ref[0].astype(F32)
        w = cw_ref[part]
        return (_lane_cat(_shift_tokens(t, r, -1)) * w[0:1] + _lane_cat(t) * w[1:2]
                + _lane_cat(_shift_tokens(t, r, +1)) * w[2:3] + cb_ref[part])

    vr, vi = short_conv(v0, 0), short_conv(v1, 0)
    x1r, x1i = short_conv(a0, 1), short_conv(a1, 1)
    x2r, x2i = short_conv(b0, 2), short_conv(b1, 2)
    bias = bias_ref[...]
    yr, yi = _long_conv(vr, vi, spectrum(k0_ref), *tabs, nc, r)
    zr = x1r * (yr + vr * bias[0:1])
    zi = x1i * (yi + vi * bias[0:1])
    yr, yi = _long_conv(zr, zi, spectrum(k1_ref), *tabs, nc, r)
    outr = x2r * (yr + zr * bias[1:2])
    outi = x2i * (yi + zi * bias[1:2])
    for j in range(nc):
        o_ref[0, j] = outr[:, j * LANES:(j + 1) * LANES].astype(o_ref.dtype)
        o_ref[1, j] = outi[:, j * LANES:(j + 1) * LANES].astype(o_ref.dtype)


def _hyena_latent(pt, ch0, width, k2t, conv_w, conv_b, bias, tabs, nc):
    b, c_all, l = pt.shape
    assert b == 2, "the two batch entries ride one complex transform"
    r = l // LANES
    r2 = 2 * r
    p4 = pt.reshape(b, c_all, r, LANES)
    k3 = k2t.reshape(k2t.shape[0], r2, LANES)
    nblk = width // nc
    rep = lambda a: jnp.repeat(a, LANES, axis=-1)
    cw = rep(conv_w.reshape(3, 3, width).transpose(1, 0, 2))
    cb = rep(conv_b.reshape(3, 1, width))
    bs = rep(bias)

    def inp(bi, part):
        off = (ch0 + part * width) // nc
        return pl.BlockSpec((1, nc, r, LANES), lambda c: (bi, off + c, 0, 0))

    def const(a):
        return pl.BlockSpec(a.shape, lambda c: (0,) * a.ndim)

    names = ("m1", "m2", "m2i", "m1i", "tc_l", "ts_l", "tc_r", "ts_r", "m1_real")
    consts = [tabs[k] for k in names]
    out = pl.pallas_call(
        functools.partial(_hyena_kernel, nc=nc, r=r),
        out_shape=jax.ShapeDtypeStruct((b, width, r, LANES), BF16),
        grid=(nblk,),
        in_specs=[inp(0, 0), inp(1, 0), inp(0, 1), inp(1, 1), inp(0, 2), inp(1, 2),
                  pl.BlockSpec((nc, r2, LANES), lambda c: (c, 0, 0)),
                  pl.BlockSpec((nc, r2, LANES), lambda c: (nblk + c, 0, 0)),
                  pl.BlockSpec((3, 3, nc * LANES), lambda c: (0, 0, c)),
                  pl.BlockSpec((3, 1, nc * LANES), lambda c: (0, 0, c)),
                  pl.BlockSpec((HY_ORDER, nc * LANES), lambda c: (0, c))]
                 + [const(a) for a in consts],
        out_specs=pl.BlockSpec((b, nc, r, LANES), lambda c: (0, c, 0, 0)),
        compiler_params=_cparams(("parallel",)),
        name="hyena_latent",
    )(p4, p4, p4, p4, p4, p4, k3, k3, cw, cb, bs, *consts)
    return out.reshape(b, width, l)


def _ctx_fourier_kernel(zr_ref, zi_ref, m_ref, o_ref):
    z = jnp.concatenate([zr_ref[0], zi_ref[0]], axis=1).astype(F32)
    o_ref[0] = _dot_data_const(z, m_ref[...])


def _ctx_fourier(ptc, n_ch):
    b, _, lc = ptc.shape
    c, s = _cos_sin(lc, lc, lc)
    m = _const_rhs3(np.concatenate([c, s], axis=0) / math.sqrt(lc))
    return pl.pallas_call(
        _ctx_fourier_kernel,
        out_shape=jax.ShapeDtypeStruct((b, n_ch, lc), F32),
        grid=(b,),
        in_specs=[pl.BlockSpec((1, n_ch, lc), lambda bi: (bi, 0, 0)),
                  pl.BlockSpec((1, n_ch, lc), lambda bi: (bi, 1, 0)),
                  pl.BlockSpec(m.shape, lambda bi: (0, 0))],
        out_specs=pl.BlockSpec((1, n_ch, lc), lambda bi: (bi, 0, 0)),
        compiler_params=_cparams(("parallel",)),
        name="ctx_fourier",
    )(ptc, ptc, m)


def _ctx_hyena_kernel(v0, v1, a0, a1, b0, b1, k_ref, cw_ref, cb_ref, bias_ref, mk_ref, mf_ref, mi_ref,
                      o_ref, *, lc, width):
    def short_conv(ref, part):
        t = ref[0].astype(F32)
        lane = lax.broadcasted_iota(jnp.int32, t.shape, 1)
        prv = jnp.where(lane == 0, 0.0, pltpu.roll(t, 1, 1))
        nxt = jnp.where(lane == lc - 1, 0.0, pltpu.roll(t, lc - 1, 1))
        w = cw_ref[part]
        return prv * w[:, 0:1] + t * w[:, 1:2] + nxt * w[:, 2:3] + cb_ref[part]

    kspec = _dot_data_const(k_ref[...].astype(F32), mk_ref[...])
    n = 2 * lc

    def long_conv(zr, zi, ks):
        x = _dot_data_const(jnp.concatenate([zr, zi], axis=1), mf_ref[...])
        xr, xi = x[:, :n], x[:, n:]
        kr, ki = ks[:, :n], ks[:, n:]
        y = jnp.concatenate([xr * kr - xi * ki, xr * ki + xi * kr], axis=1)
        out = _dot_data_const(y, mi_ref[...])
        return out[:, :lc], out[:, lc:]

    vr, vi = short_conv(v0, 0), short_conv(v1, 0)
    x1r, x1i = short_conv(a0, 1), short_conv(a1, 1)
    x2r, x2i = short_conv(b0, 2), short_conv(b1, 2)
    bias = bias_ref[...]
    yr, yi = long_conv(vr, vi, kspec[:width])
    zr = x1r * (yr + vr * bias[:, 0:1])
    zi = x1i * (yi + vi * bias[:, 0:1])
    yr, yi = long_conv(zr, zi, kspec[width:])
    o_ref[0] = x2r * (yr + zr * bias[:, 1:2])
    o_ref[1] = x2i * (yi + zi * bias[:, 1:2])


def _ctx_hyena(ptc, ch0, width, k2t, conv_w, conv_b, bias):
    b, _, lc = ptc.shape
    assert b == 2
    n = 2 * lc
    c, s = _cos_sin(n, n, n)
    mk = _const_rhs3(np.concatenate([c, -s], axis=1))
    ch, sh = c[:lc], s[:lc]
    mf = _const_rhs3(np.block([[ch, -sh], [sh, ch]]))
    ci, si = c[:, :lc], s[:, :lc]
    mi = _const_rhs3(np.block([[ci, si], [-si, ci]]) / n)
    cw = conv_w.reshape(3, 3, width).transpose(1, 2, 0)
    cb = conv_b.reshape(3, width, 1)

    def inp(bi, part):
        return pl.BlockSpec((1, width, lc), lambda i: (bi, ch0 // width + part, 0))

    def const(a):
        return pl.BlockSpec(a.shape, lambda i: (0,) * a.ndim)

    args = (k2t, cw, cb, bias.T, mk, mf, mi)
    return pl.pallas_call(
        functools.partial(_ctx_hyena_kernel, lc=lc, width=width),
        out_shape=jax.ShapeDtypeStruct((b, width, lc), F32),
        grid=(1,),
        in_specs=[inp(0, 0), inp(1, 0), inp(0, 1), inp(1, 1), inp(0, 2), inp(1, 2)] + [const(a) for a in args],
        out_specs=pl.BlockSpec((b, width, lc), lambda i: (0, 0, 0)),
        compiler_params=_cparams(("arbitrary",)),
        name="ctx_hyena",
    )(ptc, ptc, ptc, ptc, ptc, ptc, *args)


def _merge_kernel(a_ref, f_ref, hy_ref, x_ref, gate_ref, ga_ref, gf_ref, gh_ref, wa_ref, wf_ref, wh_ref, o_ref):
    a = a_ref[0].astype(F32)
    ya = a * lax.rsqrt(jnp.mean(a * a, axis=-1, keepdims=True) + EPS) * ga_ref[...]
    acc = jnp.dot(ya.astype(BF16), wa_ref[...], preferred_element_type=F32)

    def cm_part(ref, g_ref, w_ref):
        t = ref[0].astype(F32)
        y = t * lax.rsqrt(jnp.mean(t * t, axis=0, keepdims=True) + EPS) * g_ref[...]
        return jnp.dot(y.T.astype(BF16), w_ref[...], preferred_element_type=F32)

    acc = acc + cm_part(f_ref, gf_ref, wf_ref) + cm_part(hy_ref, gh_ref, wh_ref)
    o_ref[0] = x_ref[0] + gate_ref[0] * acc


def _merge_out(a, ft, ht, x, gate, g, w_all, layer, tm):
    b, l, d = x.shape
    wa_n, wf_n, wh_n = a.shape[2], ft.shape[1], ht.shape[1]
    assert wa_n % wf_n == 0 and wf_n == wh_n
    tm = min(tm, l)
    ga = g[:wa_n].reshape(1, wa_n)
    gf = g[wa_n:wa_n + wf_n].reshape(wf_n, 1)
    gh = g[wa_n + wf_n:].reshape(wh_n, 1)

    def const(arr):
        return pl.BlockSpec(arr.shape, lambda bi, i: (0,) * arr.ndim)

    def w_rows(n, blk):
        return pl.BlockSpec((None, n, d), lambda bi, i: (layer, blk, 0))

    return pl.pallas_call(
        _merge_kernel,
        out_shape=jax.ShapeDtypeStruct((b, l, d), F32),
        grid=(b, l // tm),
        in_specs=[pl.BlockSpec((1, tm, wa_n), lambda bi, i: (bi, i, 0)),
                  pl.BlockSpec((1, wf_n, tm), lambda bi, i: (bi, 0, i)),
                  pl.BlockSpec((1, wh_n, tm), lambda bi, i: (bi, 0, i)),
                  pl.BlockSpec((1, tm, d), lambda bi, i: (bi, i, 0)),
                  pl.BlockSpec((1, 1, d), lambda bi, i: (bi, 0, 0)),
                  const(ga), const(gf), const(gh),
                  w_rows(wa_n, 0), w_rows(wf_n, wa_n // wf_n), w_rows(wh_n, wa_n // wf_n + 1)],
        out_specs=pl.BlockSpec((1, tm, d), lambda bi, i: (bi, i, 0)),
        compiler_params=_cparams(("parallel", "parallel")),
        name="merge_out",
    )(a, ft, ht, x, gate.reshape(b, 1, d), ga, gf, gh, w_all, w_all, w_all)


HALO = 16


def _ffn_kernel(xp_ref, x_ref, xn_ref, g_ref, sh_ref, sc_ref, gate_ref, wg_ref, wu_ref, cw_ref, cb_ref, wd_ref,
                fg_ref, o_ref, h_ref, *, tm, final_norm):
    i = pl.program_id(1)
    c = pl.program_id(2)
    last_tile = pl.num_programs(1) - 1

    n_ext = tm + 2 * HALO

    def down_partial(ge, up):
        row = lax.broadcasted_iota(jnp.int32, ge.shape, 0)
        outside = ((row < HALO) & (i == 0)) | ((row >= HALO + tm) & (i == last_tile))
        ge = jnp.where(outside, 0.0, ge)
        gp = pltpu.roll(ge, 1, 0)[HALO:HALO + tm]
        gn = pltpu.roll(ge, n_ext - 1, 0)[HALO:HALO + tm]
        cw = cw_ref[...]
        conv = gp * cw[0:1] + ge[HALO:HALO + tm] * cw[1:2] + gn * cw[2:3] + cb_ref[...]
        inner = 0.7978845608028654 * (conv + 0.044715 * (conv * conv * conv))
        act = 0.5 * conv * (1.0 + jnp.tanh(inner)) * up
        return jnp.dot(act.astype(BF16), wd_ref[...], preferred_element_type=F32)

    @pl.when(c == 0)
    def _():
        g, sh, sc = g_ref[...], sh_ref[0], sc_ref[0]
        wg, wu = wg_ref[...], wu_ref[...]
        rs = tm // NORM_SLABS
        bounds = [(0, HALO)] + [(HALO + s * rs, HALO + (s + 1) * rs) for s in range(NORM_SLABS)] + [(HALO + tm, n_ext)]
        ge_parts, up_parts = [], []
        for lo, hi in bounds:
            if lo == 0:
                xs = xp_ref[0, 0]
            elif hi == n_ext:
                xs = xn_ref[0, 0]
            else:
                xs = x_ref[0, lo - HALO:hi - HALO, :]
            h = _norm_mod(xs, g, sh, sc).astype(BF16)
            h_ref[lo:hi, :] = h
            ge_parts.append(jnp.dot(h, wg, preferred_element_type=F32))
            if lo != 0 and hi != n_ext:
                up_parts.append(jnp.dot(h, wu, preferred_element_type=F32))
        o_ref[0] = down_partial(jnp.concatenate(ge_parts, axis=0), jnp.concatenate(up_parts, axis=0))

    @pl.when(c > 0)
    def _():
        h = h_ref[...]
        ge = jnp.dot(h, wg_ref[...], preferred_element_type=F32)
        up = jnp.dot(h[HALO:HALO + tm], wu_ref[...], preferred_element_type=F32)
        o_ref[0] += down_partial(ge, up)

    @pl.when(c == pl.num_programs(2) - 1)
    def _():
        y = x_ref[0] + gate_ref[0] * o_ref[0]
        if final_norm:
            y = y * lax.rsqrt(jnp.mean(y * y, axis=-1, keepdims=True) + EPS) * fg_ref[...]
        o_ref[0] = y


def _ffn(x, g, sh, sc, gate, w_up, conv_w, conv_b, w_down, layer, final_g, tm, tf, *, final_norm):
    b, l, d = x.shape
    dff = w_down.shape[1]
    tm = min(tm, l)
    nch = dff // tf
    x4 = x.reshape(b, l // HALO, HALO, d)
    per_tile = tm // HALO
    nhalo = l // HALO
    vec = pl.BlockSpec((1, 1, d), lambda bi, i, c: (bi, 0, 0))
    return pl.pallas_call(
        functools.partial(_ffn_kernel, tm=tm, final_norm=final_norm),
        out_shape=jax.ShapeDtypeStruct((b, l, d), F32),
        grid=(b, l // tm, nch),
        in_specs=[
            pl.BlockSpec((1, 1, HALO, d), lambda bi, i, c: (bi, jnp.maximum(i * per_tile - 1, 0), 0, 0)),
            pl.BlockSpec((1, tm, d), lambda bi, i, c: (bi, i, 0)),
            pl.BlockSpec((1, 1, HALO, d), lambda bi, i, c: (bi, jnp.minimum((i + 1) * per_tile, nhalo - 1), 0, 0)),
            pl.BlockSpec((1, d), lambda bi, i, c: (0, 0)),
            vec, vec, vec,
            pl.BlockSpec((None, d, tf), lambda bi, i, c: (layer, 0, c)),
            pl.BlockSpec((None, d, tf), lambda bi, i, c: (layer, 0, nch + c)),
            pl.BlockSpec((3, tf), lambda bi, i, c: (0, c)),
            pl.BlockSpec((1, tf), lambda bi, i, c: (0, c)),
            pl.BlockSpec((None, tf, d), lambda bi, i, c: (layer, c, 0)),
            pl.BlockSpec((1, d), lambda bi, i, c: (0, 0)),
        ],
        out_specs=pl.BlockSpec((1, tm, d), lambda bi, i, c: (bi, i, 0)),
        scratch_shapes=[pltpu.VMEM((tm + 2 * HALO, d), BF16)],
        compiler_params=_cparams(("parallel", "parallel", "arbitrary")),
        name="ffn",
    )(x4, x, x4, g.reshape(1, d), sh.reshape(b, 1, d), sc.reshape(b, 1, d), gate.reshape(b, 1, d),
      w_up, w_up, conv_w, conv_b.reshape(1, dff), w_down, final_g.reshape(1, d))


TOKEN_TILE = 512
PROJ_TOKEN_TILE = 1024
PROJ_TOK_COL_TILE = 1024
PROJ_CM_COL_TILE = 512
FFN_TOKEN_TILE = 1024
FFN_COL_TILE = 512
MIX_CHANNELS = 8
FOURIER_CHANNELS = 32


def kernel(x, c, ctx, c_ctx, ada_w, ada_b, norm1_g, norm2_g, w_in, na_rpb, hy_conv_w, hy_conv_b, hy_w1, hy_b1,
           hy_w2, hy_b2, hy_w3, hy_freq, hy_bias, mix_norm_g, w_out, ffn_w_up, ffn_conv_w, ffn_conv_b, ffn_w_down,
           final_norm_g):
    b, l, d = x.shape
    lc = ctx.shape[1]
    depth = ada_w.shape[0]
    na_w, fn_w = d // 2, d // 4
    hy_w = d - na_w - fn_w
    n_heads = na_w // HEAD_DIM
    qkv_w = 3 * na_w
    rows = l // GRID_W
    assert b + 1 <= 8 and l % (NA_ROWS_PER_BLOCK * GRID_W) == 0 and rows >= 2 * NA_ROWS_PER_BLOCK
    r = l // LANES
    nc = MIX_CHANNELS
    conv_tabs = _conv_tables(r, nc)

    s_in = jnp.concatenate([c, c_ctx[None], jnp.zeros((8 - b - 1, d), F32)], axis=0)
    mod_all = _ada_mod(s_in, ada_w, ada_b)
    xc = ctx
    w_out_bf, w_up_bf, w_down_bf = w_out.astype(BF16), ffn_w_up.astype(BF16), ffn_w_down.astype(BF16)
    for layer in range(depth):
        update_ctx = layer < depth - 1
        mod = mod_all[layer]
        sh1, sc1, g1, sh2, sc2, g2 = jnp.split(mod[:b], N_MOD, axis=-1)
        csh1, csc1, cg1, csh2, csc2, cg2 = jnp.split(jnp.broadcast_to(mod[b:b + 1], (b, N_MOD * d)), N_MOD, axis=-1)

        wl = w_in[layer]
        w_fold = _fold_fourier_weights(wl[:, qkv_w:qkv_w + fn_w])
        w_tok = jnp.concatenate([wl[:, :na_w] * (HEAD_DIM ** -0.5 * LOG2E), wl[:, na_w:qkv_w]], axis=1).astype(BF16)
        w_cm = jnp.concatenate([w_fold[0], w_fold[1], wl[:, qkv_w + fn_w:]], axis=1).astype(BF16)
        hy0 = 2 * fn_w

        tiles = dict(tm=PROJ_TOKEN_TILE, tn_tok=PROJ_TOK_COL_TILE, tn_cm=PROJ_CM_COL_TILE)
        qkv, pt = _proj(x, norm1_g[layer], sh1, sc1, w_tok, w_cm, **tiles)
        qkv_c, ptc = _proj(xc, norm1_g[layer], csh1, csc1, w_tok, w_cm, **tiles)

        a = _na_attention(qkv, qkv_c, _na_bias_table(na_rpb[layer], rows), n_heads)
        yf = _fourier_latent(pt, fn_w, FOURIER_CHANNELS)
        filt = (hy_w1[layer], hy_b1[layer], hy_w2[layer], hy_b2[layer], hy_w3[layer], hy_freq[layer])
        yh = _hyena_latent(pt, hy0, hy_w, _hyena_filters(l, *filt, hy_w), hy_conv_w[layer], hy_conv_b[layer],
                           hy_bias[layer], conv_tabs, nc)
        x_new = _merge_out(a, yf, yh, x, g1, mix_norm_g[layer], w_out_bf, layer, TOKEN_TILE)

        if update_ctx:
            ac = _ctx_attention(qkv_c, n_heads)
            yfc = _ctx_fourier(ptc, fn_w)
            yhc = _ctx_hyena(ptc, hy0, hy_w, _hyena_filters(lc, *filt, hy_w), hy_conv_w[layer], hy_conv_b[layer],
                             hy_bias[layer])
            xc = _merge_out(ac, yfc, yhc, xc, cg1, mix_norm_g[layer], w_out_bf, layer, TOKEN_TILE)
        x = x_new

        x = _ffn(x, norm2_g[layer], sh2, sc2, g2, w_up_bf, ffn_conv_w[layer], ffn_conv_b[layer], w_down_bf, layer,
                 final_norm_g, FFN_TOKEN_TILE, FFN_COL_TILE, final_norm=not update_ctx)
        if update_ctx:
            xc = _ffn(xc, norm2_g[layer], csh2, csc2, cg2, w_up_bf, ffn_conv_w[layer], ffn_conv_b[layer], w_down_bf,
                      layer, final_norm_g, FFN_TOKEN_TILE, FFN_COL_TILE, final_norm=False)
    return x
```

```python
import functools
import math

import numpy as np
import jax
import jax.numpy as jnp
from jax import lax
from jax.experimental import pallas as pl
from jax.experimental.pallas import tpu as pltpu

F32 = jnp.float32
BF16 = jnp.bfloat16

EPS = 1e-6
HEAD_DIM = 128
GRID_W = 64
NA_KH = 8
NA_KW = 16
NA_ROWS_PER_BLOCK = 8
FN_GROUP_DIM = 128
HY_ORDER = 2
HY_BANDS = 16
HY_FAST_DECAY = 0.3
HY_SLOW_DECAY = 1.5
HY_TARGET = 1e-2
HY_MIN_DECAY = math.log(HY_TARGET) / HY_SLOW_DECAY
HY_MAX_DECAY = math.log(HY_TARGET) / HY_FAST_DECAY
N_MOD = 6
LANES = 128
NEG_INF = -1e30
LOG2E = math.log2(math.e)
VMEM_LIMIT = 60 * 1024 * 1024

NT_DIMS = (((1,), (1,)), ((), ()))


def _cparams(sem):
    return pltpu.CompilerParams(dimension_semantics=sem, vmem_limit_bytes=VMEM_LIMIT)


def _hi_lo(a):
    hi = a.astype(BF16)
    lo = (a - hi.astype(F32)).astype(BF16)
    return hi, lo


def _np_hi_lo(m):
    m = np.asarray(m, np.float32)
    hi = m.astype(BF16)
    lo = (m - hi.astype(np.float32)).astype(BF16)
    return hi, lo


def _const_rhs3(m):
    hi, lo = _np_hi_lo(m)
    return jnp.asarray(np.concatenate([hi, hi, lo], axis=0))


def _const_lhs3(m):
    hi, lo = _np_hi_lo(m)
    return jnp.asarray(np.concatenate([hi, hi, lo], axis=1))


def _dot_data_const(a, c3):
    hi, lo = _hi_lo(a)
    return jnp.dot(jnp.concatenate([hi, lo, hi], axis=1), c3, preferred_element_type=F32)


def _dot_const_data(c3, b):
    hi, lo = _hi_lo(b)
    return jnp.dot(c3, jnp.concatenate([hi, lo, hi], axis=0), preferred_element_type=F32)


def _dot3(a, b):
    ah, al = _hi_lo(a)
    bh, bl = _hi_lo(b)
    return (jnp.dot(ah, bh, preferred_element_type=F32)
            + jnp.dot(al, bh, preferred_element_type=F32)
            + jnp.dot(ah, bl, preferred_element_type=F32))


def _dot3k(a, b):
    ah, al = _hi_lo(a)
    bh, bl = _hi_lo(b)
    return jnp.dot(jnp.concatenate([ah, al, ah], axis=1), jnp.concatenate([bh, bh, bl], axis=0),
                   preferred_element_type=F32)


def _const_bf16(m):
    return jnp.asarray(np.asarray(m, np.float32).astype(BF16))


def _dot1(a, b):
    return jnp.dot(a.astype(BF16), b.astype(BF16), preferred_element_type=F32)


def _ada_kernel(s_ref, w_ref, b_ref, o_ref):
    s = s_ref[...]
    s = s / (1.0 + jnp.exp(-s))
    o_ref[0] = _dot1(s, w_ref[0]) + b_ref[0]


def _ada_mod(s_in, ada_w, ada_b):
    depth, d, n = ada_w.shape
    tn = 1024
    return pl.pallas_call(
        _ada_kernel,
        out_shape=jax.ShapeDtypeStruct((depth, 8, n), F32),
        grid=(depth, n // tn),
        in_specs=[
            pl.BlockSpec((8, d), lambda l, j: (0, 0)),
            pl.BlockSpec((1, d, tn), lambda l, j: (l, 0, j)),
            pl.BlockSpec((1, 1, tn), lambda l, j: (l, 0, j)),
        ],
        out_specs=pl.BlockSpec((1, 8, tn), lambda l, j: (l, 0, j)),
        compiler_params=_cparams(("parallel", "parallel")),
        name="ada_mod",
    )(s_in, ada_w, ada_b.reshape(depth, 1, n))


def _norm_mod(x, g, sh, sc):
    ms = jnp.mean(x * x, axis=-1, keepdims=True)
    return (x * lax.rsqrt(ms + EPS) * g) * (1.0 + sc) + sh


NORM_SLABS = 4


def _proj_kernel(*refs, n_tm):
    x_refs = refs[:NORM_SLABS]
    g_ref, sh_ref, sc_ref, wa_ref, wb_ref, otm_ref, ocm_ref, h_ref = refs[NORM_SLABS:]
    j = pl.program_id(2)
    rs = x_refs[0].shape[1]

    @pl.when(j == 0)
    def _():
        g, sh, sc = g_ref[...], sh_ref[0], sc_ref[0]
        w = wa_ref[...]
        for s in range(NORM_SLABS):
            h = _norm_mod(x_refs[s][0], g, sh, sc).astype(BF16)
            h_ref[s * rs:(s + 1) * rs, :] = h
            otm_ref[0, s * rs:(s + 1) * rs, :] = jnp.dot(h, w, preferred_element_type=F32).astype(otm_ref.dtype)

    @pl.when((j > 0) & (j < n_tm))
    def _():
        otm_ref[0] = jnp.dot(h_ref[...], wa_ref[...], preferred_element_type=F32).astype(otm_ref.dtype)

    @pl.when(j >= n_tm)
    def _():
        y = jnp.dot(h_ref[...], wb_ref[...], preferred_element_type=F32)
        ocm_ref[0] = y.T.astype(ocm_ref.dtype)


def _proj(x, g, sh, sc, w_tok, w_cm, *, tm, tn_tok, tn_cm):
    b, l, d = x.shape
    n_tok, n_ch = w_tok.shape[1], w_cm.shape[1]
    tm = min(tm, l)
    n_tm, n_cm = n_tok // tn_tok, n_ch // tn_cm
    assert n_tm * tn_tok == n_tok and n_cm * tn_cm == n_ch and tm % (16 * NORM_SLABS) == 0
    vec = pl.BlockSpec((1, 1, d), lambda bi, i, j: (bi, 0, 0))
    rs = tm // NORM_SLABS
    nt, nj = l // tm, n_tm + n_cm
    n_slabs = b * l // rs
    assert nj > NORM_SLABS

    def slab(s):
        def index(bi, i, j):
            tile = bi * nt + i + (j >= nj - s).astype(jnp.int32)
            return (jnp.minimum(tile * NORM_SLABS + s, n_slabs - 1), 0, 0)
        return pl.BlockSpec((1, rs, d), index)

    tok_j = lambda j: jnp.minimum(j, n_tm - 1)
    cm_j = lambda j: jnp.maximum(j - n_tm, 0)
    return pl.pallas_call(
        functools.partial(_proj_kernel, n_tm=n_tm),
        out_shape=(jax.ShapeDtypeStruct((b, l, n_tok), BF16), jax.ShapeDtypeStruct((b, n_ch, l), BF16)),
        grid=(b, nt, nj),
        in_specs=[slab(s) for s in range(NORM_SLABS)] + [
            pl.BlockSpec((1, d), lambda bi, i, j: (0, 0)),
            vec, vec,
            pl.BlockSpec((d, tn_tok), lambda bi, i, j: (0, tok_j(j))),
            pl.BlockSpec((d, tn_cm), lambda bi, i, j: (0, cm_j(j))),
        ],
        out_specs=(pl.BlockSpec((1, tm, tn_tok), lambda bi, i, j: (bi, i, tok_j(j))),
                   pl.BlockSpec((1, tn_cm, tm), lambda bi, i, j: (bi, cm_j(j), i))),
        scratch_shapes=[pltpu.VMEM((tm, d), BF16)],
        compiler_params=_cparams(("parallel", "parallel", "arbitrary")),
        name="proj_in",
    )(*([x.reshape(n_slabs, rs, d)] * NORM_SLABS), g.reshape(1, d), sh.reshape(b, 1, d), sc.reshape(b, 1, d),
      w_tok, w_cm)


def _fold_kernel(w_ref, m_ref, o_ref):
    o_ref[0] = _dot_data_const(w_ref[...], m_ref[...])


def _fold_fourier_weights(w_f):
    d, width = w_f.shape
    gd = FN_GROUP_DIM
    idx = np.arange(gd)
    ang = 2.0 * np.pi * np.outer(idx, idx) / gd
    mat = np.concatenate([np.cos(ang), -np.sin(ang)], axis=1) / math.sqrt(gd)
    groups = width // gd
    return pl.pallas_call(
        _fold_kernel,
        out_shape=jax.ShapeDtypeStruct((2, d, width), F32),
        grid=(groups, 2),
        in_specs=[
            pl.BlockSpec((d, gd), lambda g, p: (0, g)),
            pl.BlockSpec((3 * gd, gd), lambda g, p: (0, p)),
        ],
        out_specs=pl.BlockSpec((1, d, gd), lambda g, p: (p, 0, g)),
        compiler_params=_cparams(("parallel", "parallel")),
        name="fold_fourier",
    )(w_f, _const_rhs3(mat))


def _na_bias_table(rpb, rows):
    h = rpb.shape[0]
    w, kw, kh, rb = GRID_W, NA_KW, NA_KH, NA_ROWS_PER_BLOCK
    col = np.arange(w)
    cs = np.clip(col - kw // 2, 0, w - kw)
    kc = np.arange(w)[None, :]
    col_ok = (kc >= cs[:, None]) & (kc < cs[:, None] + kw)
    padded = jnp.pad((rpb * LOG2E).astype(BF16), ((0, 0), (0, 0), (w - kw, w - kw)))
    t1 = jnp.stack([padded[:, :, w - 1 - q:2 * w - 1 - q] for q in range(w)], axis=1)
    t1 = jnp.where(col_ok[None, :, None, :], t1, NEG_INF)
    slots = 2 * rb
    blocks = []
    for t, row0 in enumerate((0, rb, rows - rb)):
        per_row = []
        for qr in range(rb):
            r = row0 + qr
            r0 = r - kh // 2 if t == 1 else min(max(r - kh // 2, 0), rows - kh)
            valid = [kr for kr in range(slots)
                     if r0 <= row0 - rb // 2 + kr < r0 + kh and (t == 1 or 0 <= row0 - rb // 2 + kr < rows)]
            lo, hi = valid[0], valid[-1] + 1
            d0 = (row0 - rb // 2 + lo) - r + (kh - 1)
            live = [t1[:, :, d0 + k, :] for k in range(hi - lo)]
            dead = jnp.full((h, w, w), NEG_INF, t1.dtype)
            per_row.append(jnp.concatenate([dead] * lo + live + [dead] * (slots - hi), axis=-1))
        blocks.append(jnp.stack(per_row, axis=1))
    return jnp.stack(blocks, axis=0).reshape(3, h, rb * w, slots * w)


NA_HEADS_PER_STEP = 8


def _na_kernel(q_ref, kp_ref, kc_ref, kn_ref, vp_ref, vc_ref, vn_ref, kx_ref, vx_ref, bias_ref, o_ref):
    k_all = jnp.concatenate([kp_ref[0], kc_ref[0], kn_ref[0]], axis=0)
    v_all = jnp.concatenate([vp_ref[0], vc_ref[0], vn_ref[0]], axis=0)
    hq = q_ref.shape[1] // 2
    band = (NA_ROWS_PER_BLOCK // 2 + NA_KH) * GRID_W
    for h in range(NA_HEADS_PER_STEP):
        sl = slice(h * HEAD_DIM, (h + 1) * HEAD_DIM)
        kx, vx = kx_ref[0, :, sl], vx_ref[0, :, sl]
        for part in range(2):
            rows = slice(part * hq, (part + 1) * hq)
            keys = slice(part * hq, part * hq + band)
            q = q_ref[0, rows, sl]
            s = (lax.dot_general(q, k_all[keys, sl], NT_DIMS, preferred_element_type=F32)
                 + bias_ref[0, h, rows, keys].astype(F32))
            sx = lax.dot_general(q, kx, NT_DIMS, preferred_element_type=F32)
            m = jnp.maximum(jnp.max(s, axis=-1, keepdims=True), jnp.max(sx, axis=-1, keepdims=True))
            p = jnp.exp2(s - m)
            px = jnp.exp2(sx - m)
            den = jnp.sum(p, axis=-1, keepdims=True) + jnp.sum(px, axis=-1, keepdims=True)
            o = (jnp.dot(p.astype(BF16), v_all[keys, sl], preferred_element_type=F32)
                 + jnp.dot(px.astype(BF16), vx, preferred_element_type=F32))
            o_ref[0, rows, sl] = (o / den).astype(o_ref.dtype)


def _na_attention(qkv, qkv_c, bias_tab, n_heads):
    b, l, _ = qkv.shape
    lc = qkv_c.shape[1]
    hps = NA_HEADS_PER_STEP
    wd = hps * HEAD_DIM
    tq = NA_ROWS_PER_BLOCK * GRID_W
    th = tq // 2
    nb = l // tq
    nh = l // th
    ng = n_heads // hps
    assert nb >= 2 and l % tq == 0 and n_heads % hps == 0

    def btype(i):
        return jnp.where(i == 0, 0, jnp.where(i == nb - 1, 2, 1))

    def cur(off):
        return pl.BlockSpec((1, tq, wd), lambda h, bi, i: (bi, i, off + h))

    def prev(off):
        return pl.BlockSpec((1, th, wd), lambda h, bi, i: (bi, jnp.maximum(2 * i - 1, 0), off + h))

    def nxt(off):
        return pl.BlockSpec((1, th, wd), lambda h, bi, i: (bi, jnp.minimum(2 * i + 2, nh - 1), off + h))

    def ctx(off):
        return pl.BlockSpec((1, lc, wd), lambda h, bi, i: (bi, 0, off + h))

    ko, vo = ng, 2 * ng
    return pl.pallas_call(
        _na_kernel,
        out_shape=jax.ShapeDtypeStruct((b, l, n_heads * HEAD_DIM), BF16),
        grid=(ng, b, nb),
        in_specs=[cur(0), prev(ko), cur(ko), nxt(ko), prev(vo), cur(vo), nxt(vo), ctx(ko), ctx(vo),
                  pl.BlockSpec((1, hps, tq, 2 * tq), lambda h, bi, i: (btype(i), h, 0, 0))],
        out_specs=pl.BlockSpec((1, tq, wd), lambda h, bi, i: (bi, i, h)),
        compiler_params=_cparams(("parallel", "parallel", "arbitrary")),
        name="na_attention",
    )(qkv, qkv, qkv, qkv, qkv, qkv, qkv, qkv_c, qkv_c, bias_tab)


def _ctx_attn_kernel(q_ref, k_ref, v_ref, o_ref):
    s = lax.dot_general(q_ref[0], k_ref[0], NT_DIMS, preferred_element_type=F32)
    m = jnp.max(s, axis=-1, keepdims=True)
    p = jnp.exp2(s - m)
    den = jnp.sum(p, axis=-1, keepdims=True)
    o_ref[0] = (jnp.dot(p.astype(BF16), v_ref[0], preferred_element_type=F32) / den).astype(o_ref.dtype)


def _ctx_attention(qkv_c, n_heads):
    b, lc, _ = qkv_c.shape
    hd = HEAD_DIM

    def spec(off):
        return pl.BlockSpec((1, lc, hd), lambda bi, h: (bi, 0, off + h))

    return pl.pallas_call(
        _ctx_attn_kernel,
        out_shape=jax.ShapeDtypeStruct((b, lc, n_heads * hd), F32),
        grid=(b, n_heads),
        in_specs=[spec(0), spec(n_heads), spec(2 * n_heads)],
        out_specs=spec(0),
        compiler_params=_cparams(("parallel", "parallel")),
        name="ctx_attention",
    )(qkv_c, qkv_c, qkv_c)


def _cos_sin(n_out, n_in, period):
    ang = 2.0 * np.pi * ((np.arange(n_out)[:, None] * np.arange(n_in)[None, :]) % period) / period
    return np.cos(ang), np.sin(ang)


def _lane_cat(x3):
    return jnp.concatenate([x3[j] for j in range(x3.shape[0])], axis=1)


def _rows_from_lanes(re, im, nc):
    return jnp.concatenate(
        [jnp.concatenate([re[:, j * LANES:(j + 1) * LANES], im[:, j * LANES:(j + 1) * LANES]], axis=1)
         for j in range(nc)], axis=0)


def _lanes_from_rows(x, nc, r):
    re = jnp.concatenate([x[j * r:(j + 1) * r, :LANES] for j in range(nc)], axis=1)
    im = jnp.concatenate([x[j * r:(j + 1) * r, LANES:] for j in range(nc)], axis=1)
    return jnp.concatenate([re, im], axis=0)


def _fourier_tables(r, nc):
    l = r * LANES
    c1, s1 = _cos_sin(r, r, r)
    m1 = np.block([[c1, s1], [-s1, c1]])
    tc, ts = _cos_sin(r, LANES, l)
    c2, s2 = _cos_sin(LANES, LANES, LANES)
    m2 = np.concatenate([c2, s2], axis=0) / math.sqrt(l)
    return (_const_bf16(m1), jnp.asarray(np.tile(tc, (1, nc)), F32), jnp.asarray(np.tile(ts, (1, nc)), F32),
            _const_bf16(m2))


def _fourier_kernel(zr_ref, zi_ref, m1_ref, tc_ref, ts_ref, m2_ref, o_ref, *, nc, r):
    z = jnp.concatenate([_lane_cat(zr_ref[0]), _lane_cat(zi_ref[0])], axis=0)
    a = _dot1(m1_ref[...], z)
    ar, ai = a[:r], a[r:]
    tc, ts = tc_ref[...], ts_ref[...]
    br = (ar * tc + ai * ts).astype(BF16)
    bi = (ai * tc - ar * ts).astype(BF16)
    y = _dot1(_rows_from_lanes(br, bi, nc), m2_ref[...])
    for j in range(nc):
        o_ref[0, j] = y[j * r:(j + 1) * r].T.astype(o_ref.dtype)


def _fourier_latent(pt, n_ch, nc):
    b, _, l = pt.shape
    r = l // LANES
    p4 = pt.reshape(b, pt.shape[1], r, LANES)
    m1, tc, ts, m2 = _fourier_tables(r, nc)
    nblk = n_ch // nc

    def const(a):
        return pl.BlockSpec(a.shape, lambda bi, c: (0,) * a.ndim)

    out = pl.pallas_call(
        functools.partial(_fourier_kernel, nc=nc, r=r),
        out_shape=jax.ShapeDtypeStruct((b, n_ch, LANES, r), BF16),
        grid=(b, nblk),
        in_specs=[
            pl.BlockSpec((1, nc, r, LANES), lambda bi, c: (bi, c, 0, 0)),
            pl.BlockSpec((1, nc, r, LANES), lambda bi, c: (bi, nblk + c, 0, 0)),
            const(m1), const(tc), const(ts), const(m2),
        ],
        out_specs=pl.BlockSpec((1, nc, LANES, r), lambda bi, c: (bi, c, 0, 0)),
        compiler_params=_cparams(("parallel", "parallel")),
        name="fourier_latent",
    )(p4, p4, m1, tc, ts, m2)
    return out.reshape(b, n_ch, l)


def _filter_mlp_kernel(bands_ref, w1t_ref, w1c_ref, w1s_ref, b1_ref, w2_ref, b2_ref, fr_ref, o_ref, *, l, tl):
    pos = (pl.program_id(0) * tl + lax.broadcasted_iota(jnp.int32, (1, tl), 1)).astype(F32)
    t = pos / float(max(l - 1, 1))
    ang = bands_ref[...] * (2.0 * math.pi / l) * pos
    fr = fr_ref[...]
    cos_a, sin_a = jnp.cos(ang), jnp.sin(ang)
    w1c, w1s = w1c_ref[...], w1s_ref[...]
    pre = w1t_ref[...] * t
    for k in range(HY_BANDS):
        pre = pre + w1c[:, k:k + 1] * cos_a[k:k + 1, :] - w1s[:, k:k + 1] * sin_a[k:k + 1, :]
    kpad = jnp.zeros((w2_ref.shape[1] - w2_ref.shape[0], tl), F32)
    h1 = jnp.sin(fr * (pre + b1_ref[...]))
    o_ref[...] = jnp.sin(fr * (_dot3k(w2_ref[...], jnp.concatenate([h1, kpad], axis=0)) + b2_ref[...]))


def _filter_out_kernel(h_ref, w3_ref, dl_ref, o_ref, *, l, tl):
    n = pl.program_id(0) * tl + lax.broadcasted_iota(jnp.int32, (1, tl), 1)
    t = jnp.where(n < l, n, 2 * l - n).astype(F32) / float(max(l - 1, 1))
    h2 = h_ref[...]
    kpad = jnp.zeros((w3_ref.shape[2] - h2.shape[0], tl), F32)
    out = _dot1(w3_ref[0], jnp.concatenate([h2, kpad], axis=0)) * jnp.exp(-t * dl_ref[...])
    o_ref[...] = jnp.where(n == l, 0.0, out).astype(o_ref.dtype)


def _hyena_filters(l, w1, b1, w2, b2, w3, freq, width):
    hid = w1.shape[1]
    kdim = -(-hid // LANES) * LANES
    pad = kdim - hid
    tl = min(2048, l)
    rows = HY_ORDER * width
    bands = np.linspace(1e-4, HY_BANDS - 1, HY_BANDS, dtype=np.float32).reshape(HY_BANDS, 1)
    deltas = np.abs(np.linspace(HY_MIN_DECAY, HY_MAX_DECAY, width, dtype=np.float32))
    dl = np.tile(deltas, HY_ORDER).reshape(rows, 1)
    w1t = w1.T
    w2t = jnp.pad(w2.T, ((0, 0), (0, pad)))
    w3d = w3.reshape(hid, HY_ORDER, 2, width).transpose(2, 1, 3, 0).reshape(2, rows, hid)
    w3d = jnp.pad(w3d, ((0, 0), (0, 0), (0, pad)))
    col = lambda v: v.reshape(hid, 1)

    def const(shape):
        return pl.BlockSpec(shape, lambda i: (0,) * len(shape))

    hidden = pl.pallas_call(
        functools.partial(_filter_mlp_kernel, l=l, tl=tl),
        out_shape=jax.ShapeDtypeStruct((hid, l), F32),
        grid=(l // tl,),
        in_specs=[const((HY_BANDS, 1)), const((hid, 1)), const((hid, HY_BANDS)), const((hid, HY_BANDS)),
                  const((hid, 1)), const((hid, kdim)), const((hid, 1)), const((hid, 1))],
        out_specs=pl.BlockSpec((hid, tl), lambda i: (0, i)),
        compiler_params=_cparams(("parallel",)),
        name="hyena_filter_mlp",
    )(jnp.asarray(bands), w1t[:, 0:1], w1t[:, 1:1 + HY_BANDS], w1t[:, 1 + HY_BANDS:], col(b1), w2t, col(b2),
      col(freq))
    two_sided = jnp.concatenate([hidden, hidden[:, :1], jnp.flip(hidden[:, 1:], axis=1)], axis=1)
    return pl.pallas_call(
        functools.partial(_filter_out_kernel, l=l, tl=tl),
        out_shape=jax.ShapeDtypeStruct((rows, 2 * l), BF16),
        grid=(2 * l // tl,),
        in_specs=[pl.BlockSpec((hid, tl), lambda i: (0, i)),
                  pl.BlockSpec((1, rows, kdim), lambda i: (i // (l // tl), 0, 0)),
                  const((rows, 1))],
        out_specs=pl.BlockSpec((rows, tl), lambda i: (0, i)),
        compiler_params=_cparams(("parallel",)),
        name="hyena_filters",
    )(two_sided, w3d, jnp.asarray(dl))


def _conv_tables(r, nc):
    r2 = 2 * r
    n = r2 * LANES
    c1, s1 = _cos_sin(r2, r2, r2)
    c1h, s1h = c1[:, :r], s1[:, :r]
    m1 = np.block([[c1h, s1h], [-s1h, c1h]])
    m1_real = np.concatenate([c1, -s1], axis=0)
    tc, ts = _cos_sin(r2, LANES, n)
    c2, s2 = _cos_sin(LANES, LANES, LANES)
    m2 = np.block([[c2, -s2], [s2, c2]])
    m2i = np.block([[c2, s2], [-s2, c2]])
    ct, st = c1h.T, s1h.T
    m1i = np.block([[ct, -st], [st, ct]]) / n
    return dict(
        m1=_const_bf16(m1), m1_real=_const_bf16(m1_real), m2=_const_bf16(m2), m2i=_const_bf16(m2i),
        m1i=_const_bf16(m1i),
        tc_l=_const_bf16(np.tile(tc, (1, nc))), ts_l=_const_bf16(np.tile(ts, (1, nc))),
        tc_r=_const_bf16(np.tile(tc, (nc, 1))), ts_r=_const_bf16(np.tile(ts, (nc, 1))))


def _kernel_spectrum(k3, m1_real, tc_l, ts_l, m2, nc, r2):
    a = _dot1(m1_real, _lane_cat(k3)).astype(BF16)
    ar, ai = a[:r2], a[r2:]
    return _dot1(_rows_from_lanes(ar * tc_l + ai * ts_l, ai * tc_l - ar * ts_l, nc), m2).astype(BF16)


def _shift_tokens(t, r, direction):
    lane = lax.broadcasted_iota(jnp.int32, t.shape, 2)
    row = lax.broadcasted_iota(jnp.int32, t.shape, 1)
    if direction < 0:
        near = pltpu.roll(t, 1, 2)
        wrap = pltpu.roll(near, 1, 1)
        edge = lane == 0
        dead = edge & (row == 0)
    else:
        near = pltpu.roll(t, LANES - 1, 2)
        wrap = pltpu.roll(near, r - 1, 1)
        edge = lane == LANES - 1
        dead = edge & (row == r - 1)
    return jnp.where(dead, 0.0, jnp.where(edge, wrap, near))


def _long_conv(zr, zi, kspec, m1, m2, m2i, m1i, tc_l, ts_l, tc_r, ts_r, nc, r):
    r2 = 2 * r
    a = _dot1(m1, jnp.concatenate([zr.astype(BF16), zi.astype(BF16)], axis=0)).astype(BF16)
    ar, ai = a[:r2], a[r2:]
    x = _dot1(_rows_from_lanes(ar * tc_l + ai * ts_l, ai * tc_l - ar * ts_l, nc), m2).astype(BF16)
    xr, xi = x[:, :LANES], x[:, LANES:]
    kr, ki = kspec[:, :LANES], kspec[:, LANES:]
    bb = _dot1(jnp.concatenate([xr * kr - xi * ki, xr * ki + xi * kr], axis=1), m2i).astype(BF16)
    pr, pi = bb[:, :LANES], bb[:, LANES:]
    q = jnp.concatenate([pr * tc_r - pi * ts_r, pi * tc_r + pr * ts_r], axis=1)
    out = _dot1(m1i, _lanes_from_rows(q, nc, r2))
    return out[:r], out[r:]


def _hyena_kernel(v0, v1, a0, a1, b0, b1, k0_ref, k1_ref, cw_ref, cb_ref, bias_ref,
                  m1_ref, m2_ref, m2i_ref, m1i_ref, tcl_ref, tsl_ref, tcr_ref, tsr_ref, m1r_ref, o_ref, *, nc, r):
    tabs = (m1_ref[...], m2_ref[...], m2i_ref[...], m1i_ref[...],
            tcl_ref[...], tsl_ref[...], tcr_ref[...], tsr_ref[...])

    def spectrum(k_ref):
        return _kernel_spectrum(k_ref[...], m1r_ref[...], tcl_ref[...], tsl_ref[...], m2_ref[...], nc, 2 * r)

    def short_conv(ref, part):
        t = ref[0].astype(F32)
        w = cw_ref[part]
        return (_lane_cat(_shift_tokens(t, r, -1)) * w[0:1] + _lane_cat(t) * w[1:2]
                + _lane_cat(_shift_tokens(t, r, +1)) * w[2:3] + cb_ref[part])

    vr, vi = short_conv(v0, 0), short_conv(v1, 0)
    x1r, x1i = short_conv(a0, 1), short_conv(a1, 1)
    x2r, x2i = short_conv(b0, 2), short_conv(b1, 2)
    bias = bias_ref[...]
    yr, yi = _long_conv(vr, vi, spectrum(k0_ref), *tabs, nc, r)
    zr = x1r * (yr + vr * bias[0:1])
    zi = x1i * (yi + vi * bias[0:1])
    yr, yi = _long_conv(zr, zi, spectrum(k1_ref), *tabs, nc, r)
    outr = x2r * (yr + zr * bias[1:2])
    outi = x2i * (yi + zi * bias[1:2])
    for j in range(nc):
        o_ref[0, j] = outr[:, j * LANES:(j + 1) * LANES].astype(o_ref.dtype)
        o_ref[1, j] = outi[:, j * LANES:(j + 1) * LANES].astype(o_ref.dtype)


def _hyena_latent(pt, ch0, width, k2t, conv_w, conv_b, bias, tabs, nc):
    b, c_all, l = pt.shape
    assert b == 2, "the two batch entries ride one complex transform"
    r = l // LANES
    r2 = 2 * r
    p4 = pt.reshape(b, c_all, r, LANES)
    k3 = k2t.reshape(k2t.shape[0], r2, LANES)
    nblk = width // nc
    rep = lambda a: jnp.repeat(a, LANES, axis=-1)
    cw = rep(conv_w.reshape(3, 3, width).transpose(1, 0, 2))
    cb = rep(conv_b.reshape(3, 1, width))
    bs = rep(bias)

    def inp(bi, part):
        off = (ch0 + part * width) // nc
        return pl.BlockSpec((1, nc, r, LANES), lambda c: (bi, off + c, 0, 0))

    def const(a):
        return pl.BlockSpec(a.shape, lambda c: (0,) * a.ndim)

    names = ("m1", "m2", "m2i", "m1i", "tc_l", "ts_l", "tc_r", "ts_r", "m1_real")
    consts = [tabs[k] for k in names]
    out = pl.pallas_call(
        functools.partial(_hyena_kernel, nc=nc, r=r),
        out_shape=jax.ShapeDtypeStruct((b, width, r, LANES), BF16),
        grid=(nblk,),
        in_specs=[inp(0, 0), inp(1, 0), inp(0, 1), inp(1, 1), inp(0, 2), inp(1, 2),
                  pl.BlockSpec((nc, r2, LANES), lambda c: (c, 0, 0)),
                  pl.BlockSpec((nc, r2, LANES), lambda c: (nblk + c, 0, 0)),
                  pl.BlockSpec((3, 3, nc * LANES), lambda c: (0, 0, c)),
                  pl.BlockSpec((3, 1, nc * LANES), lambda c: (0, 0, c)),
                  pl.BlockSpec((HY_ORDER, nc * LANES), lambda c: (0, c))]
                 + [const(a) for a in consts],
        out_specs=pl.BlockSpec((b, nc, r, LANES), lambda c: (0, c, 0, 0)),
        compiler_params=_cparams(("parallel",)),
        name="hyena_latent",
    )(p4, p4, p4, p4, p4, p4, k3, k3, cw, cb, bs, *consts)
    return out.reshape(b, width, l)


def _ctx_fourier_kernel(zr_ref, zi_ref, m_ref, o_ref):
    z = jnp.concatenate([zr_ref[0], zi_ref[0]], axis=1).astype(F32)
    o_ref[0] = _dot_data_const(z, m_ref[...])


def _ctx_fourier(ptc, n_ch):
    b, _, lc = ptc.shape
    c, s = _cos_sin(lc, lc, lc)
    m = _const_rhs3(np.concatenate([c, s], axis=0) / math.sqrt(lc))
    return pl.pallas_call(
        _ctx_fourier_kernel,
        out_shape=jax.ShapeDtypeStruct((b, n_ch, lc), F32),
        grid=(b,),
        in_specs=[pl.BlockSpec((1, n_ch, lc), lambda bi: (bi, 0, 0)),
                  pl.BlockSpec((1, n_ch, lc), lambda bi: (bi, 1, 0)),
                  pl.BlockSpec(m.shape, lambda bi: (0, 0))],
        out_specs=pl.BlockSpec((1, n_ch, lc), lambda bi: (bi, 0, 0)),
        compiler_params=_cparams(("parallel",)),
        name="ctx_fourier",
    )(ptc, ptc, m)


def _ctx_hyena_kernel(v0, v1, a0, a1, b0, b1, k_ref, cw_ref, cb_ref, bias_ref, mk_ref, mf_ref, mi_ref,
                      o_ref, *, lc, width):
    def short_conv(ref, part):
        t = ref[0].astype(F32)
        lane = lax.broadcasted_iota(jnp.int32, t.shape, 1)
        prv = jnp.where(lane == 0, 0.0, pltpu.roll(t, 1, 1))
        nxt = jnp.where(lane == lc - 1, 0.0, pltpu.roll(t, lc - 1, 1))
        w = cw_ref[part]
        return prv * w[:, 0:1] + t * w[:, 1:2] + nxt * w[:, 2:3] + cb_ref[part]

    kspec = _dot_data_const(k_ref[...].astype(F32), mk_ref[...])
    n = 2 * lc

    def long_conv(zr, zi, ks):
        x = _dot_data_const(jnp.concatenate([zr, zi], axis=1), mf_ref[...])
        xr, xi = x[:, :n], x[:, n:]
        kr, ki = ks[:, :n], ks[:, n:]
        y = jnp.concatenate([xr * kr - xi * ki, xr * ki + xi * kr], axis=1)
        out = _dot_data_const(y, mi_ref[...])
        return out[:, :lc], out[:, lc:]

    vr, vi = short_conv(v0, 0), short_conv(v1, 0)
    x1r, x1i = short_conv(a0, 1), short_conv(a1, 1)
    x2r, x2i = short_conv(b0, 2), short_conv(b1, 2)
    bias = bias_ref[...]
    yr, yi = long_conv(vr, vi, kspec[:width])
    zr = x1r * (yr + vr * bias[:, 0:1])
    zi = x1i * (yi + vi * bias[:, 0:1])
    yr, yi = long_conv(zr, zi, kspec[width:])
    o_ref[0] = x2r * (yr + zr * bias[:, 1:2])
    o_ref[1] = x2i * (yi + zi * bias[:, 1:2])


def _ctx_hyena(ptc, ch0, width, k2t, conv_w, conv_b, bias):
    b, _, lc = ptc.shape
    assert b == 2
    n = 2 * lc
    c, s = _cos_sin(n, n, n)
    mk = _const_rhs3(np.concatenate([c, -s], axis=1))
    ch, sh = c[:lc], s[:lc]
    mf = _const_rhs3(np.block([[ch, -sh], [sh, ch]]))
    ci, si = c[:, :lc], s[:, :lc]
    mi = _const_rhs3(np.block([[ci, si], [-si, ci]]) / n)
    cw = conv_w.reshape(3, 3, width).transpose(1, 2, 0)
    cb = conv_b.reshape(3, width, 1)

    def inp(bi, part):
        return pl.BlockSpec((1, width, lc), lambda i: (bi, ch0 // width + part, 0))

    def const(a):
        return pl.BlockSpec(a.shape, lambda i: (0,) * a.ndim)

    args = (k2t, cw, cb, bias.T, mk, mf, mi)
    return pl.pallas_call(
        functools.partial(_ctx_hyena_kernel, lc=lc, width=width),
        out_shape=jax.ShapeDtypeStruct((b, width, lc), F32),
        grid=(1,),
        in_specs=[inp(0, 0), inp(1, 0), inp(0, 1), inp(1, 1), inp(0, 2), inp(1, 2)] + [const(a) for a in args],
        out_specs=pl.BlockSpec((b, width, lc), lambda i: (0, 0, 0)),
        compiler_params=_cparams(("arbitrary",)),
        name="ctx_hyena",
    )(ptc, ptc, ptc, ptc, ptc, ptc, *args)


def _merge_kernel(a_ref, f_ref, hy_ref, x_ref, gate_ref, ga_ref, gf_ref, gh_ref, wa_ref, wf_ref, wh_ref, o_ref):
    a = a_ref[0].astype(F32)
    ya = a * lax.rsqrt(jnp.mean(a * a, axis=-1, keepdims=True) + EPS) * ga_ref[...]
    acc = jnp.dot(ya.astype(BF16), wa_ref[...], preferred_element_type=F32)

    def cm_part(ref, g_ref, w_ref):
        t = ref[0].astype(F32)
        y = t * lax.rsqrt(jnp.mean(t * t, axis=0, keepdims=True) + EPS) * g_ref[...]
        return jnp.dot(y.T.astype(BF16), w_ref[...], preferred_element_type=F32)

    acc = acc + cm_part(f_ref, gf_ref, wf_ref) + cm_part(hy_ref, gh_ref, wh_ref)
    o_ref[0] = x_ref[0] + gate_ref[0] * acc


def _merge_out(a, ft, ht, x, gate, g, w_all, layer, tm):
    b, l, d = x.shape
    wa_n, wf_n, wh_n = a.shape[2], ft.shape[1], ht.shape[1]
    assert wa_n % wf_n == 0 and wf_n == wh_n
    tm = min(tm, l)
    ga = g[:wa_n].reshape(1, wa_n)
    gf = g[wa_n:wa_n + wf_n].reshape(wf_n, 1)
    gh = g[wa_n + wf_n:].reshape(wh_n, 1)

    def const(arr):
        return pl.BlockSpec(arr.shape, lambda bi, i: (0,) * arr.ndim)

    def w_rows(n, blk):
        return pl.BlockSpec((None, n, d), lambda bi, i: (layer, blk, 0))

    return pl.pallas_call(
        _merge_kernel,
        out_shape=jax.ShapeDtypeStruct((b, l, d), F32),
        grid=(b, l // tm),
        in_specs=[pl.BlockSpec((1, tm, wa_n), lambda bi, i: (bi, i, 0)),
                  pl.BlockSpec((1, wf_n, tm), lambda bi, i: (bi, 0, i)),
                  pl.BlockSpec((1, wh_n, tm), lambda bi, i: (bi, 0, i)),
                  pl.BlockSpec((1, tm, d), lambda bi, i: (bi, i, 0)),
                  pl.BlockSpec((1, 1, d), lambda bi, i: (bi, 0, 0)),
                  const(ga), const(gf), const(gh),
                  w_rows(wa_n, 0), w_rows(wf_n, wa_n // wf_n), w_rows(wh_n, wa_n // wf_n + 1)],
        out_specs=pl.BlockSpec((1, tm, d), lambda bi, i: (bi, i, 0)),
        compiler_params=_cparams(("parallel", "parallel")),
        name="merge_out",
    )(a, ft, ht, x, gate.reshape(b, 1, d), ga, gf, gh, w_all, w_all, w_all)


HALO = 16


def _ffn_kernel(xp_ref, x_ref, xn_ref, g_ref, sh_ref, sc_ref, gate_ref, wg_ref, wu_ref, cw_ref, cb_ref, wd_ref,
                fg_ref, o_ref, h_ref, *, tm, final_norm):
    i = pl.program_id(1)
    c = pl.program_id(2)
    last_tile = pl.num_programs(1) - 1

    n_ext = tm + 2 * HALO

    def down_partial(ge, up):
        row = lax.broadcasted_iota(jnp.int32, ge.shape, 0)
        outside = ((row < HALO) & (i == 0)) | ((row >= HALO + tm) & (i == last_tile))
        ge = jnp.where(outside, 0.0, ge)
        gp = pltpu.roll(ge, 1, 0)[HALO:HALO + tm]
        gn = pltpu.roll(ge, n_ext - 1, 0)[HALO:HALO + tm]
        cw = cw_ref[...]
        conv = gp * cw[0:1] + ge[HALO:HALO + tm] * cw[1:2] + gn * cw[2:3] + cb_ref[...]
        inner = 0.7978845608028654 * (conv + 0.044715 * (conv * conv * conv))
        act = 0.5 * conv * (1.0 + jnp.tanh(inner)) * up
        return jnp.dot(act.astype(BF16), wd_ref[...], preferred_element_type=F32)

    @pl.when(c == 0)
    def _():
        g, sh, sc = g_ref[...], sh_ref[0], sc_ref[0]
        wg, wu = wg_ref[...], wu_ref[...]
        rs = tm // NORM_SLABS
        bounds = [(0, HALO)] + [(HALO + s * rs, HALO + (s + 1) * rs) for s in range(NORM_SLABS)] + [(HALO + tm, n_ext)]
        ge_parts, up_parts = [], []
        for lo, hi in bounds:
            if lo == 0:
                xs = xp_ref[0, 0]
            elif hi == n_ext:
                xs = xn_ref[0, 0]
            else:
                xs = x_ref[0, lo - HALO:hi - HALO, :]
            h = _norm_mod(xs, g, sh, sc).astype(BF16)
            h_ref[lo:hi, :] = h
            ge_parts.append(jnp.dot(h, wg, preferred_element_type=F32))
            if lo != 0 and hi != n_ext:
                up_parts.append(jnp.dot(h, wu, preferred_element_type=F32))
        o_ref[0] = down_partial(jnp.concatenate(ge_parts, axis=0), jnp.concatenate(up_parts, axis=0))

    @pl.when(c > 0)
    def _():
        h = h_ref[...]
        ge = jnp.dot(h, wg_ref[...], preferred_element_type=F32)
        up = jnp.dot(h[HALO:HALO + tm], wu_ref[...], preferred_element_type=F32)
        o_ref[0] += down_partial(ge, up)

    @pl.when(c == pl.num_programs(2) - 1)
    def _():
        y = x_ref[0] + gate_ref[0] * o_ref[0]
        if final_norm:
            y = y * lax.rsqrt(jnp.mean(y * y, axis=-1, keepdims=True) + EPS) * fg_ref[...]
        o_ref[0] = y


def _ffn(x, g, sh, sc, gate, w_up, conv_w, conv_b, w_down, layer, final_g, tm, tf, *, final_norm):
    b, l, d = x.shape
    dff = w_down.shape[1]
    tm = min(tm, l)
    nch = dff // tf
    x4 = x.reshape(b, l // HALO, HALO, d)
    per_tile = tm // HALO
    nhalo = l // HALO
    vec = pl.BlockSpec((1, 1, d), lambda bi, i, c: (bi, 0, 0))
    return pl.pallas_call(
        functools.partial(_ffn_kernel, tm=tm, final_norm=final_norm),
        out_shape=jax.ShapeDtypeStruct((b, l, d), F32),
        grid=(b, l // tm, nch),
        in_specs=[
            pl.BlockSpec((1, 1, HALO, d), lambda bi, i, c: (bi, jnp.maximum(i * per_tile - 1, 0), 0, 0)),
            pl.BlockSpec((1, tm, d), lambda bi, i, c: (bi, i, 0)),
            pl.BlockSpec((1, 1, HALO, d), lambda bi, i, c: (bi, jnp.minimum((i + 1) * per_tile, nhalo - 1), 0, 0)),
            pl.BlockSpec((1, d), lambda bi, i, c: (0, 0)),
            vec, vec, vec,
            pl.BlockSpec((None, d, tf), lambda bi, i, c: (layer, 0, c)),
            pl.BlockSpec((None, d, tf), lambda bi, i, c: (layer, 0, nch + c)),
            pl.BlockSpec((3, tf), lambda bi, i, c: (0, c)),
            pl.BlockSpec((1, tf), lambda bi, i, c: (0, c)),
            pl.BlockSpec((None, tf, d), lambda bi, i, c: (layer, c, 0)),
            pl.BlockSpec((1, d), lambda bi, i, c: (0, 0)),
        ],
        out_specs=pl.BlockSpec((1, tm, d), lambda bi, i, c: (bi, i, 0)),
        scratch_shapes=[pltpu.VMEM((tm + 2 * HALO, d), BF16)],
        compiler_params=_cparams(("parallel", "parallel", "arbitrary")),
        name="ffn",
    )(x4, x, x4, g.reshape(1, d), sh.reshape(b, 1, d), sc.reshape(b, 1, d), gate.reshape(b, 1, d),
      w_up, w_up, conv_w, conv_b.reshape(1, dff), w_down, final_g.reshape(1, d))


TOKEN_TILE = 512
PROJ_TOKEN_TILE = 1024
PROJ_TOK_COL_TILE = 1024
PROJ_CM_COL_TILE = 512
FFN_TOKEN_TILE = 1024
FFN_COL_TILE = 512
MIX_CHANNELS = 8
FOURIER_CHANNELS = 32


def kernel(x, c, ctx, c_ctx, ada_w, ada_b, norm1_g, norm2_g, w_in, na_rpb, hy_conv_w, hy_conv_b, hy_w1, hy_b1,
           hy_w2, hy_b2, hy_w3, hy_freq, hy_bias, mix_norm_g, w_out, ffn_w_up, ffn_conv_w, ffn_conv_b, ffn_w_down,
           final_norm_g):
    b, l, d = x.shape
    lc = ctx.shape[1]
    depth = ada_w.shape[0]
    na_w, fn_w = d // 2, d // 4
    hy_w = d - na_w - fn_w
    n_heads = na_w // HEAD_DIM
    qkv_w = 3 * na_w
    rows = l // GRID_W
    assert b + 1 <= 8 and l % (NA_ROWS_PER_BLOCK * GRID_W) == 0 and rows >= 2 * NA_ROWS_PER_BLOCK
    r = l // LANES
    nc = MIX_CHANNELS
    conv_tabs = _conv_tables(r, nc)

    s_in = jnp.concatenate([c, c_ctx[None], jnp.zeros((8 - b - 1, d), F32)], axis=0)
    mod_all = _ada_mod(s_in, ada_w, ada_b)
    xc = ctx
    w_out_bf, w_up_bf, w_down_bf = w_out.astype(BF16), ffn_w_up.astype(BF16), ffn_w_down.astype(BF16)
    for layer in range(depth):
        update_ctx = layer < depth - 1
        mod = mod_all[layer]
        sh1, sc1, g1, sh2, sc2, g2 = jnp.split(mod[:b], N_MOD, axis=-1)
        csh1, csc1, cg1, csh2, csc2, cg2 = jnp.split(jnp.broadcast_to(mod[b:b + 1], (b, N_MOD * d)), N_MOD, axis=-1)

        wl = w_in[layer]
        w_fold = _fold_fourier_weights(wl[:, qkv_w:qkv_w + fn_w])
        w_tok = jnp.concatenate([wl[:, :na_w] * (HEAD_DIM ** -0.5 * LOG2E), wl[:, na_w:qkv_w]], axis=1).astype(BF16)
        w_cm = jnp.concatenate([w_fold[0], w_fold[1], wl[:, qkv_w + fn_w:]], axis=1).astype(BF16)
        hy0 = 2 * fn_w

        tiles = dict(tm=PROJ_TOKEN_TILE, tn_tok=PROJ_TOK_COL_TILE, tn_cm=PROJ_CM_COL_TILE)
        qkv, pt = _proj(x, norm1_g[layer], sh1, sc1, w_tok, w_cm, **tiles)
        qkv_c, ptc = _proj(xc, norm1_g[layer], csh1, csc1, w_tok, w_cm, **tiles)

        a = _na_attention(qkv, qkv_c, _na_bias_table(na_rpb[layer], rows), n_heads)
        yf = _fourier_latent(pt, fn_w, FOURIER_CHANNELS)
        filt = (hy_w1[layer], hy_b1[layer], hy_w2[layer], hy_b2[layer], hy_w3[layer], hy_freq[layer])
        yh = _hyena_latent(pt, hy0, hy_w, _hyena_filters(l, *filt, hy_w), hy_conv_w[layer], hy_conv_b[layer],
                           hy_bias[layer], conv_tabs, nc)
        x_new = _merge_out(a, yf, yh, x, g1, mix_norm_g[layer], w_out_bf, layer, TOKEN_TILE)

        if update_ctx:
            ac = _ctx_attention(qkv_c, n_heads)
            yfc = _ctx_fourier(ptc, fn_w)
            yhc = _ctx_hyena(ptc, hy0, hy_w, _hyena_filters(lc, *filt, hy_w), hy_conv_w[layer], hy_conv_b[layer],
                             hy_bias[layer])
            xc = _merge_out(ac, yfc, yhc, xc, cg1, mix_norm_g[layer], w_out_bf, layer, TOKEN_TILE)
        x = x_new

        x = _ffn(x, norm2_g[layer], sh2, sc2, g2, w_up_bf, ffn_conv_w[layer], ffn_conv_b[layer], w_down_bf, layer,
                 final_norm_g, FFN_TOKEN_TILE, FFN_COL_TILE, final_norm=not update_ctx)
        if update_ctx:
            xc = _ffn(xc, norm2_g[layer], csh2, csc2, cg2, w_up_bf, ffn_conv_w[layer], ffn_conv_b[layer], w_down_bf,
                      layer, final_norm_g, FFN_TOKEN_TILE, FFN_COL_TILE, final_norm=False)
    return x
```

```python
import functools
import math

import numpy as np
import jax
import jax.numpy as jnp
from jax import lax
from jax.experimental import pallas as pl
from jax.experimental.pallas import tpu as pltpu

F32 = jnp.float32
BF16 = jnp.bfloat16

EPS = 1e-6
HEAD_DIM = 128
GRID_W = 64
NA_KH = 8
NA_KW = 16
NA_ROWS_PER_BLOCK = 8
FN_GROUP_DIM = 128
HY_ORDER = 2
HY_BANDS = 16
HY_FAST_DECAY = 0.3
HY_SLOW_DECAY = 1.5
HY_TARGET = 1e-2
HY_MIN_DECAY = math.log(HY_TARGET) / HY_SLOW_DECAY
HY_MAX_DECAY = math.log(HY_TARGET) / HY_FAST_DECAY
N_MOD = 6
LANES = 128
NEG_INF = -1e30
LOG2E = math.log2(math.e)
VMEM_LIMIT = 60 * 1024 * 1024

NT_DIMS = (((1,), (1,)), ((), ()))


def _cparams(sem):
    return pltpu.CompilerParams(dimension_semantics=sem, vmem_limit_bytes=VMEM_LIMIT)


def _hi_lo(a):
    hi = a.astype(BF16)
    lo = (a - hi.astype(F32)).astype(BF16)
    return hi, lo


def _np_hi_lo(m):
    m = np.asarray(m, np.float32)
    hi = m.astype(BF16)
    lo = (m - hi.astype(np.float32)).astype(BF16)
    return hi, lo


def _const_rhs3(m):
    hi, lo = _np_hi_lo(m)
    return jnp.asarray(np.concatenate([hi, hi, lo], axis=0))


def _const_lhs3(m):
    hi, lo = _np_hi_lo(m)
    return jnp.asarray(np.concatenate([hi, hi, lo], axis=1))


def _dot_data_const(a, c3):
    hi, lo = _hi_lo(a)
    return jnp.dot(jnp.concatenate([hi, lo, hi], axis=1), c3, preferred_element_type=F32)


def _dot_const_data(c3, b):
    hi, lo = _hi_lo(b)
    return jnp.dot(c3, jnp.concatenate([hi, lo, hi], axis=0), preferred_element_type=F32)


def _dot3(a, b):
    ah, al = _hi_lo(a)
    bh, bl = _hi_lo(b)
    return (jnp.dot(ah, bh, preferred_element_type=F32)
            + jnp.dot(al, bh, preferred_element_type=F32)
            + jnp.dot(ah, bl, preferred_element_type=F32))


def _dot3k(a, b):
    ah, al = _hi_lo(a)
    bh, bl = _hi_lo(b)
    return jnp.dot(jnp.concatenate([ah, al, ah], axis=1), jnp.concatenate([bh, bh, bl], axis=0),
                   preferred_element_type=F32)


def _const_bf16(m):
    return jnp.asarray(np.asarray(m, np.float32).astype(BF16))


def _dot1(a, b):
    return jnp.dot(a.astype(BF16), b.astype(BF16), preferred_element_type=F32)


def _ada_kernel(s_ref, w_ref, b_ref, o_ref):
    s = s_ref[...]
    s = s / (1.0 + jnp.exp(-s))
    o_ref[0] = _dot1(s, w_ref[0]) + b_ref[0]


def _ada_mod(s_in, ada_w, ada_b):
    depth, d, n = ada_w.shape
    tn = 1024
    return pl.pallas_call(
        _ada_kernel,
        out_shape=jax.ShapeDtypeStruct((depth, 8, n), F32),
        grid=(depth, n // tn),
        in_specs=[
            pl.BlockSpec((8, d), lambda l, j: (0, 0)),
            pl.BlockSpec((1, d, tn), lambda l, j: (l, 0, j)),
            pl.BlockSpec((1, 1, tn), lambda l, j: (l, 0, j)),
        ],
        out_specs=pl.BlockSpec((1, 8, tn), lambda l, j: (l, 0, j)),
        compiler_params=_cparams(("parallel", "parallel")),
        name="ada_mod",
    )(s_in, ada_w, ada_b.reshape(depth, 1, n))


def _norm_mod(x, g, sh, sc):
    ms = jnp.mean(x * x, axis=-1, keepdims=True)
    return (x * lax.rsqrt(ms + EPS) * g) * (1.0 + sc) + sh


NORM_SLABS = 4


def _proj_kernel(*refs, n_tm):
    x_refs = refs[:NORM_SLABS]
    g_ref, sh_ref, sc_ref, wa_ref, wb_ref, otm_ref, ocm_ref, h_ref = refs[NORM_SLABS:]
    j = pl.program_id(2)
    rs = x_refs[0].shape[1]

    @pl.when(j == 0)
    def _():
        g, sh, sc = g_ref[...], sh_ref[0], sc_ref[0]
        w = wa_ref[...]
        for s in range(NORM_SLABS):
            h = _norm_mod(x_refs[s][0], g, sh, sc).astype(BF16)
            h_ref[s * rs:(s + 1) * rs, :] = h
            otm_ref[0, s * rs:(s + 1) * rs, :] = jnp.dot(h, w, preferred_element_type=F32).astype(otm_ref.dtype)

    @pl.when((j > 0) & (j < n_tm))
    def _():
        otm_ref[0] = jnp.dot(h_ref[...], wa_ref[...], preferred_element_type=F32).astype(otm_ref.dtype)

    @pl.when(j >= n_tm)
    def _():
        y = jnp.dot(h_ref[...], wb_ref[...], preferred_element_type=F32)
        ocm_ref[0] = y.T.astype(ocm_ref.dtype)


def _proj(x, g, sh, sc, w_tok, w_cm, *, tm, tn_tok, tn_cm):
    b, l, d = x.shape
    n_tok, n_ch = w_tok.shape[1], w_cm.shape[1]
    tm = min(tm, l)
    n_tm, n_cm = n_tok // tn_tok, n_ch // tn_cm
    assert n_tm * tn_tok == n_tok and n_cm * tn_cm == n_ch and tm % (16 * NORM_SLABS) == 0
    vec = pl.BlockSpec((1, 1, d), lambda bi, i, j: (bi, 0, 0))
    rs = tm // NORM_SLABS
    nt, nj = l // tm, n_tm + n_cm
    n_slabs = b * l // rs
    assert nj > NORM_SLABS

    def slab(s):
        def index(bi, i, j):
            tile = bi * nt + i + (j >= nj - s).astype(jnp.int32)
            return (jnp.minimum(tile * NORM_SLABS + s, n_slabs - 1), 0, 0)
        return pl.BlockSpec((1, rs, d), index)

    tok_j = lambda j: jnp.minimum(j, n_tm - 1)
    cm_j = lambda j: jnp.maximum(j - n_tm, 0)
    return pl.pallas_call(
        functools.partial(_proj_kernel, n_tm=n_tm),
        out_shape=(jax.ShapeDtypeStruct((b, l, n_tok), BF16), jax.ShapeDtypeStruct((b, n_ch, l), BF16)),
        grid=(b, nt, nj),
        in_specs=[slab(s) for s in range(NORM_SLABS)] + [
            pl.BlockSpec((1, d), lambda bi, i, j: (0, 0)),
            vec, vec,
            pl.BlockSpec((d, tn_tok), lambda bi, i, j: (0, tok_j(j))),
            pl.BlockSpec((d, tn_cm), lambda bi, i, j: (0, cm_j(j))),
        ],
        out_specs=(pl.BlockSpec((1, tm, tn_tok), lambda bi, i, j: (bi, i, tok_j(j))),
                   pl.BlockSpec((1, tn_cm, tm), lambda bi, i, j: (bi, cm_j(j), i))),
        scratch_shapes=[pltpu.VMEM((tm, d), BF16)],
        compiler_params=_cparams(("parallel", "parallel", "arbitrary")),
        name="proj_in",
    )(*([x.reshape(n_slabs, rs, d)] * NORM_SLABS), g.reshape(1, d), sh.reshape(b, 1, d), sc.reshape(b, 1, d),
      w_tok, w_cm)


def _fold_kernel(w_ref, m_ref, o_ref):
    o_ref[0] = _dot_data_const(w_ref[...], m_ref[...])


def _fold_fourier_weights(w_f):
    d, width = w_f.shape
    gd = FN_GROUP_DIM
    idx = np.arange(gd)
    ang = 2.0 * np.pi * np.outer(idx, idx) / gd
    mat = np.concatenate([np.cos(ang), -np.sin(ang)], axis=1) / math.sqrt(gd)
    groups = width // gd
    return pl.pallas_call(
        _fold_kernel,
        out_shape=jax.ShapeDtypeStruct((2, d, width), F32),
        grid=(groups, 2),
        in_specs=[
            pl.BlockSpec((d, gd), lambda g, p: (0, g)),
            pl.BlockSpec((3 * gd, gd), lambda g, p: (0, p)),
        ],
        out_specs=pl.BlockSpec((1, d, gd), lambda g, p: (p, 0, g)),
        compiler_params=_cparams(("parallel", "parallel")),
        name="fold_fourier",
    )(w_f, _const_rhs3(mat))


def _na_bias_table(rpb, rows):
    h = rpb.shape[0]
    w, kw, kh, rb = GRID_W, NA_KW, NA_KH, NA_ROWS_PER_BLOCK
    col = np.arange(w)
    cs = np.clip(col - kw // 2, 0, w - kw)
    kc = np.arange(w)[None, :]
    col_ok = (kc >= cs[:, None]) & (kc < cs[:, None] + kw)
    padded = jnp.pad((rpb * LOG2E).astype(BF16), ((0, 0), (0, 0), (w - kw, w - kw)))
    t1 = jnp.stack([padded[:, :, w - 1 - q:2 * w - 1 - q] for q in range(w)], axis=1)
    t1 = jnp.where(col_ok[None, :, None, :], t1, NEG_INF)
    slots = 2 * rb
    blocks = []
    for t, row0 in enumerate((0, rb, rows - rb)):
        per_row = []
        for qr in range(rb):
            r = row0 + qr
            r0 = r - kh // 2 if t == 1 else min(max(r - kh // 2, 0), rows - kh)
            valid = [kr for kr in range(slots)
                     if r0 <= row0 - rb // 2 + kr < r0 + kh and (t == 1 or 0 <= row0 - rb // 2 + kr < rows)]
            lo, hi = valid[0], valid[-1] + 1
            d0 = (row0 - rb // 2 + lo) - r + (kh - 1)
            live = [t1[:, :, d0 + k, :] for k in range(hi - lo)]
            dead = jnp.full((h, w, w), NEG_INF, t1.dtype)
            per_row.append(jnp.concatenate([dead] * lo + live + [dead] * (slots - hi), axis=-1))
        blocks.append(jnp.stack(per_row, axis=1))
    return jnp.stack(blocks, axis=0).reshape(3, h, rb * w, slots * w)


NA_HEADS_PER_STEP = 8


def _na_kernel(q_ref, kp_ref, kc_ref, kn_ref, vp_ref, vc_ref, vn_ref, kx_ref, vx_ref, bias_ref, o_ref):
    k_all = jnp.concatenate([kp_ref[0], kc_ref[0], kn_ref[0]], axis=0)
    v_all = jnp.concatenate([vp_ref[0], vc_ref[0], vn_ref[0]], axis=0)
    hq = q_ref.shape[1] // 2
    band = (NA_ROWS_PER_BLOCK // 2 + NA_KH) * GRID_W
    for h in range(NA_HEADS_PER_STEP):
        sl = slice(h * HEAD_DIM, (h + 1) * HEAD_DIM)
        kx, vx = kx_ref[0, :, sl], vx_ref[0, :, sl]
        for part in range(2):
            rows = slice(part * hq, (part + 1) * hq)
            keys = slice(part * hq, part * hq + band)
            q = q_ref[0, rows, sl]
            s = (lax.dot_general(q, k_all[keys, sl], NT_DIMS, preferred_element_type=F32)
                 + bias_ref[0, h, rows, keys].astype(F32))
            sx = lax.dot_general(q, kx, NT_DIMS, preferred_element_type=F32)
            m = jnp.maximum(jnp.max(s, axis=-1, keepdims=True), jnp.max(sx, axis=-1, keepdims=True))
            p = jnp.exp2(s - m)
            px = jnp.exp2(sx - m)
            den = jnp.sum(p, axis=-1, keepdims=True) + jnp.sum(px, axis=-1, keepdims=True)
            o = (jnp.dot(p.astype(BF16), v_all[keys, sl], preferred_element_type=F32)
                 + jnp.dot(px.astype(BF16), vx, preferred_element_type=F32))
            o_ref[0, rows, sl] = (o / den).astype(o_ref.dtype)


def _na_attention(qkv, qkv_c, bias_tab, n_heads):
    b, l, _ = qkv.shape
    lc = qkv_c.shape[1]
    hps = NA_HEADS_PER_STEP
    wd = hps * HEAD_DIM
    tq = NA_ROWS_PER_BLOCK * GRID_W
    th = tq // 2
    nb = l // tq
    nh = l // th
    ng = n_heads // hps
    assert nb >= 2 and l % tq == 0 and n_heads % hps == 0

    def btype(i):
        return jnp.where(i == 0, 0, jnp.where(i == nb - 1, 2, 1))

    def cur(off):
        return pl.BlockSpec((1, tq, wd), lambda h, bi, i: (bi, i, off + h))

    def prev(off):
        return pl.BlockSpec((1, th, wd), lambda h, bi, i: (bi, jnp.maximum(2 * i - 1, 0), off + h))

    def nxt(off):
        return pl.BlockSpec((1, th, wd), lambda h, bi, i: (bi, jnp.minimum(2 * i + 2, nh - 1), off + h))

    def ctx(off):
        return pl.BlockSpec((1, lc, wd), lambda h, bi, i: (bi, 0, off + h))

    ko, vo = ng, 2 * ng
    return pl.pallas_call(
        _na_kernel,
        out_shape=jax.ShapeDtypeStruct((b, l, n_heads * HEAD_DIM), BF16),
        grid=(ng, b, nb),
        in_specs=[cur(0), prev(ko), cur(ko), nxt(ko), prev(vo), cur(vo), nxt(vo), ctx(ko), ctx(vo),
                  pl.BlockSpec((1, hps, tq, 2 * tq), lambda h, bi, i: (btype(i), h, 0, 0))],
        out_specs=pl.BlockSpec((1, tq, wd), lambda h, bi, i: (bi, i, h)),
        compiler_params=_cparams(("parallel", "parallel", "arbitrary")),
        name="na_attention",
    )(qkv, qkv, qkv, qkv, qkv, qkv, qkv, qkv_c, qkv_c, bias_tab)


def _ctx_attn_kernel(q_ref, k_ref, v_ref, o_ref):
    s = lax.dot_general(q_ref[0], k_ref[0], NT_DIMS, preferred_element_type=F32)
    m = jnp.max(s, axis=-1, keepdims=True)
    p = jnp.exp2(s - m)
    den = jnp.sum(p, axis=-1, keepdims=True)
    o_ref[0] = (jnp.dot(p.astype(BF16), v_ref[0], preferred_element_type=F32) / den).astype(o_ref.dtype)


def _ctx_attention(qkv_c, n_heads):
    b, lc, _ = qkv_c.shape
    hd = HEAD_DIM

    def spec(off):
        return pl.BlockSpec((1, lc, hd), lambda bi, h: (bi, 0, off + h))

    return pl.pallas_call(
        _ctx_attn_kernel,
        out_shape=jax.ShapeDtypeStruct((b, lc, n_heads * hd), F32),
        grid=(b, n_heads),
        in_specs=[spec(0), spec(n_heads), spec(2 * n_heads)],
        out_specs=spec(0),
        compiler_params=_cparams(("parallel", "parallel")),
        name="ctx_attention",
    )(qkv_c, qkv_c, qkv_c)


def _cos_sin(n_out, n_in, period):
    ang = 2.0 * np.pi * ((np.arange(n_out)[:, None] * np.arange(n_in)[None, :]) % period) / period
    return np.cos(ang), np.sin(ang)


def _lane_cat(x3):
    return jnp.concatenate([x3[j] for j in range(x3.shape[0])], axis=1)


def _rows_from_lanes(re, im, nc):
    return jnp.concatenate(
        [jnp.concatenate([re[:, j * LANES:(j + 1) * LANES], im[:, j * LANES:(j + 1) * LANES]], axis=1)
         for j in range(nc)], axis=0)


def _lanes_from_rows(x, nc, r):
    re = jnp.concatenate([x[j * r:(j + 1) * r, :LANES] for j in range(nc)], axis=1)
    im = jnp.concatenate([x[j * r:(j + 1) * r, LANES:] for j in range(nc)], axis=1)
    return jnp.concatenate([re, im], axis=0)


def _fourier_tables(r, nc):
    l = r * LANES
    c1, s1 = _cos_sin(r, r, r)
    m1 = np.block([[c1, s1], [-s1, c1]])
    tc, ts = _cos_sin(r, LANES, l)
    c2, s2 = _cos_sin(LANES, LANES, LANES)
    m2 = np.concatenate([c2, s2], axis=0) / math.sqrt(l)
    return (_const_bf16(m1), jnp.asarray(np.tile(tc, (1, nc)), F32), jnp.asarray(np.tile(ts, (1, nc)), F32),
            _const_bf16(m2))


def _fourier_kernel(zr_ref, zi_ref, m1_ref, tc_ref, ts_ref, m2_ref, o_ref, *, nc, r):
    z = jnp.concatenate([_lane_cat(zr_ref[0]), _lane_cat(zi_ref[0])], axis=0)
    a = _dot1(m1_ref[...], z)
    ar, ai = a[:r], a[r:]
    tc, ts = tc_ref[...], ts_ref[...]
    br = (ar * tc + ai * ts).astype(BF16)
    bi = (ai * tc - ar * ts).astype(BF16)
    y = _dot1(_rows_from_lanes(br, bi, nc), m2_ref[...])
    for j in range(nc):
        o_ref[0, j] = y[j * r:(j + 1) * r].T.astype(o_ref.dtype)


def _fourier_latent(pt, n_ch, nc):
    b, _, l = pt.shape
    r = l // LANES
    p4 = pt.reshape(b, pt.shape[1], r, LANES)
    m1, tc, ts, m2 = _fourier_tables(r, nc)
    nblk = n_ch // nc

    def const(a):
        return pl.BlockSpec(a.shape, lambda bi, c: (0,) * a.ndim)

    out = pl.pallas_call(
        functools.partial(_fourier_kernel, nc=nc, r=r),
        out_shape=jax.ShapeDtypeStruct((b, n_ch, LANES, r), BF16),
        grid=(b, nblk),
        in_specs=[
            pl.BlockSpec((1, nc, r, LANES), lambda bi, c: (bi, c, 0, 0)),
            pl.BlockSpec((1, nc, r, LANES), lambda bi, c: (bi, nblk + c, 0, 0)),
            const(m1), const(tc), const(ts), const(m2),
        ],
        out_specs=pl.BlockSpec((1, nc, LANES, r), lambda bi, c: (bi, c, 0, 0)),
        compiler_params=_cparams(("parallel", "parallel")),
        name="fourier_latent",
    )(p4, p4, m1, tc, ts, m2)
    return out.reshape(b, n_ch, l)


def _filter_mlp_kernel(bands_ref, w1t_ref, w1c_ref, w1s_ref, b1_ref, w2_ref, b2_ref, fr_ref, o_ref, *, l, tl):
    pos = (pl.program_id(0) * tl + lax.broadcasted_iota(jnp.int32, (1, tl), 1)).astype(F32)
    t = pos / float(max(l - 1, 1))
    ang = bands_ref[...] * (2.0 * math.pi / l) * pos
    fr = fr_ref[...]
    cos_a, sin_a = jnp.cos(ang), jnp.sin(ang)
    w1c, w1s = w1c_ref[...], w1s_ref[...]
    pre = w1t_ref[...] * t
    for k in range(HY_BANDS):
        pre = pre + w1c[:, k:k + 1] * cos_a[k:k + 1, :] - w1s[:, k:k + 1] * sin_a[k:k + 1, :]
    kpad = jnp.zeros((w2_ref.shape[1] - w2_ref.shape[0], tl), F32)
    h1 = jnp.sin(fr * (pre + b1_ref[...]))
    o_ref[...] = jnp.sin(fr * (_dot3k(w2_ref[...], jnp.concatenate([h1, kpad], axis=0)) + b2_ref[...]))


def _flip_lanes(x, anti_eye):
    nb = x.shape[1] // LANES
    hi, lo = _hi_lo(x)
    blocks = []
    for b in range(nb):
        sl = slice((nb - 1 - b) * LANES, (nb - b) * LANES)
        blocks.append(jnp.dot(hi[:, sl], anti_eye, preferred_element_type=F32)
                      + jnp.dot(lo[:, sl], anti_eye, preferred_element_type=F32))
    return jnp.concatenate(blocks, axis=1)


def _filter_out_kernel(h_ref, nb_ref, eye_ref, w3_ref, dl_ref, o_ref, *, l, tl):
    i = pl.program_id(0)
    lane = lax.broadcasted_iota(jnp.int32, (1, tl), 1)
    n = i * tl + lane
    t = jnp.where(n < l, n, 2 * l - n).astype(F32) / float(max(l - 1, 1))
    tile = h_ref[...]
    rev = pltpu.roll(_flip_lanes(tile, eye_ref[...]), 1, 1)
    rev = jnp.where(lane == 0, nb_ref[:, 0:1], rev)
    h2 = jnp.where(i >= l // tl, rev, tile)
    kpad = jnp.zeros((w3_ref.shape[2] - h2.shape[0], tl), F32)
    out = _dot1(w3_ref[0], jnp.concatenate([h2, kpad], axis=0)) * jnp.exp(-t * dl_ref[...])
    o_ref[...] = jnp.where(n == l, 0.0, out).astype(o_ref.dtype)


def _hyena_filters(l, w1, b1, w2, b2, w3, freq, width):
    hid = w1.shape[1]
    kdim = -(-hid // LANES) * LANES
    pad = kdim - hid
    tl = min(2048, l)
    rows = HY_ORDER * width
    bands = np.linspace(1e-4, HY_BANDS - 1, HY_BANDS, dtype=np.float32).reshape(HY_BANDS, 1)
    deltas = np.abs(np.linspace(HY_MIN_DECAY, HY_MAX_DECAY, width, dtype=np.float32))
    dl = np.tile(deltas, HY_ORDER).reshape(rows, 1)
    w1t = w1.T
    w2t = jnp.pad(w2.T, ((0, 0), (0, pad)))
    w3d = w3.reshape(hid, HY_ORDER, 2, width).transpose(2, 1, 3, 0).reshape(2, rows, hid)
    w3d = jnp.pad(w3d, ((0, 0), (0, 0), (0, pad)))
    col = lambda v: v.reshape(hid, 1)

    def const(shape):
        return pl.BlockSpec(shape, lambda i: (0,) * len(shape))

    hidden = pl.pallas_call(
        functools.partial(_filter_mlp_kernel, l=l, tl=tl),
        out_shape=jax.ShapeDtypeStruct((hid, l), F32),
        grid=(l // tl,),
        in_specs=[const((HY_BANDS, 1)), const((hid, 1)), const((hid, HY_BANDS)), const((hid, HY_BANDS)),
                  const((hid, 1)), const((hid, kdim)), const((hid, 1)), const((hid, 1))],
        out_specs=pl.BlockSpec((hid, tl), lambda i: (0, i)),
        compiler_params=_cparams(("parallel",)),
        name="hyena_filter_mlp",
    )(jnp.asarray(bands), w1t[:, 0:1], w1t[:, 1:1 + HY_BANDS], w1t[:, 1 + HY_BANDS:], col(b1), w2t, col(b2),
      col(freq))
    nt = l // tl
    anti_eye = _const_bf16(np.eye(LANES)[::-1])
    return pl.pallas_call(
        functools.partial(_filter_out_kernel, l=l, tl=tl),
        out_shape=jax.ShapeDtypeStruct((rows, 2 * l), BF16),
        grid=(2 * nt,),
        in_specs=[pl.BlockSpec((hid, tl), lambda i: (0, jnp.where(i < nt, i, 2 * nt - 1 - i))),
                  pl.BlockSpec((hid, tl), lambda i: (0, jnp.clip(2 * nt - i, 0, nt - 1))),
                  const((LANES, LANES)),
                  pl.BlockSpec((1, rows, kdim), lambda i: (i // nt, 0, 0)),
                  const((rows, 1))],
        out_specs=pl.BlockSpec((rows, tl), lambda i: (0, i)),
        compiler_params=_cparams(("parallel",)),
        name="hyena_filters",
    )(hidden, hidden, anti_eye, w3d, jnp.asarray(dl))


def _conv_tables(r, nc):
    r2 = 2 * r
    n = r2 * LANES
    c1, s1 = _cos_sin(r2, r2, r2)
    c1h, s1h = c1[:, :r], s1[:, :r]
    m1 = np.block([[c1h, s1h], [-s1h, c1h]])
    m1_real = np.concatenate([c1, -s1], axis=0)
    tc, ts = _cos_sin(r2, LANES, n)
    c2, s2 = _cos_sin(LANES, LANES, LANES)
    m2 = np.block([[c2, -s2], [s2, c2]])
    m2i = np.block([[c2, s2], [-s2, c2]])
    ct, st = c1h.T, s1h.T
    m1i = np.block([[ct, -st], [st, ct]]) / n
    return dict(
        m1=_const_bf16(m1), m1_real=_const_bf16(m1_real), m2=_const_bf16(m2), m2i=_const_bf16(m2i),
        m1i=_const_bf16(m1i),
        tc_l=_const_bf16(np.tile(tc, (1, nc))), ts_l=_const_bf16(np.tile(ts, (1, nc))),
        tc_r=_const_bf16(np.tile(tc, (nc, 1))), ts_r=_const_bf16(np.tile(ts, (nc, 1))))


def _kernel_spectrum(k3, m1_real, tc_l, ts_l, m2, nc, r2):
    a = _dot1(m1_real, _lane_cat(k3)).astype(BF16)
    ar, ai = a[:r2], a[r2:]
    return _dot1(_rows_from_lanes(ar * tc_l + ai * ts_l, ai * tc_l - ar * ts_l, nc), m2).astype(BF16)


def _shift_tokens(t, r, direction):
    lane = lax.broadcasted_iota(jnp.int32, t.shape, 2)
    row = lax.broadcasted_iota(jnp.int32, t.shape, 1)
    if direction < 0:
        near = pltpu.roll(t, 1, 2)
        wrap = pltpu.roll(near, 1, 1)
        edge = lane == 0
        dead = edge & (row == 0)
    else:
        near = pltpu.roll(t, LANES - 1, 2)
        wrap = pltpu.roll(near, r - 1, 1)
        edge = lane == LANES - 1
        dead = edge & (row == r - 1)
    return jnp.where(dead, 0.0, jnp.where(edge, wrap, near))


def _long_conv(zr, zi, kspec, m1, m2, m2i, m1i, tc_l, ts_l, tc_r, ts_r, nc, r):
    r2 = 2 * r
    a = _dot1(m1, jnp.concatenate([zr.astype(BF16), zi.astype(BF16)], axis=0)).astype(BF16)
    ar, ai = a[:r2], a[r2:]
    x = _dot1(_rows_from_lanes(ar * tc_l + ai * ts_l, ai * tc_l - ar * ts_l, nc), m2).astype(BF16)
    xr, xi = x[:, :LANES], x[:, LANES:]
    kr, ki = kspec[:, :LANES], kspec[:, LANES:]
    bb = _dot1(jnp.concatenate([xr * kr - xi * ki, xr * ki + xi * kr], axis=1), m2i).astype(BF16)
    pr, pi = bb[:, :LANES], bb[:, LANES:]
    q = jnp.concatenate([pr * tc_r - pi * ts_r, pi * tc_r + pr * ts_r], axis=1)
    out = _dot1(m1i, _lanes_from_rows(q, nc, r2))
    return out[:r], out[r:]


def _hyena_kernel(v0, v1, a0, a1, b0, b1, k0_ref, k1_ref, cw_ref, cb_ref, bias_ref,
                  m1_ref, m2_ref, m2i_ref, m1i_ref, tcl_ref, tsl_ref, tcr_ref, tsr_ref, m1r_ref, o_ref, *, nc, r):
    tabs = (m1_ref[...], m2_ref[...], m2i_ref[...], m1i_ref[...],
            tcl_ref[...], tsl_ref[...], tcr_ref[...], tsr_ref[...])

    def spectrum(k_ref):
        return _kernel_spectrum(k_ref[...], m1r_ref[...], tcl_ref[...], tsl_ref[...], m2_ref[...], nc, 2 * r)

    def short_conv(ref, part):
        t = ref[0].astype(F32)
        w = cw_ref[part]
        return (_lane_cat(_shift_tokens(t, r, -1)) * w[0:1] + _lane_cat(t) * w[1:2]
                + _lane_cat(_shift_tokens(t, r, +1)) * w[2:3] + cb_ref[part])

    vr, vi = short_conv(v0, 0), short_conv(v1, 0)
    x1r, x1i = short_conv(a0, 1), short_conv(a1, 1)
    x2r, x2i = short_conv(b0, 2), short_conv(b1, 2)
    bias = bias_ref[...]
    yr, yi = _long_conv(vr, vi, spectrum(k0_ref), *tabs, nc, r)
    zr = x1r * (yr + vr * bias[0:1])
    zi = x1i * (yi + vi * bias[0:1])
    yr, yi = _long_conv(zr, zi, spectrum(k1_ref), *tabs, nc, r)
    outr = x2r * (yr + zr * bias[1:2])
    outi = x2i * (yi + zi * bias[1:2])
    for j in range(nc):
        o_ref[0, j] = outr[:, j * LANES:(j + 1) * LANES].astype(o_ref.dtype)
        o_ref[1, j] = outi[:, j * LANES:(j + 1) * LANES].astype(o_ref.dtype)


def _hyena_latent(pt, ch0, width, k2t, conv_w, conv_b, bias, tabs, nc):
    b, c_all, l = pt.shape
    assert b == 2, "the two batch entries ride one complex transform"
    r = l // LANES
    r2 = 2 * r
    p4 = pt.reshape(b, c_all, r, LANES)
    k3 = k2t.reshape(k2t.shape[0], r2, LANES)
    nblk = width // nc
    rep = lambda a: jnp.repeat(a, LANES, axis=-1)
    cw = rep(conv_w.reshape(3, 3, width).transpose(1, 0, 2))
    cb = rep(conv_b.reshape(3, 1, width))
    bs = rep(bias)

    def inp(bi, part):
        off = (ch0 + part * width) // nc
        return pl.BlockSpec((1, nc, r, LANES), lambda c: (bi, off + c, 0, 0))

    def const(a):
        return pl.BlockSpec(a.shape, lambda c: (0,) * a.ndim)

    names = ("m1", "m2", "m2i", "m1i", "tc_l", "ts_l", "tc_r", "ts_r", "m1_real")
    consts = [tabs[k] for k in names]
    out = pl.pallas_call(
        functools.partial(_hyena_kernel, nc=nc, r=r),
        out_shape=jax.ShapeDtypeStruct((b, width, r, LANES), BF16),
        grid=(nblk,),
        in_specs=[inp(0, 0), inp(1, 0), inp(0, 1), inp(1, 1), inp(0, 2), inp(1, 2),
                  pl.BlockSpec((nc, r2, LANES), lambda c: (c, 0, 0)),
                  pl.BlockSpec((nc, r2, LANES), lambda c: (nblk + c, 0, 0)),
                  pl.BlockSpec((3, 3, nc * LANES), lambda c: (0, 0, c)),
                  pl.BlockSpec((3, 1, nc * LANES), lambda c: (0, 0, c)),
                  pl.BlockSpec((HY_ORDER, nc * LANES), lambda c: (0, c))]
                 + [const(a) for a in consts],
        out_specs=pl.BlockSpec((b, nc, r, LANES), lambda c: (0, c, 0, 0)),
        compiler_params=_cparams(("parallel",)),
        name="hyena_latent",
    )(p4, p4, p4, p4, p4, p4, k3, k3, cw, cb, bs, *consts)
    return out.reshape(b, width, l)


def _ctx_fourier_kernel(zr_ref, zi_ref, m_ref, o_ref):
    z = jnp.concatenate([zr_ref[0], zi_ref[0]], axis=1).astype(F32)
    o_ref[0] = _dot_data_const(z, m_ref[...])


def _ctx_fourier(ptc, n_ch):
    b, _, lc = ptc.shape
    c, s = _cos_sin(lc, lc, lc)
    m = _const_rhs3(np.concatenate([c, s], axis=0) / math.sqrt(lc))
    return pl.pallas_call(
        _ctx_fourier_kernel,
        out_shape=jax.ShapeDtypeStruct((b, n_ch, lc), F32),
        grid=(b,),
        in_specs=[pl.BlockSpec((1, n_ch, lc), lambda bi: (bi, 0, 0)),
                  pl.BlockSpec((1, n_ch, lc), lambda bi: (bi, 1, 0)),
                  pl.BlockSpec(m.shape, lambda bi: (0, 0))],
        out_specs=pl.BlockSpec((1, n_ch, lc), lambda bi: (bi, 0, 0)),
        compiler_params=_cparams(("parallel",)),
        name="ctx_fourier",
    )(ptc, ptc, m)


def _ctx_hyena_kernel(v0, v1, a0, a1, b0, b1, k_ref, cw_ref, cb_ref, bias_ref, mk_ref, mf_ref, mi_ref,
                      o_ref, *, lc, width):
    def short_conv(ref, part):
        t = ref[0].astype(F32)
        lane = lax.broadcasted_iota(jnp.int32, t.shape, 1)
        prv = jnp.where(lane == 0, 0.0, pltpu.roll(t, 1, 1))
        nxt = jnp.where(lane == lc - 1, 0.0, pltpu.roll(t, lc - 1, 1))
        w = cw_ref[part]
        return prv * w[:, 0:1] + t * w[:, 1:2] + nxt * w[:, 2:3] + cb_ref[part]

    kspec = _dot_data_const(k_ref[...].astype(F32), mk_ref[...])
    n = 2 * lc

    def long_conv(zr, zi, ks):
        x = _dot_data_const(jnp.concatenate([zr, zi], axis=1), mf_ref[...])
        xr, xi = x[:, :n], x[:, n:]
        kr, ki = ks[:, :n], ks[:, n:]
        y = jnp.concatenate([xr * kr - xi * ki, xr * ki + xi * kr], axis=1)
        out = _dot_data_const(y, mi_ref[...])
        return out[:, :lc], out[:, lc:]

    vr, vi = short_conv(v0, 0), short_conv(v1, 0)
    x1r, x1i = short_conv(a0, 1), short_conv(a1, 1)
    x2r, x2i = short_conv(b0, 2), short_conv(b1, 2)
    bias = bias_ref[...]
    yr, yi = long_conv(vr, vi, kspec[:width])
    zr = x1r * (yr + vr * bias[:, 0:1])
    zi = x1i * (yi + vi * bias[:, 0:1])
    yr, yi = long_conv(zr, zi, kspec[width:])
    o_ref[0] = x2r * (yr + zr * bias[:, 1:2])
    o_ref[1] = x2i * (yi + zi * bias[:, 1:2])


def _ctx_hyena(ptc, ch0, width, k2t, conv_w, conv_b, bias):
    b, _, lc = ptc.shape
    assert b == 2
    n = 2 * lc
    c, s = _cos_sin(n, n, n)
    mk = _const_rhs3(np.concatenate([c, -s], axis=1))
    ch, sh = c[:lc], s[:lc]
    mf = _const_rhs3(np.block([[ch, -sh], [sh, ch]]))
    ci, si = c[:, :lc], s[:, :lc]
    mi = _const_rhs3(np.block([[ci, si], [-si, ci]]) / n)
    cw = conv_w.reshape(3, 3, width).transpose(1, 2, 0)
    cb = conv_b.reshape(3, width, 1)

    def inp(bi, part):
        return pl.BlockSpec((1, width, lc), lambda i: (bi, ch0 // width + part, 0))

    def const(a):
        return pl.BlockSpec(a.shape, lambda i: (0,) * a.ndim)

    args = (k2t, cw, cb, bias.T, mk, mf, mi)
    return pl.pallas_call(
        functools.partial(_ctx_hyena_kernel, lc=lc, width=width),
        out_shape=jax.ShapeDtypeStruct((b, width, lc), F32),
        grid=(1,),
        in_specs=[inp(0, 0), inp(1, 0), inp(0, 1), inp(1, 1), inp(0, 2), inp(1, 2)] + [const(a) for a in args],
        out_specs=pl.BlockSpec((b, width, lc), lambda i: (0, 0, 0)),
        compiler_params=_cparams(("arbitrary",)),
        name="ctx_hyena",
    )(ptc, ptc, ptc, ptc, ptc, ptc, *args)


def _merge_kernel(a_ref, f_ref, hy_ref, x_ref, gate_ref, ga_ref, gf_ref, gh_ref, wa_ref, wf_ref, wh_ref, o_ref):
    a = a_ref[0].astype(F32)
    ya = a * lax.rsqrt(jnp.mean(a * a, axis=-1, keepdims=True) + EPS) * ga_ref[...]
    acc = jnp.dot(ya.astype(BF16), wa_ref[...], preferred_element_type=F32)

    def cm_part(ref, g_ref, w_ref):
        t = ref[0].astype(F32)
        y = t * lax.rsqrt(jnp.mean(t * t, axis=0, keepdims=True) + EPS) * g_ref[...]
        return jnp.dot(y.T.astype(BF16), w_ref[...], preferred_element_type=F32)

    acc = acc + cm_part(f_ref, gf_ref, wf_ref) + cm_part(hy_ref, gh_ref, wh_ref)
    o_ref[0] = x_ref[0] + gate_ref[0] * acc


def _merge_out(a, ft, ht, x, gate, g, w_all, layer, tm):
    b, l, d = x.shape
    wa_n, wf_n, wh_n = a.shape[2], ft.shape[1], ht.shape[1]
    assert wa_n % wf_n == 0 and wf_n == wh_n
    tm = min(tm, l)
    ga = g[:wa_n].reshape(1, wa_n)
    gf = g[wa_n:wa_n + wf_n].reshape(wf_n, 1)
    gh = g[wa_n + wf_n:].reshape(wh_n, 1)

    def const(arr):
        return pl.BlockSpec(arr.shape, lambda bi, i: (0,) * arr.ndim)

    def w_rows(n, blk):
        return pl.BlockSpec((None, n, d), lambda bi, i: (layer, blk, 0))

    return pl.pallas_call(
        _merge_kernel,
        out_shape=jax.ShapeDtypeStruct((b, l, d), F32),
        grid=(b, l // tm),
        in_specs=[pl.BlockSpec((1, tm, wa_n), lambda bi, i: (bi, i, 0)),
                  pl.BlockSpec((1, wf_n, tm), lambda bi, i: (bi, 0, i)),
                  pl.BlockSpec((1, wh_n, tm), lambda bi, i: (bi, 0, i)),
                  pl.BlockSpec((1, tm, d), lambda bi, i: (bi, i, 0)),
                  pl.BlockSpec((1, 1, d), lambda bi, i: (bi, 0, 0)),
                  const(ga), const(gf), const(gh),
                  w_rows(wa_n, 0), w_rows(wf_n, wa_n // wf_n), w_rows(wh_n, wa_n // wf_n + 1)],
        out_specs=pl.BlockSpec((1, tm, d), lambda bi, i: (bi, i, 0)),
        compiler_params=_cparams(("parallel", "parallel")),
        name="merge_out",
    )(a, ft, ht, x, gate.reshape(b, 1, d), ga, gf, gh, w_all, w_all, w_all)


HALO = 16


def _ffn_kernel(xp_ref, x_ref, xn_ref, g_ref, sh_ref, sc_ref, gate_ref, wg_ref, wu_ref, cw_ref, cb_ref, wd_ref,
                fg_ref, o_ref, h_ref, *, tm, final_norm):
    i = pl.program_id(1)
    c = pl.program_id(2)
    last_tile = pl.num_programs(1) - 1

    n_ext = tm + 2 * HALO

    def down_partial(ge, up):
        row = lax.broadcasted_iota(jnp.int32, ge.shape, 0)
        outside = ((row < HALO) & (i == 0)) | ((row >= HALO + tm) & (i == last_tile))
        ge = jnp.where(outside, 0.0, ge)
        gp = pltpu.roll(ge, 1, 0)[HALO:HALO + tm]
        gn = pltpu.roll(ge, n_ext - 1, 0)[HALO:HALO + tm]
        cw = cw_ref[...]
        conv = gp * cw[0:1] + ge[HALO:HALO + tm] * cw[1:2] + gn * cw[2:3] + cb_ref[...]
        inner = 0.7978845608028654 * (conv + 0.044715 * (conv * conv * conv))
        act = 0.5 * conv * (1.0 + jnp.tanh(inner)) * up
        return jnp.dot(act.astype(BF16), wd_ref[...], preferred_element_type=F32)

    @pl.when(c == 0)
    def _():
        g, sh, sc = g_ref[...], sh_ref[0], sc_ref[0]
        wg, wu = wg_ref[...], wu_ref[...]
        rs = tm // NORM_SLABS
        bounds = [(0, HALO)] + [(HALO + s * rs, HALO + (s + 1) * rs) for s in range(NORM_SLABS)] + [(HALO + tm, n_ext)]
        ge_parts, up_parts = [], []
        for lo, hi in bounds:
            if lo == 0:
                xs = xp_ref[0, 0]
            elif hi == n_ext:
                xs = xn_ref[0, 0]
            else:
                xs = x_ref[0, lo - HALO:hi - HALO, :]
            h = _norm_mod(xs, g, sh, sc).astype(BF16)
            h_ref[lo:hi, :] = h
            ge_parts.append(jnp.dot(h, wg, preferred_element_type=F32))
            if lo != 0 and hi != n_ext:
                up_parts.append(jnp.dot(h, wu, preferred_element_type=F32))
        o_ref[0] = down_partial(jnp.concatenate(ge_parts, axis=0), jnp.concatenate(up_parts, axis=0))

    @pl.when(c > 0)
    def _():
        h = h_ref[...]
        ge = jnp.dot(h, wg_ref[...], preferred_element_type=F32)
        up = jnp.dot(h[HALO:HALO + tm], wu_ref[...], preferred_element_type=F32)
        o_ref[0] += down_partial(ge, up)

    @pl.when(c == pl.num_programs(2) - 1)
    def _():
        y = x_ref[0] + gate_ref[0] * o_ref[0]
        if final_norm:
            y = y * lax.rsqrt(jnp.mean(y * y, axis=-1, keepdims=True) + EPS) * fg_ref[...]
        o_ref[0] = y


def _ffn(x, g, sh, sc, gate, w_up, conv_w, conv_b, w_down, layer, final_g, tm, tf, *, final_norm):
    b, l, d = x.shape
    dff = w_down.shape[1]
    tm = min(tm, l)
    nch = dff // tf
    x4 = x.reshape(b, l // HALO, HALO, d)
    per_tile = tm // HALO
    nhalo = l // HALO
    vec = pl.BlockSpec((1, 1, d), lambda bi, i, c: (bi, 0, 0))
    return pl.pallas_call(
        functools.partial(_ffn_kernel, tm=tm, final_norm=final_norm),
        out_shape=jax.ShapeDtypeStruct((b, l, d), F32),
        grid=(b, l // tm, nch),
        in_specs=[
            pl.BlockSpec((1, 1, HALO, d), lambda bi, i, c: (bi, jnp.maximum(i * per_tile - 1, 0), 0, 0)),
            pl.BlockSpec((1, tm, d), lambda bi, i, c: (bi, i, 0)),
            pl.BlockSpec((1, 1, HALO, d), lambda bi, i, c: (bi, jnp.minimum((i + 1) * per_tile, nhalo - 1), 0, 0)),
            pl.BlockSpec((1, d), lambda bi, i, c: (0, 0)),
            vec, vec, vec,
            pl.BlockSpec((None, d, tf), lambda bi, i, c: (layer, 0, c)),
            pl.BlockSpec((None, d, tf), lambda bi, i, c: (layer, 0, nch + c)),
            pl.BlockSpec((3, tf), lambda bi, i, c: (0, c)),
            pl.BlockSpec((1, tf), lambda bi, i, c: (0, c)),
            pl.BlockSpec((None, tf, d), lambda bi, i, c: (layer, c, 0)),
            pl.BlockSpec((1, d), lambda bi, i, c: (0, 0)),
        ],
        out_specs=pl.BlockSpec((1, tm, d), lambda bi, i, c: (bi, i, 0)),
        scratch_shapes=[pltpu.VMEM((tm + 2 * HALO, d), BF16)],
        compiler_params=_cparams(("parallel", "parallel", "arbitrary")),
        name="ffn",
    )(x4, x, x4, g.reshape(1, d), sh.reshape(b, 1, d), sc.reshape(b, 1, d), gate.reshape(b, 1, d),
      w_up, w_up, conv_w, conv_b.reshape(1, dff), w_down, final_g.reshape(1, d))


TOKEN_TILE = 512
PROJ_TOKEN_TILE = 1024
PROJ_TOK_COL_TILE = 1024
PROJ_CM_COL_TILE = 512
FFN_TOKEN_TILE = 1024
FFN_COL_TILE = 512
MIX_CHANNELS = 8
FOURIER_CHANNELS = 32


def kernel(x, c, ctx, c_ctx, ada_w, ada_b, norm1_g, norm2_g, w_in, na_rpb, hy_conv_w, hy_conv_b, hy_w1, hy_b1,
           hy_w2, hy_b2, hy_w3, hy_freq, hy_bias, mix_norm_g, w_out, ffn_w_up, ffn_conv_w, ffn_conv_b, ffn_w_down,
           final_norm_g):
    b, l, d = x.shape
    lc = ctx.shape[1]
    depth = ada_w.shape[0]
    na_w, fn_w = d // 2, d // 4
    hy_w = d - na_w - fn_w
    n_heads = na_w // HEAD_DIM
    qkv_w = 3 * na_w
    rows = l // GRID_W
    assert b + 1 <= 8 and l % (NA_ROWS_PER_BLOCK * GRID_W) == 0 and rows >= 2 * NA_ROWS_PER_BLOCK
    r = l // LANES
    nc = MIX_CHANNELS
    conv_tabs = _conv_tables(r, nc)

    s_in = jnp.concatenate([c, c_ctx[None], jnp.zeros((8 - b - 1, d), F32)], axis=0)
    mod_all = _ada_mod(s_in, ada_w, ada_b)
    xc = ctx
    w_out_bf, w_up_bf, w_down_bf = w_out.astype(BF16), ffn_w_up.astype(BF16), ffn_w_down.astype(BF16)
    for layer in range(depth):
        update_ctx = layer < depth - 1
        mod = mod_all[layer]
        sh1, sc1, g1, sh2, sc2, g2 = jnp.split(mod[:b], N_MOD, axis=-1)
        csh1, csc1, cg1, csh2, csc2, cg2 = jnp.split(jnp.broadcast_to(mod[b:b + 1], (b, N_MOD * d)), N_MOD, axis=-1)

        wl = w_in[layer]
        w_fold = _fold_fourier_weights(wl[:, qkv_w:qkv_w + fn_w])
        w_tok = jnp.concatenate([wl[:, :na_w] * (HEAD_DIM ** -0.5 * LOG2E), wl[:, na_w:qkv_w]], axis=1).astype(BF16)
        w_cm = jnp.concatenate([w_fold[0], w_fold[1], wl[:, qkv_w + fn_w:]], axis=1).astype(BF16)
        hy0 = 2 * fn_w

        tiles = dict(tm=PROJ_TOKEN_TILE, tn_tok=PROJ_TOK_COL_TILE, tn_cm=PROJ_CM_COL_TILE)
        qkv, pt = _proj(x, norm1_g[layer], sh1, sc1, w_tok, w_cm, **tiles)
        qkv_c, ptc = _proj(xc, norm1_g[layer], csh1, csc1, w_tok, w_cm, **tiles)

        a = _na_attention(qkv, qkv_c, _na_bias_table(na_rpb[layer], rows), n_heads)
        yf = _fourier_latent(pt, fn_w, FOURIER_CHANNELS)
        filt = (hy_w1[layer], hy_b1[layer], hy_w2[layer], hy_b2[layer], hy_w3[layer], hy_freq[layer])
        yh = _hyena_latent(pt, hy0, hy_w, _hyena_filters(l, *filt, hy_w), hy_conv_w[layer], hy_conv_b[layer],
                           hy_bias[layer], conv_tabs, nc)
        x_new = _merge_out(a, yf, yh, x, g1, mix_norm_g[layer], w_out_bf, layer, TOKEN_TILE)

        if update_ctx:
            ac = _ctx_attention(qkv_c, n_heads)
            yfc = _ctx_fourier(ptc, fn_w)
            yhc = _ctx_hyena(ptc, hy0, hy_w, _hyena_filters(lc, *filt, hy_w), hy_conv_w[layer], hy_conv_b[layer],
                             hy_bias[layer])
            xc = _merge_out(ac, yfc, yhc, xc, cg1, mix_norm_g[layer], w_out_bf, layer, TOKEN_TILE)
        x = x_new

        x = _ffn(x, norm2_g[layer], sh2, sc2, g2, w_up_bf, ffn_conv_w[layer], ffn_conv_b[layer], w_down_bf, layer,
                 final_norm_g, FFN_TOKEN_TILE, FFN_COL_TILE, final_norm=not update_ctx)
        if update_ctx:
            xc = _ffn(xc, norm2_g[layer], csh2, csc2, cg2, w_up_bf, ffn_conv_w[layer], ffn_conv_b[layer], w_down_bf,
                      layer, final_norm_g, FFN_TOKEN_TILE, FFN_COL_TILE, final_norm=False)
    return x
```

```python
import functools
import math

import numpy as np
import jax
import jax.numpy as jnp
from jax import lax
from jax.experimental import pallas as pl
from jax.experimental.pallas import tpu as pltpu

F32 = jnp.float32
BF16 = jnp.bfloat16

EPS = 1e-6
HEAD_DIM = 128
GRID_W = 64
NA_KH = 8
NA_KW = 16
NA_ROWS_PER_BLOCK = 8
FN_GROUP_DIM = 128
HY_ORDER = 2
HY_BANDS = 16
HY_FAST_DECAY = 0.3
HY_SLOW_DECAY = 1.5
HY_TARGET = 1e-2
HY_MIN_DECAY = math.log(HY_TARGET) / HY_SLOW_DECAY
HY_MAX_DECAY = math.log(HY_TARGET) / HY_FAST_DECAY
N_MOD = 6
LANES = 128
NEG_INF = -1e30
LOG2E = math.log2(math.e)
VMEM_LIMIT = 60 * 1024 * 1024

NT_DIMS = (((1,), (1,)), ((), ()))


def _cparams(sem):
    return pltpu.CompilerParams(dimension_semantics=sem, vmem_limit_bytes=VMEM_LIMIT)


def _hi_lo(a):
    hi = a.astype(BF16)
    lo = (a - hi.astype(F32)).astype(BF16)
    return hi, lo


def _np_hi_lo(m):
    m = np.asarray(m, np.float32)
    hi = m.astype(BF16)
    lo = (m - hi.astype(np.float32)).astype(BF16)
    return hi, lo


def _const_rhs3(m):
    hi, lo = _np_hi_lo(m)
    return jnp.asarray(np.concatenate([hi, hi, lo], axis=0))


def _dot_data_const(a, c3):
    hi, lo = _hi_lo(a)
    return jnp.dot(jnp.concatenate([hi, lo, hi], axis=1), c3, preferred_element_type=F32)


def _dot3k(a, b):
    ah, al = _hi_lo(a)
    bh, bl = _hi_lo(b)
    return jnp.dot(jnp.concatenate([ah, al, ah], axis=1), jnp.concatenate([bh, bh, bl], axis=0),
                   preferred_element_type=F32)


def _const_bf16(m):
    return jnp.asarray(np.asarray(m, np.float32).astype(BF16))


def _dot1(a, b):
    return jnp.dot(a.astype(BF16), b.astype(BF16), preferred_element_type=F32)


def _ada_kernel(s_ref, w_ref, b_ref, o_ref):
    s = s_ref[...]
    s = s / (1.0 + jnp.exp(-s))
    o_ref[0] = _dot1(s, w_ref[0]) + b_ref[0]


def _ada_mod(s_in, ada_w, ada_b):
    depth, d, n = ada_w.shape
    tn = 1024
    return pl.pallas_call(
        _ada_kernel,
        out_shape=jax.ShapeDtypeStruct((depth, 8, n), F32),
        grid=(depth, n // tn),
        in_specs=[
            pl.BlockSpec((8, d), lambda l, j: (0, 0)),
            pl.BlockSpec((1, d, tn), lambda l, j: (l, 0, j)),
            pl.BlockSpec((1, 1, tn), lambda l, j: (l, 0, j)),
        ],
        out_specs=pl.BlockSpec((1, 8, tn), lambda l, j: (l, 0, j)),
        compiler_params=_cparams(("parallel", "parallel")),
        name="ada_mod",
    )(s_in, ada_w, ada_b.reshape(depth, 1, n))


def _norm_mod(x, g, sh, sc):
    ms = jnp.mean(x * x, axis=-1, keepdims=True)
    return (x * lax.rsqrt(ms + EPS) * g) * (1.0 + sc) + sh


NORM_SLABS = 4


def _proj_kernel(*refs, n_tm):
    x_refs = refs[:NORM_SLABS]
    g_ref, sh_ref, sc_ref, wa_ref, wb_ref, otm_ref, ocm_ref, h_ref = refs[NORM_SLABS:]
    j = pl.program_id(2)
    rs = x_refs[0].shape[1]

    @pl.when(j == 0)
    def _():
        g, sh, sc = g_ref[...], sh_ref[0], sc_ref[0]
        w = wa_ref[...]
        for s in range(NORM_SLABS):
            h = _norm_mod(x_refs[s][0], g, sh, sc).astype(BF16)
            h_ref[s * rs:(s + 1) * rs, :] = h
            otm_ref[0, s * rs:(s + 1) * rs, :] = jnp.dot(h, w, preferred_element_type=F32).astype(otm_ref.dtype)

    @pl.when((j > 0) & (j < n_tm))
    def _():
        otm_ref[0] = jnp.dot(h_ref[...], wa_ref[...], preferred_element_type=F32).astype(otm_ref.dtype)

    @pl.when(j >= n_tm)
    def _():
        y = jnp.dot(h_ref[...], wb_ref[...], preferred_element_type=F32)
        ocm_ref[0] = y.T.astype(ocm_ref.dtype)


def _proj(x, g, sh, sc, w_tok, w_cm, *, tm, tn_tok, tn_cm):
    b, l, d = x.shape
    n_tok, n_ch = w_tok.shape[1], w_cm.shape[1]
    tm = min(tm, l)
    n_tm, n_cm = n_tok // tn_tok, n_ch // tn_cm
    assert n_tm * tn_tok == n_tok and n_cm * tn_cm == n_ch and tm % (16 * NORM_SLABS) == 0
    vec = pl.BlockSpec((1, 1, d), lambda bi, i, j: (bi, 0, 0))
    rs = tm // NORM_SLABS
    nt, nj = l // tm, n_tm + n_cm
    n_slabs = b * l // rs
    assert nj > NORM_SLABS

    def slab(s):
        def index(bi, i, j):
            tile = bi * nt + i + (j >= nj - s).astype(jnp.int32)
            return (jnp.minimum(tile * NORM_SLABS + s, n_slabs - 1), 0, 0)
        return pl.BlockSpec((1, rs, d), index)

    tok_j = lambda j: jnp.minimum(j, n_tm - 1)
    cm_j = lambda j: jnp.maximum(j - n_tm, 0)
    return pl.pallas_call(
        functools.partial(_proj_kernel, n_tm=n_tm),
        out_shape=(jax.ShapeDtypeStruct((b, l, n_tok), BF16), jax.ShapeDtypeStruct((b, n_ch, l), BF16)),
        grid=(b, nt, nj),
        in_specs=[slab(s) for s in range(NORM_SLABS)] + [
            pl.BlockSpec((1, d), lambda bi, i, j: (0, 0)),
            vec, vec,
            pl.BlockSpec((d, tn_tok), lambda bi, i, j: (0, tok_j(j))),
            pl.BlockSpec((d, tn_cm), lambda bi, i, j: (0, cm_j(j))),
        ],
        out_specs=(pl.BlockSpec((1, tm, tn_tok), lambda bi, i, j: (bi, i, tok_j(j))),
                   pl.BlockSpec((1, tn_cm, tm), lambda bi, i, j: (bi, cm_j(j), i))),
        scratch_shapes=[pltpu.VMEM((tm, d), BF16)],
        compiler_params=_cparams(("parallel", "parallel", "arbitrary")),
        name="proj_in",
    )(*([x.reshape(n_slabs, rs, d)] * NORM_SLABS), g.reshape(1, d), sh.reshape(b, 1, d), sc.reshape(b, 1, d),
      w_tok, w_cm)


def _fold_kernel(w_ref, m_ref, o_ref):
    o_ref[0] = _dot_data_const(w_ref[...], m_ref[...])


def _fold_fourier_weights(w_f):
    d, width = w_f.shape
    gd = FN_GROUP_DIM
    idx = np.arange(gd)
    ang = 2.0 * np.pi * np.outer(idx, idx) / gd
    mat = np.concatenate([np.cos(ang), -np.sin(ang)], axis=1) / math.sqrt(gd)
    groups = width // gd
    return pl.pallas_call(
        _fold_kernel,
        out_shape=jax.ShapeDtypeStruct((2, d, width), F32),
        grid=(groups, 2),
        in_specs=[
            pl.BlockSpec((d, gd), lambda g, p: (0, g)),
            pl.BlockSpec((3 * gd, gd), lambda g, p: (0, p)),
        ],
        out_specs=pl.BlockSpec((1, d, gd), lambda g, p: (p, 0, g)),
        compiler_params=_cparams(("parallel", "parallel")),
        name="fold_fourier",
    )(w_f, _const_rhs3(mat))


def _na_bias_table(rpb, rows):
    h = rpb.shape[0]
    w, kw, kh, rb = GRID_W, NA_KW, NA_KH, NA_ROWS_PER_BLOCK
    col = np.arange(w)
    cs = np.clip(col - kw // 2, 0, w - kw)
    kc = np.arange(w)[None, :]
    col_ok = (kc >= cs[:, None]) & (kc < cs[:, None] + kw)
    padded = jnp.pad((rpb * LOG2E).astype(BF16), ((0, 0), (0, 0), (w - kw, w - kw)))
    t1 = jnp.stack([padded[:, :, w - 1 - q:2 * w - 1 - q] for q in range(w)], axis=1)
    t1 = jnp.where(col_ok[None, :, None, :], t1, NEG_INF)
    slots = 2 * rb
    blocks = []
    for t, row0 in enumerate((0, rb, rows - rb)):
        per_row = []
        for qr in range(rb):
            r = row0 + qr
            r0 = r - kh // 2 if t == 1 else min(max(r - kh // 2, 0), rows - kh)
            valid = [kr for kr in range(slots)
                     if r0 <= row0 - rb // 2 + kr < r0 + kh and (t == 1 or 0 <= row0 - rb // 2 + kr < rows)]
            lo, hi = valid[0], valid[-1] + 1
            d0 = (row0 - rb // 2 + lo) - r + (kh - 1)
            live = [t1[:, :, d0 + k, :] for k in range(hi - lo)]
            dead = jnp.full((h, w, w), NEG_INF, t1.dtype)
            per_row.append(jnp.concatenate([dead] * lo + live + [dead] * (slots - hi), axis=-1))
        blocks.append(jnp.stack(per_row, axis=1))
    return jnp.stack(blocks, axis=0).reshape(3, h, rb * w, slots * w)


NA_HEADS_PER_STEP = 8


def _na_kernel(q_ref, kp_ref, kc_ref, kn_ref, vp_ref, vc_ref, vn_ref, kx_ref, vx_ref, bias_ref, o_ref):
    k_all = jnp.concatenate([kp_ref[0], kc_ref[0], kn_ref[0]], axis=0)
    v_all = jnp.concatenate([vp_ref[0], vc_ref[0], vn_ref[0]], axis=0)
    hq = q_ref.shape[1] // 2
    band = (NA_ROWS_PER_BLOCK // 2 + NA_KH) * GRID_W
    for h in range(NA_HEADS_PER_STEP):
        sl = slice(h * HEAD_DIM, (h + 1) * HEAD_DIM)
        kx, vx = kx_ref[0, :, sl], vx_ref[0, :, sl]
        for part in range(2):
            rows = slice(part * hq, (part + 1) * hq)
            keys = slice(part * hq, part * hq + band)
            q = q_ref[0, rows, sl]
            s = (lax.dot_general(q, k_all[keys, sl], NT_DIMS, preferred_element_type=F32)
                 + bias_ref[0, h, rows, keys].astype(F32))
            sx = lax.dot_general(q, kx, NT_DIMS, preferred_element_type=F32)
            m = jnp.maximum(jnp.max(s, axis=-1, keepdims=True), jnp.max(sx, axis=-1, keepdims=True))
            p = jnp.exp2(s - m)
            px = jnp.exp2(sx - m)
            den = jnp.sum(p, axis=-1, keepdims=True) + jnp.sum(px, axis=-1, keepdims=True)
            o = (jnp.dot(p.astype(BF16), v_all[keys, sl], preferred_element_type=F32)
                 + jnp.dot(px.astype(BF16), vx, preferred_element_type=F32))
            o_ref[0, rows, sl] = (o / den).astype(o_ref.dtype)


def _na_attention(qkv, qkv_c, bias_tab, n_heads):
    b, l, _ = qkv.shape
    lc = qkv_c.shape[1]
    hps = NA_HEADS_PER_STEP
    wd = hps * HEAD_DIM
    tq = NA_ROWS_PER_BLOCK * GRID_W
    th = tq // 2
    nb = l // tq
    nh = l // th
    ng = n_heads // hps
    assert nb >= 2 and l % tq == 0 and n_heads % hps == 0

    def btype(i):
        return jnp.where(i == 0, 0, jnp.where(i == nb - 1, 2, 1))

    def cur(off):
        return pl.BlockSpec((1, tq, wd), lambda h, bi, i: (bi, i, off + h))

    def prev(off):
        return pl.BlockSpec((1, th, wd), lambda h, bi, i: (bi, jnp.maximum(2 * i - 1, 0), off + h))

    def nxt(off):
        return pl.BlockSpec((1, th, wd), lambda h, bi, i: (bi, jnp.minimum(2 * i + 2, nh - 1), off + h))

    def ctx(off):
        return pl.BlockSpec((1, lc, wd), lambda h, bi, i: (bi, 0, off + h))

    ko, vo = ng, 2 * ng
    return pl.pallas_call(
        _na_kernel,
        out_shape=jax.ShapeDtypeStruct((b, l, n_heads * HEAD_DIM), BF16),
        grid=(ng, b, nb),
        in_specs=[cur(0), prev(ko), cur(ko), nxt(ko), prev(vo), cur(vo), nxt(vo), ctx(ko), ctx(vo),
                  pl.BlockSpec((1, hps, tq, 2 * tq), lambda h, bi, i: (btype(i), h, 0, 0))],
        out_specs=pl.BlockSpec((1, tq, wd), lambda h, bi, i: (bi, i, h)),
        compiler_params=_cparams(("parallel", "parallel", "arbitrary")),
        name="na_attention",
    )(qkv, qkv, qkv, qkv, qkv, qkv, qkv, qkv_c, qkv_c, bias_tab)


def _ctx_attn_kernel(q_ref, k_ref, v_ref, o_ref):
    s = lax.dot_general(q_ref[0], k_ref[0], NT_DIMS, preferred_element_type=F32)
    m = jnp.max(s, axis=-1, keepdims=True)
    p = jnp.exp2(s - m)
    den = jnp.sum(p, axis=-1, keepdims=True)
    o_ref[0] = (jnp.dot(p.astype(BF16), v_ref[0], preferred_element_type=F32) / den).astype(o_ref.dtype)


def _ctx_attention(qkv_c, n_heads):
    b, lc, _ = qkv_c.shape
    hd = HEAD_DIM

    def spec(off):
        return pl.BlockSpec((1, lc, hd), lambda bi, h: (bi, 0, off + h))

    return pl.pallas_call(
        _ctx_attn_kernel,
        out_shape=jax.ShapeDtypeStruct((b, lc, n_heads * hd), F32),
        grid=(b, n_heads),
        in_specs=[spec(0), spec(n_heads), spec(2 * n_heads)],
        out_specs=spec(0),
        compiler_params=_cparams(("parallel", "parallel")),
        name="ctx_attention",
    )(qkv_c, qkv_c, qkv_c)


def _cos_sin(n_out, n_in, period):
    ang = 2.0 * np.pi * ((np.arange(n_out)[:, None] * np.arange(n_in)[None, :]) % period) / period
    return np.cos(ang), np.sin(ang)


def _lane_cat(x3):
    return jnp.concatenate([x3[j] for j in range(x3.shape[0])], axis=1)


def _rows_from_lanes(re, im, nc):
    return jnp.concatenate(
        [jnp.concatenate([re[:, j * LANES:(j + 1) * LANES], im[:, j * LANES:(j + 1) * LANES]], axis=1)
         for j in range(nc)], axis=0)


def _lanes_from_rows(x, nc, r):
    re = jnp.concatenate([x[j * r:(j + 1) * r, :LANES] for j in range(nc)], axis=1)
    im = jnp.concatenate([x[j * r:(j + 1) * r, LANES:] for j in range(nc)], axis=1)
    return jnp.concatenate([re, im], axis=0)


def _fourier_tables(r, nc):
    l = r * LANES
    c1, s1 = _cos_sin(r, r, r)
    m1 = np.block([[c1, s1], [-s1, c1]])
    tc, ts = _cos_sin(r, LANES, l)
    c2, s2 = _cos_sin(LANES, LANES, LANES)
    m2 = np.concatenate([c2, s2], axis=0) / math.sqrt(l)
    return (_const_bf16(m1), jnp.asarray(np.tile(tc, (1, nc)), F32), jnp.asarray(np.tile(ts, (1, nc)), F32),
            _const_bf16(m2))


def _fourier_kernel(zr_ref, zi_ref, m1_ref, tc_ref, ts_ref, m2_ref, o_ref, *, nc, r):
    z = jnp.concatenate([_lane_cat(zr_ref[0]), _lane_cat(zi_ref[0])], axis=0)
    a = _dot1(m1_ref[...], z)
    ar, ai = a[:r], a[r:]
    tc, ts = tc_ref[...], ts_ref[...]
    br = (ar * tc + ai * ts).astype(BF16)
    bi = (ai * tc - ar * ts).astype(BF16)
    y = _dot1(_rows_from_lanes(br, bi, nc), m2_ref[...])
    for j in range(nc):
        o_ref[0, j] = y[j * r:(j + 1) * r].T.astype(o_ref.dtype)


def _fourier_latent(pt, n_ch, nc):
    b, _, l = pt.shape
    r = l // LANES
    p4 = pt.reshape(b, pt.shape[1], r, LANES)
    m1, tc, ts, m2 = _fourier_tables(r, nc)
    nblk = n_ch // nc

    def const(a):
        return pl.BlockSpec(a.shape, lambda bi, c: (0,) * a.ndim)

    out = pl.pallas_call(
        functools.partial(_fourier_kernel, nc=nc, r=r),
        out_shape=jax.ShapeDtypeStruct((b, n_ch, LANES, r), BF16),
        grid=(b, nblk),
        in_specs=[
            pl.BlockSpec((1, nc, r, LANES), lambda bi, c: (bi, c, 0, 0)),
            pl.BlockSpec((1, nc, r, LANES), lambda bi, c: (bi, nblk + c, 0, 0)),
            const(m1), const(tc), const(ts), const(m2),
        ],
        out_specs=pl.BlockSpec((1, nc, LANES, r), lambda bi, c: (bi, c, 0, 0)),
        compiler_params=_cparams(("parallel", "parallel")),
        name="fourier_latent",
    )(p4, p4, m1, tc, ts, m2)
    return out.reshape(b, n_ch, l)


def _filter_mlp_kernel(bands_ref, w1t_ref, w1c_ref, w1s_ref, b1_ref, w2_ref, b2_ref, fr_ref, o_ref, *, l, tl):
    pos = (pl.program_id(0) * tl + lax.broadcasted_iota(jnp.int32, (1, tl), 1)).astype(F32)
    t = pos / float(max(l - 1, 1))
    ang = bands_ref[...] * (2.0 * math.pi / l) * pos
    fr = fr_ref[...]
    cos_a, sin_a = jnp.cos(ang), jnp.sin(ang)
    w1c, w1s = w1c_ref[...], w1s_ref[...]
    pre = w1t_ref[...] * t
    for k in range(HY_BANDS):
        pre = pre + w1c[:, k:k + 1] * cos_a[k:k + 1, :] - w1s[:, k:k + 1] * sin_a[k:k + 1, :]
    kpad = jnp.zeros((w2_ref.shape[1] - w2_ref.shape[0], tl), F32)
    h1 = jnp.sin(fr * (pre + b1_ref[...]))
    o_ref[...] = jnp.sin(fr * (_dot3k(w2_ref[...], jnp.concatenate([h1, kpad], axis=0)) + b2_ref[...]))


def _flip_lanes(x, anti_eye):
    nb = x.shape[1] // LANES
    hi, lo = _hi_lo(x)
    blocks = []
    for b in range(nb):
        sl = slice((nb - 1 - b) * LANES, (nb - b) * LANES)
        blocks.append(jnp.dot(hi[:, sl], anti_eye, preferred_element_type=F32)
                      + jnp.dot(lo[:, sl], anti_eye, preferred_element_type=F32))
    return jnp.concatenate(blocks, axis=1)


def _filter_out_kernel(h_ref, nb_ref, eye_ref, w3_ref, dl_ref, o_ref, *, l, tl):
    i = pl.program_id(0)
    lane = lax.broadcasted_iota(jnp.int32, (1, tl), 1)
    n = i * tl + lane
    t = jnp.where(n < l, n, 2 * l - n).astype(F32) / float(max(l - 1, 1))
    tile = h_ref[...]
    rev = pltpu.roll(_flip_lanes(tile, eye_ref[...]), 1, 1)
    rev = jnp.where(lane == 0, nb_ref[:, 0:1], rev)
    h2 = jnp.where(i >= l // tl, rev, tile)
    kpad = jnp.zeros((w3_ref.shape[2] - h2.shape[0], tl), F32)
    out = _dot1(w3_ref[0], jnp.concatenate([h2, kpad], axis=0)) * jnp.exp(-t * dl_ref[...])
    o_ref[...] = jnp.where(n == l, 0.0, out).astype(o_ref.dtype)


def _hyena_filters(l, w1, b1, w2, b2, w3, freq, width):
    hid = w1.shape[1]
    kdim = -(-hid // LANES) * LANES
    pad = kdim - hid
    tl = min(2048, l)
    rows = HY_ORDER * width
    bands = np.linspace(1e-4, HY_BANDS - 1, HY_BANDS, dtype=np.float32).reshape(HY_BANDS, 1)
    deltas = np.abs(np.linspace(HY_MIN_DECAY, HY_MAX_DECAY, width, dtype=np.float32))
    dl = np.tile(deltas, HY_ORDER).reshape(rows, 1)
    w1t = w1.T
    w2t = jnp.pad(w2.T, ((0, 0), (0, pad)))
    w3d = w3.reshape(hid, HY_ORDER, 2, width).transpose(2, 1, 3, 0).reshape(2, rows, hid)
    w3d = jnp.pad(w3d, ((0, 0), (0, 0), (0, pad)))
    col = lambda v: v.reshape(hid, 1)

    def const(shape):
        return pl.BlockSpec(shape, lambda i: (0,) * len(shape))

    hidden = pl.pallas_call(
        functools.partial(_filter_mlp_kernel, l=l, tl=tl),
        out_shape=jax.ShapeDtypeStruct((hid, l), F32),
        grid=(l // tl,),
        in_specs=[const((HY_BANDS, 1)), const((hid, 1)), const((hid, HY_BANDS)), const((hid, HY_BANDS)),
                  const((hid, 1)), const((hid, kdim)), const((hid, 1)), const((hid, 1))],
        out_specs=pl.BlockSpec((hid, tl), lambda i: (0, i)),
        compiler_params=_cparams(("parallel",)),
        name="hyena_filter_mlp",
    )(jnp.asarray(bands), w1t[:, 0:1], w1t[:, 1:1 + HY_BANDS], w1t[:, 1 + HY_BANDS:], col(b1), w2t, col(b2),
      col(freq))
    nt = l // tl
    anti_eye = _const_bf16(np.eye(LANES)[::-1])
    return pl.pallas_call(
        functools.partial(_filter_out_kernel, l=l, tl=tl),
        out_shape=jax.ShapeDtypeStruct((rows, 2 * l), BF16),
        grid=(2 * nt,),
        in_specs=[pl.BlockSpec((hid, tl), lambda i: (0, jnp.where(i < nt, i, 2 * nt - 1 - i))),
                  pl.BlockSpec((hid, tl), lambda i: (0, jnp.clip(2 * nt - i, 0, nt - 1))),
                  const((LANES, LANES)),
                  pl.BlockSpec((1, rows, kdim), lambda i: (i // nt, 0, 0)),
                  const((rows, 1))],
        out_specs=pl.BlockSpec((rows, tl), lambda i: (0, i)),
        compiler_params=_cparams(("parallel",)),
        name="hyena_filters",
    )(hidden, hidden, anti_eye, w3d, jnp.asarray(dl))


def _conv_tables(r, nc):
    r2 = 2 * r
    n = r2 * LANES
    c1, s1 = _cos_sin(r2, r2, r2)
    c1h, s1h = c1[:, :r], s1[:, :r]
    m1 = np.block([[c1h, s1h], [-s1h, c1h]])
    m1_real = np.concatenate([c1, -s1], axis=0)
    tc, ts = _cos_sin(r2, LANES, n)
    c2, s2 = _cos_sin(LANES, LANES, LANES)
    m2 = np.block([[c2, -s2], [s2, c2]])
    m2i = np.block([[c2, s2], [-s2, c2]])
    ct, st = c1h.T, s1h.T
    m1i = np.block([[ct, -st], [st, ct]]) / n
    return dict(
        m1=_const_bf16(m1), m1_real=_const_bf16(m1_real), m2=_const_bf16(m2), m2i=_const_bf16(m2i),
        m1i=_const_bf16(m1i),
        tc_l=_const_bf16(np.tile(tc, (1, nc))), ts_l=_const_bf16(np.tile(ts, (1, nc))),
        tc_r=_const_bf16(np.tile(tc, (nc, 1))), ts_r=_const_bf16(np.tile(ts, (nc, 1))))


def _kernel_spectrum(k3, m1_real, tc_l, ts_l, m2, nc, r2):
    a = _dot1(m1_real, _lane_cat(k3)).astype(BF16)
    ar, ai = a[:r2], a[r2:]
    return _dot1(_rows_from_lanes(ar * tc_l + ai * ts_l, ai * tc_l - ar * ts_l, nc), m2).astype(BF16)


def _shift_tokens(t, r, direction):
    lane = lax.broadcasted_iota(jnp.int32, t.shape, 2)
    row = lax.broadcasted_iota(jnp.int32, t.shape, 1)
    if direction < 0:
        near = pltpu.roll(t, 1, 2)
        wrap = pltpu.roll(near, 1, 1)
        edge = lane == 0
        dead = edge & (row == 0)
    else:
        near = pltpu.roll(t, LANES - 1, 2)
        wrap = pltpu.roll(near, r - 1, 1)
        edge = lane == LANES - 1
        dead = edge & (row == r - 1)
    return jnp.where(dead, 0.0, jnp.where(edge, wrap, near))


def _long_conv(zr, zi, kspec, m1, m2, m2i, m1i, tc_l, ts_l, tc_r, ts_r, nc, r):
    r2 = 2 * r
    a = _dot1(m1, jnp.concatenate([zr.astype(BF16), zi.astype(BF16)], axis=0)).astype(BF16)
    ar, ai = a[:r2], a[r2:]
    x = _dot1(_rows_from_lanes(ar * tc_l + ai * ts_l, ai * tc_l - ar * ts_l, nc), m2).astype(BF16)
    xr, xi = x[:, :LANES], x[:, LANES:]
    kr, ki = kspec[:, :LANES], kspec[:, LANES:]
    bb = _dot1(jnp.concatenate([xr * kr - xi * ki, xr * ki + xi * kr], axis=1), m2i).astype(BF16)
    pr, pi = bb[:, :LANES], bb[:, LANES:]
    q = jnp.concatenate([pr * tc_r - pi * ts_r, pi * tc_r + pr * ts_r], axis=1)
    out = _dot1(m1i, _lanes_from_rows(q, nc, r2))
    return out[:r], out[r:]


def _hyena_kernel(v0, v1, a0, a1, b0, b1, k0_ref, k1_ref, cw_ref, cb_ref, bias_ref,
                  m1_ref, m2_ref, m2i_ref, m1i_ref, tcl_ref, tsl_ref, tcr_ref, tsr_ref, m1r_ref, o_ref, *, nc, r):
    tabs = (m1_ref[...], m2_ref[...], m2i_ref[...], m1i_ref[...],
            tcl_ref[...], tsl_ref[...], tcr_ref[...], tsr_ref[...])

    def spectrum(k_ref):
        return _kernel_spectrum(k_ref[...], m1r_ref[...], tcl_ref[...], tsl_ref[...], m2_ref[...], nc, 2 * r)

    def short_conv(ref, part):
        t = ref[0].astype(F32)
        w = cw_ref[part]
        return (_lane_cat(_shift_tokens(t, r, -1)) * w[0:1] + _lane_cat(t) * w[1:2]
                + _lane_cat(_shift_tokens(t, r, +1)) * w[2:3] + cb_ref[part])

    vr, vi = short_conv(v0, 0), short_conv(v1, 0)
    x1r, x1i = short_conv(a0, 1), short_conv(a1, 1)
    x2r, x2i = short_conv(b0, 2), short_conv(b1, 2)
    bias = bias_ref[...]
    yr, yi = _long_conv(vr, vi, spectrum(k0_ref), *tabs, nc, r)
    zr = x1r * (yr + vr * bias[0:1])
    zi = x1i * (yi + vi * bias[0:1])
    yr, yi = _long_conv(zr, zi, spectrum(k1_ref), *tabs, nc, r)
    outr = x2r * (yr + zr * bias[1:2])
    outi = x2i * (yi + zi * bias[1:2])
    for j in range(nc):
        o_ref[0, j] = outr[:, j * LANES:(j + 1) * LANES].astype(o_ref.dtype)
        o_ref[1, j] = outi[:, j * LANES:(j + 1) * LANES].astype(o_ref.dtype)


def _hyena_latent(pt, ch0, width, k2t, conv_w, conv_b, bias, tabs, nc):
    b, c_all, l = pt.shape
    assert b == 2, "the two batch entries ride one complex transform"
    r = l // LANES
    r2 = 2 * r
    p4 = pt.reshape(b, c_all, r, LANES)
    k3 = k2t.reshape(k2t.shape[0], r2, LANES)
    nblk = width // nc
    rep = lambda a: jnp.repeat(a, LANES, axis=-1)
    cw = rep(conv_w.reshape(3, 3, width).transpose(1, 0, 2))
    cb = rep(conv_b.reshape(3, 1, width))
    bs = rep(bias)

    def inp(bi, part):
        off = (ch0 + part * width) // nc
        return pl.BlockSpec((1, nc, r, LANES), lambda c: (bi, off + c, 0, 0))

    def const(a):
        return pl.BlockSpec(a.shape, lambda c: (0,) * a.ndim)

    names = ("m1", "m2", "m2i", "m1i", "tc_l", "ts_l", "tc_r", "ts_r", "m1_real")
    consts = [tabs[k] for k in names]
    out = pl.pallas_call(
        functools.partial(_hyena_kernel, nc=nc, r=r),
        out_shape=jax.ShapeDtypeStruct((b, width, r, LANES), BF16),
        grid=(nblk,),
        in_specs=[inp(0, 0), inp(1, 0), inp(0, 1), inp(1, 1), inp(0, 2), inp(1, 2),
                  pl.BlockSpec((nc, r2, LANES), lambda c: (c, 0, 0)),
                  pl.BlockSpec((nc, r2, LANES), lambda c: (nblk + c, 0, 0)),
                  pl.BlockSpec((3, 3, nc * LANES), lambda c: (0, 0, c)),
                  pl.BlockSpec((3, 1, nc * LANES), lambda c: (0, 0, c)),
                  pl.BlockSpec((HY_ORDER, nc * LANES), lambda c: (0, c))]
                 + [const(a) for a in consts],
        out_specs=pl.BlockSpec((b, nc, r, LANES), lambda c: (0, c, 0, 0)),
        compiler_params=_cparams(("parallel",)),
        name="hyena_latent",
    )(p4, p4, p4, p4, p4, p4, k3, k3, cw, cb, bs, *consts)
    return out.reshape(b, width, l)


def _ctx_fourier_kernel(zr_ref, zi_ref, m_ref, o_ref):
    z = jnp.concatenate([zr_ref[0], zi_ref[0]], axis=1).astype(F32)
    o_ref[0] = _dot_data_const(z, m_ref[...])


def _ctx_fourier(ptc, n_ch):
    b, _, lc = ptc.shape
    c, s = _cos_sin(lc, lc, lc)
    m = _const_rhs3(np.concatenate([c, s], axis=0) / math.sqrt(lc))
    return pl.pallas_call(
        _ctx_fourier_kernel,
        out_shape=jax.ShapeDtypeStruct((b, n_ch, lc), F32),
        grid=(b,),
        in_specs=[pl.BlockSpec((1, n_ch, lc), lambda bi: (bi, 0, 0)),
                  pl.BlockSpec((1, n_ch, lc), lambda bi: (bi, 1, 0)),
                  pl.BlockSpec(m.shape, lambda bi: (0, 0))],
        out_specs=pl.BlockSpec((1, n_ch, lc), lambda bi: (bi, 0, 0)),
        compiler_params=_cparams(("parallel",)),
        name="ctx_fourier",
    )(ptc, ptc, m)


def _ctx_hyena_kernel(v0, v1, a0, a1, b0, b1, k_ref, cw_ref, cb_ref, bias_ref, mk_ref, mf_ref, mi_ref,
                      o_ref, *, lc, width):
    def short_conv(ref, part):
        t = ref[0].astype(F32)
        lane = lax.broadcasted_iota(jnp.int32, t.shape, 1)
        prv = jnp.where(lane == 0, 0.0, pltpu.roll(t, 1, 1))
        nxt = jnp.where(lane == lc - 1, 0.0, pltpu.roll(t, lc - 1, 1))
        w = cw_ref[part]
        return prv * w[:, 0:1] + t * w[:, 1:2] + nxt * w[:, 2:3] + cb_ref[part]

    kspec = _dot_data_const(k_ref[...].astype(F32), mk_ref[...])
    n = 2 * lc

    def long_conv(zr, zi, ks):
        x = _dot_data_const(jnp.concatenate([zr, zi], axis=1), mf_ref[...])
        xr, xi = x[:, :n], x[:, n:]
        kr, ki = ks[:, :n], ks[:, n:]
        y = jnp.concatenate([xr * kr - xi * ki, xr * ki + xi * kr], axis=1)
        out = _dot_data_const(y, mi_ref[...])
        return out[:, :lc], out[:, lc:]

    vr, vi = short_conv(v0, 0), short_conv(v1, 0)
    x1r, x1i = short_conv(a0, 1), short_conv(a1, 1)
    x2r, x2i = short_conv(b0, 2), short_conv(b1, 2)
    bias = bias_ref[...]
    yr, yi = long_conv(vr, vi, kspec[:width])
    zr = x1r * (yr + vr * bias[:, 0:1])
    zi = x1i * (yi + vi * bias[:, 0:1])
    yr, yi = long_conv(zr, zi, kspec[width:])
    o_ref[0] = x2r * (yr + zr * bias[:, 1:2])
    o_ref[1] = x2i * (yi + zi * bias[:, 1:2])


def _ctx_hyena(ptc, ch0, width, k2t, conv_w, conv_b, bias):
    b, _, lc = ptc.shape
    assert b == 2
    n = 2 * lc
    c, s = _cos_sin(n, n, n)
    mk = _const_rhs3(np.concatenate([c, -s], axis=1))
    ch, sh = c[:lc], s[:lc]
    mf = _const_rhs3(np.block([[ch, -sh], [sh, ch]]))
    ci, si = c[:, :lc], s[:, :lc]
    mi = _const_rhs3(np.block([[ci, si], [-si, ci]]) / n)
    cw = conv_w.reshape(3, 3, width).transpose(1, 2, 0)
    cb = conv_b.reshape(3, width, 1)

    def inp(bi, part):
        return pl.BlockSpec((1, width, lc), lambda i: (bi, ch0 // width + part, 0))

    def const(a):
        return pl.BlockSpec(a.shape, lambda i: (0,) * a.ndim)

    args = (k2t, cw, cb, bias.T, mk, mf, mi)
    return pl.pallas_call(
        functools.partial(_ctx_hyena_kernel, lc=lc, width=width),
        out_shape=jax.ShapeDtypeStruct((b, width, lc), F32),
        grid=(1,),
        in_specs=[inp(0, 0), inp(1, 0), inp(0, 1), inp(1, 1), inp(0, 2), inp(1, 2)] + [const(a) for a in args],
        out_specs=pl.BlockSpec((b, width, lc), lambda i: (0, 0, 0)),
        compiler_params=_cparams(("arbitrary",)),
        name="ctx_hyena",
    )(ptc, ptc, ptc, ptc, ptc, ptc, *args)


def _merge_kernel(a_ref, f_ref, hy_ref, x_ref, gate_ref, ga_ref, gf_ref, gh_ref, wa_ref, wf_ref, wh_ref, o_ref):
    a = a_ref[0].astype(F32)
    ya = a * lax.rsqrt(jnp.mean(a * a, axis=-1, keepdims=True) + EPS) * ga_ref[...]
    acc = jnp.dot(ya.astype(BF16), wa_ref[...], preferred_element_type=F32)

    def cm_part(ref, g_ref, w_ref):
        t = ref[0].astype(F32)
        y = t * lax.rsqrt(jnp.mean(t * t, axis=0, keepdims=True) + EPS) * g_ref[...]
        return jnp.dot(y.T.astype(BF16), w_ref[...], preferred_element_type=F32)

    acc = acc + cm_part(f_ref, gf_ref, wf_ref) + cm_part(hy_ref, gh_ref, wh_ref)
    o_ref[0] = x_ref[0] + gate_ref[0] * acc


def _merge_out(a, ft, ht, x, gate, g, w_all, layer, tm):
    b, l, d = x.shape
    wa_n, wf_n, wh_n = a.shape[2], ft.shape[1], ht.shape[1]
    assert wa_n % wf_n == 0 and wf_n == wh_n
    tm = min(tm, l)
    ga = g[:wa_n].reshape(1, wa_n)
    gf = g[wa_n:wa_n + wf_n].reshape(wf_n, 1)
    gh = g[wa_n + wf_n:].reshape(wh_n, 1)

    def const(arr):
        return pl.BlockSpec(arr.shape, lambda bi, i: (0,) * arr.ndim)

    def w_rows(n, blk):
        return pl.BlockSpec((None, n, d), lambda bi, i: (layer, blk, 0))

    return pl.pallas_call(
        _merge_kernel,
        out_shape=jax.ShapeDtypeStruct((b, l, d), F32),
        grid=(b, l // tm),
        in_specs=[pl.BlockSpec((1, tm, wa_n), lambda bi, i: (bi, i, 0)),
                  pl.BlockSpec((1, wf_n, tm), lambda bi, i: (bi, 0, i)),
                  pl.BlockSpec((1, wh_n, tm), lambda bi, i: (bi, 0, i)),
                  pl.BlockSpec((1, tm, d), lambda bi, i: (bi, i, 0)),
                  pl.BlockSpec((1, 1, d), lambda bi, i: (bi, 0, 0)),
                  const(ga), const(gf), const(gh),
                  w_rows(wa_n, 0), w_rows(wf_n, wa_n // wf_n), w_rows(wh_n, wa_n // wf_n + 1)],
        out_specs=pl.BlockSpec((1, tm, d), lambda bi, i: (bi, i, 0)),
        compiler_params=_cparams(("parallel", "parallel")),
        name="merge_out",
    )(a, ft, ht, x, gate.reshape(b, 1, d), ga, gf, gh, w_all, w_all, w_all)


HALO = 16


def _ffn_kernel(xp_ref, x_ref, xn_ref, g_ref, sh_ref, sc_ref, gate_ref, wg_ref, wu_ref, cw_ref, cb_ref, wd_ref,
                fg_ref, o_ref, h_ref, *, tm, final_norm):
    i = pl.program_id(1)
    c = pl.program_id(2)
    last_tile = pl.num_programs(1) - 1

    n_ext = tm + 2 * HALO

    def down_partial(ge, up):
        row = lax.broadcasted_iota(jnp.int32, ge.shape, 0)
        outside = ((row < HALO) & (i == 0)) | ((row >= HALO + tm) & (i == last_tile))
        ge = jnp.where(outside, 0.0, ge)
        gp = pltpu.roll(ge, 1, 0)[HALO:HALO + tm]
        gn = pltpu.roll(ge, n_ext - 1, 0)[HALO:HALO + tm]
        cw = cw_ref[...]
        conv = gp * cw[0:1] + ge[HALO:HALO + tm] * cw[1:2] + gn * cw[2:3] + cb_ref[...]
        inner = 0.7978845608028654 * (conv + 0.044715 * (conv * conv * conv))
        act = 0.5 * conv * (1.0 + jnp.tanh(inner)) * up
        return jnp.dot(act.astype(BF16), wd_ref[...], preferred_element_type=F32)

    @pl.when(c == 0)
    def _():
        g, sh, sc = g_ref[...], sh_ref[0], sc_ref[0]
        wg, wu = wg_ref[...], wu_ref[...]
        rs = tm // NORM_SLABS
        bounds = [(0, HALO)] + [(HALO + s * rs, HALO + (s + 1) * rs) for s in range(NORM_SLABS)] + [(HALO + tm, n_ext)]
        ge_parts, up_parts = [], []
        for lo, hi in bounds:
            if lo == 0:
                xs = xp_ref[0, 0]
            elif hi == n_ext:
                xs = xn_ref[0, 0]
            else:
                xs = x_ref[0, lo - HALO:hi - HALO, :]
            h = _norm_mod(xs, g, sh, sc).astype(BF16)
            h_ref[lo:hi, :] = h
            ge_parts.append(jnp.dot(h, wg, preferred_element_type=F32))
            if lo != 0 and hi != n_ext:
                up_parts.append(jnp.dot(h, wu, preferred_element_type=F32))
        o_ref[0] = down_partial(jnp.concatenate(ge_parts, axis=0), jnp.concatenate(up_parts, axis=0))

    @pl.when(c > 0)
    def _():
        h = h_ref[...]
        ge = jnp.dot(h, wg_ref[...], preferred_element_type=F32)
        up = jnp.dot(h[HALO:HALO + tm], wu_ref[...], preferred_element_type=F32)
        o_ref[0] += down_partial(ge, up)

    @pl.when(c == pl.num_programs(2) - 1)
    def _():
        y = x_ref[0] + gate_ref[0] * o_ref[0]
        if final_norm:
            y = y * lax.rsqrt(jnp.mean(y * y, axis=-1, keepdims=True) + EPS) * fg_ref[...]
        o_ref[0] = y


def _ffn(x, g, sh, sc, gate, w_up, conv_w, conv_b, w_down, layer, final_g, tm, tf, *, final_norm):
    b, l, d = x.shape
    dff = w_down.shape[1]
    tm = min(tm, l)
    nch = dff // tf
    x4 = x.reshape(b, l // HALO, HALO, d)
    per_tile = tm // HALO
    nhalo = l // HALO
    vec = pl.BlockSpec((1, 1, d), lambda bi, i, c: (bi, 0, 0))
    return pl.pallas_call(
        functools.partial(_ffn_kernel, tm=tm, final_norm=final_norm),
        out_shape=jax.ShapeDtypeStruct((b, l, d), F32),
        grid=(b, l // tm, nch),
        in_specs=[
            pl.BlockSpec((1, 1, HALO, d), lambda bi, i, c: (bi, jnp.maximum(i * per_tile - 1, 0), 0, 0)),
            pl.BlockSpec((1, tm, d), lambda bi, i, c: (bi, i, 0)),
            pl.BlockSpec((1, 1, HALO, d), lambda bi, i, c: (bi, jnp.minimum((i + 1) * per_tile, nhalo - 1), 0, 0)),
            pl.BlockSpec((1, d), lambda bi, i, c: (0, 0)),
            vec, vec, vec,
            pl.BlockSpec((None, d, tf), lambda bi, i, c: (layer, 0, c)),
            pl.BlockSpec((None, d, tf), lambda bi, i, c: (layer, 0, nch + c)),
            pl.BlockSpec((3, tf), lambda bi, i, c: (0, c)),
            pl.BlockSpec((1, tf), lambda bi, i, c: (0, c)),
            pl.BlockSpec((None, tf, d), lambda bi, i, c: (layer, c, 0)),
            pl.BlockSpec((1, d), lambda bi, i, c: (0, 0)),
        ],
        out_specs=pl.BlockSpec((1, tm, d), lambda bi, i, c: (bi, i, 0)),
        scratch_shapes=[pltpu.VMEM((tm + 2 * HALO, d), BF16)],
        compiler_params=_cparams(("parallel", "parallel", "arbitrary")),
        name="ffn",
    )(x4, x, x4, g.reshape(1, d), sh.reshape(b, 1, d), sc.reshape(b, 1, d), gate.reshape(b, 1, d),
      w_up, w_up, conv_w, conv_b.reshape(1, dff), w_down, final_g.reshape(1, d))


TOKEN_TILE = 512
PROJ_TOKEN_TILE = 1024
PROJ_TOK_COL_TILE = 1024
PROJ_CM_COL_TILE = 512
FFN_TOKEN_TILE = 1024
FFN_COL_TILE = 512
MIX_CHANNELS = 8
FOURIER_CHANNELS = 32


def kernel(x, c, ctx, c_ctx, ada_w, ada_b, norm1_g, norm2_g, w_in, na_rpb, hy_conv_w, hy_conv_b, hy_w1, hy_b1,
           hy_w2, hy_b2, hy_w3, hy_freq, hy_bias, mix_norm_g, w_out, ffn_w_up, ffn_conv_w, ffn_conv_b, ffn_w_down,
           final_norm_g):
    b, l, d = x.shape
    lc = ctx.shape[1]
    depth = ada_w.shape[0]
    na_w, fn_w = d // 2, d // 4
    hy_w = d - na_w - fn_w
    n_heads = na_w // HEAD_DIM
    qkv_w = 3 * na_w
    rows = l // GRID_W
    assert b + 1 <= 8 and l % (NA_ROWS_PER_BLOCK * GRID_W) == 0 and rows >= 2 * NA_ROWS_PER_BLOCK
    r = l // LANES
    nc = MIX_CHANNELS
    conv_tabs = _conv_tables(r, nc)

    s_in = jnp.concatenate([c, c_ctx[None], jnp.zeros((8 - b - 1, d), F32)], axis=0)
    mod_all = _ada_mod(s_in, ada_w, ada_b)
    xc = ctx
    w_out_bf, w_up_bf, w_down_bf = w_out.astype(BF16), ffn_w_up.astype(BF16), ffn_w_down.astype(BF16)
    for layer in range(depth):
        update_ctx = layer < depth - 1
        mod = mod_all[layer]
        sh1, sc1, g1, sh2, sc2, g2 = jnp.split(mod[:b], N_MOD, axis=-1)
        csh1, csc1, cg1, csh2, csc2, cg2 = jnp.split(jnp.broadcast_to(mod[b:b + 1], (b, N_MOD * d)), N_MOD, axis=-1)

        wl = w_in[layer]
        w_fold = _fold_fourier_weights(wl[:, qkv_w:qkv_w + fn_w])
        w_tok = jnp.concatenate([wl[:, :na_w] * (HEAD_DIM ** -0.5 * LOG2E), wl[:, na_w:qkv_w]], axis=1).astype(BF16)
        w_cm = jnp.concatenate([w_fold[0], w_fold[1], wl[:, qkv_w + fn_w:]], axis=1).astype(BF16)
        hy0 = 2 * fn_w

        tiles = dict(tm=PROJ_TOKEN_TILE, tn_tok=PROJ_TOK_COL_TILE, tn_cm=PROJ_CM_COL_TILE)
        qkv, pt = _proj(x, norm1_g[layer], sh1, sc1, w_tok, w_cm, **tiles)
        qkv_c, ptc = _proj(xc, norm1_g[layer], csh1, csc1, w_tok, w_cm, **tiles)

        a = _na_attention(qkv, qkv_c, _na_bias_table(na_rpb[layer], rows), n_heads)
        yf = _fourier_latent(pt, fn_w, FOURIER_CHANNELS)
        filt = (hy_w1[layer], hy_b1[layer], hy_w2[layer], hy_b2[layer], hy_w3[layer], hy_freq[layer])
        yh = _hyena_latent(pt, hy0, hy_w, _hyena_filters(l, *filt, hy_w), hy_conv_w[layer], hy_conv_b[layer],
                           hy_bias[layer], conv_tabs, nc)
        x_new = _merge_out(a, yf, yh, x, g1, mix_norm_g[layer], w_out_bf, layer, TOKEN_TILE)

        if update_ctx:
            ac = _ctx_attention(qkv_c, n_heads)
            yfc = _ctx_fourier(ptc, fn_w)
            yhc = _ctx_hyena(ptc, hy0, hy_w, _hyena_filters(lc, *filt, hy_w), hy_conv_w[layer], hy_conv_b[layer],
                             hy_bias[layer])
            xc = _merge_out(ac, yfc, yhc, xc, cg1, mix_norm_g[layer], w_out_bf, layer, TOKEN_TILE)
        x = x_new

        x = _ffn(x, norm2_g[layer], sh2, sc2, g2, w_up_bf, ffn_conv_w[layer], ffn_conv_b[layer], w_down_bf, layer,
                 final_norm_g, FFN_TOKEN_TILE, FFN_COL_TILE, final_norm=not update_ctx)
        if update_ctx:
            xc = _ffn(xc, norm2_g[layer], csh2, csc2, cg2, w_up_bf, ffn_conv_w[layer], ffn_conv_b[layer], w_down_bf,
                      layer, final_norm_g, FFN_TOKEN_TILE, FFN_COL_TILE, final_norm=False)
    return x
```

```python
import functools
import math

import numpy as np
import jax
import jax.numpy as jnp
from jax import lax
from jax.experimental import pallas as pl
from jax.experimental.pallas import tpu as pltpu

F32 = jnp.float32
BF16 = jnp.bfloat16

EPS = 1e-6
HEAD_DIM = 128
GRID_W = 64
NA_KH = 8
NA_KW = 16
NA_ROWS_PER_BLOCK = 8
FN_GROUP_DIM = 128
HY_ORDER = 2
HY_BANDS = 16
HY_FAST_DECAY = 0.3
HY_SLOW_DECAY = 1.5
HY_TARGET = 1e-2
HY_MIN_DECAY = math.log(HY_TARGET) / HY_SLOW_DECAY
HY_MAX_DECAY = math.log(HY_TARGET) / HY_FAST_DECAY
N_MOD = 6
LANES = 128
NEG_INF = -1e30
LOG2E = math.log2(math.e)
VMEM_LIMIT = 60 * 1024 * 1024

NT_DIMS = (((1,), (1,)), ((), ()))


def _cparams(sem):
    return pltpu.CompilerParams(dimension_semantics=sem, vmem_limit_bytes=VMEM_LIMIT)


def _hi_lo(a):
    hi = a.astype(BF16)
    lo = (a - hi.astype(F32)).astype(BF16)
    return hi, lo


def _np_hi_lo(m):
    m = np.asarray(m, np.float32)
    hi = m.astype(BF16)
    lo = (m - hi.astype(np.float32)).astype(BF16)
    return hi, lo


def _const_rhs3(m):
    hi, lo = _np_hi_lo(m)
    return jnp.asarray(np.concatenate([hi, hi, lo], axis=0))


def _dot_data_const(a, c3):
    hi, lo = _hi_lo(a)
    return jnp.dot(jnp.concatenate([hi, lo, hi], axis=1), c3, preferred_element_type=F32)


def _dot3k(a, b):
    ah, al = _hi_lo(a)
    bh, bl = _hi_lo(b)
    return jnp.dot(jnp.concatenate([ah, al, ah], axis=1), jnp.concatenate([bh, bh, bl], axis=0),
                   preferred_element_type=F32)


def _const_bf16(m):
    return jnp.asarray(np.asarray(m, np.float32).astype(BF16))


def _dot1(a, b):
    return jnp.dot(a.astype(BF16), b.astype(BF16), preferred_element_type=F32)


def _ada_kernel(s_ref, w_ref, b_ref, o_ref):
    s = s_ref[...]
    s = s / (1.0 + jnp.exp(-s))
    o_ref[0] = _dot1(s, w_ref[0]) + b_ref[0]


def _ada_mod(s_in, ada_w, ada_b):
    depth, d, n = ada_w.shape
    tn = 1024
    return pl.pallas_call(
        _ada_kernel,
        out_shape=jax.ShapeDtypeStruct((depth, 8, n), F32),
        grid=(depth, n // tn),
        in_specs=[
            pl.BlockSpec((8, d), lambda l, j: (0, 0)),
            pl.BlockSpec((1, d, tn), lambda l, j: (l, 0, j)),
            pl.BlockSpec((1, 1, tn), lambda l, j: (l, 0, j)),
        ],
        out_specs=pl.BlockSpec((1, 8, tn), lambda l, j: (l, 0, j)),
        compiler_params=_cparams(("parallel", "parallel")),
        name="ada_mod",
    )(s_in, ada_w, ada_b.reshape(depth, 1, n))


def _norm_mod(x, g, sh, sc):
    ms = jnp.mean(x * x, axis=-1, keepdims=True)
    return (x * lax.rsqrt(ms + EPS) * g) * (1.0 + sc) + sh


NORM_SLABS = 4


def _proj_kernel(*refs, n_tm):
    x_refs = refs[:NORM_SLABS]
    g_ref, sh_ref, sc_ref, wa_ref, wb_ref, otm_ref, ocm_ref, h_ref = refs[NORM_SLABS:]
    j = pl.program_id(2)
    rs = x_refs[0].shape[1]

    @pl.when(j == 0)
    def _():
        g, sh, sc = g_ref[...], sh_ref[0], sc_ref[0]
        w = wa_ref[...]
        for s in range(NORM_SLABS):
            h = _norm_mod(x_refs[s][0], g, sh, sc).astype(BF16)
            h_ref[s * rs:(s + 1) * rs, :] = h
            otm_ref[0, s * rs:(s + 1) * rs, :] = jnp.dot(h, w, preferred_element_type=F32).astype(otm_ref.dtype)

    @pl.when((j > 0) & (j < n_tm))
    def _():
        otm_ref[0] = jnp.dot(h_ref[...], wa_ref[...], preferred_element_type=F32).astype(otm_ref.dtype)

    @pl.when(j >= n_tm)
    def _():
        y = jnp.dot(h_ref[...], wb_ref[...], preferred_element_type=F32)
        ocm_ref[0] = y.T.astype(ocm_ref.dtype)


def _proj(x, g, sh, sc, w_tok, w_cm, *, tm, tn_tok, tn_cm):
    b, l, d = x.shape
    n_tok, n_ch = w_tok.shape[1], w_cm.shape[1]
    tm = min(tm, l)
    n_tm, n_cm = n_tok // tn_tok, n_ch // tn_cm
    assert n_tm * tn_tok == n_tok and n_cm * tn_cm == n_ch and tm % (16 * NORM_SLABS) == 0
    vec = pl.BlockSpec((1, 1, d), lambda bi, i, j: (bi, 0, 0))
    rs = tm // NORM_SLABS
    nt, nj = l // tm, n_tm + n_cm
    n_slabs = b * l // rs
    assert nj > NORM_SLABS

    def slab(s):
        def index(bi, i, j):
            tile = bi * nt + i + (j >= nj - s).astype(jnp.int32)
            return (jnp.minimum(tile * NORM_SLABS + s, n_slabs - 1), 0, 0)
        return pl.BlockSpec((1, rs, d), index)

    tok_j = lambda j: jnp.minimum(j, n_tm - 1)
    cm_j = lambda j: jnp.maximum(j - n_tm, 0)
    return pl.pallas_call(
        functools.partial(_proj_kernel, n_tm=n_tm),
        out_shape=(jax.ShapeDtypeStruct((b, l, n_tok), BF16), jax.ShapeDtypeStruct((b, n_ch, l), BF16)),
        grid=(b, nt, nj),
        in_specs=[slab(s) for s in range(NORM_SLABS)] + [
            pl.BlockSpec((1, d), lambda bi, i, j: (0, 0)),
            vec, vec,
            pl.BlockSpec((d, tn_tok), lambda bi, i, j: (0, tok_j(j))),
            pl.BlockSpec((d, tn_cm), lambda bi, i, j: (0, cm_j(j))),
        ],
        out_specs=(pl.BlockSpec((1, tm, tn_tok), lambda bi, i, j: (bi, i, tok_j(j))),
                   pl.BlockSpec((1, tn_cm, tm), lambda bi, i, j: (bi, cm_j(j), i))),
        scratch_shapes=[pltpu.VMEM((tm, d), BF16)],
        compiler_params=_cparams(("parallel", "parallel", "arbitrary")),
        name="proj_in",
    )(*([x.reshape(n_slabs, rs, d)] * NORM_SLABS), g.reshape(1, d), sh.reshape(b, 1, d), sc.reshape(b, 1, d),
      w_tok, w_cm)


def _fold_kernel(w_ref, m_ref, o_ref):
    o_ref[0] = _dot_data_const(w_ref[...], m_ref[...])


def _fold_fourier_weights(w_f):
    d, width = w_f.shape
    gd = FN_GROUP_DIM
    idx = np.arange(gd)
    ang = 2.0 * np.pi * np.outer(idx, idx) / gd
    mat = np.concatenate([np.cos(ang), -np.sin(ang)], axis=1) / math.sqrt(gd)
    groups = width // gd
    return pl.pallas_call(
        _fold_kernel,
        out_shape=jax.ShapeDtypeStruct((2, d, width), F32),
        grid=(groups, 2),
        in_specs=[
            pl.BlockSpec((d, gd), lambda g, p: (0, g)),
            pl.BlockSpec((3 * gd, gd), lambda g, p: (0, p)),
        ],
        out_specs=pl.BlockSpec((1, d, gd), lambda g, p: (p, 0, g)),
        compiler_params=_cparams(("parallel", "parallel")),
        name="fold_fourier",
    )(w_f, _const_rhs3(mat))


def _na_bias_table(rpb, rows):
    h = rpb.shape[0]
    w, kw, kh, rb = GRID_W, NA_KW, NA_KH, NA_ROWS_PER_BLOCK
    col = np.arange(w)
    cs = np.clip(col - kw // 2, 0, w - kw)
    kc = np.arange(w)[None, :]
    col_ok = (kc >= cs[:, None]) & (kc < cs[:, None] + kw)
    padded = jnp.pad((rpb * LOG2E).astype(BF16), ((0, 0), (0, 0), (w - kw, w - kw)))
    t1 = jnp.stack([padded[:, :, w - 1 - q:2 * w - 1 - q] for q in range(w)], axis=1)
    t1 = jnp.where(col_ok[None, :, None, :], t1, NEG_INF)
    slots = 2 * rb
    blocks = []
    for t, row0 in enumerate((0, rb, rows - rb)):
        per_row = []
        for qr in range(rb):
            r = row0 + qr
            r0 = r - kh // 2 if t == 1 else min(max(r - kh // 2, 0), rows - kh)
            valid = [kr for kr in range(slots)
                     if r0 <= row0 - rb // 2 + kr < r0 + kh and (t == 1 or 0 <= row0 - rb // 2 + kr < rows)]
            lo, hi = valid[0], valid[-1] + 1
            d0 = (row0 - rb // 2 + lo) - r + (kh - 1)
            live = [t1[:, :, d0 + k, :] for k in range(hi - lo)]
            dead = jnp.full((h, w, w), NEG_INF, t1.dtype)
            per_row.append(jnp.concatenate([dead] * lo + live + [dead] * (slots - hi), axis=-1))
        blocks.append(jnp.stack(per_row, axis=1))
    return jnp.stack(blocks, axis=0).reshape(3, h, rb * w, slots * w)


NA_HEADS_PER_STEP = 8


def _na_kernel(q_ref, kp_ref, kc_ref, kn_ref, vp_ref, vc_ref, vn_ref, kx_ref, vx_ref, bias_ref, o_ref):
    k_all = jnp.concatenate([kp_ref[0], kc_ref[0], kn_ref[0]], axis=0)
    v_all = jnp.concatenate([vp_ref[0], vc_ref[0], vn_ref[0]], axis=0)
    hq = q_ref.shape[1] // 2
    band = (NA_ROWS_PER_BLOCK // 2 + NA_KH) * GRID_W
    for h in range(NA_HEADS_PER_STEP):
        sl = slice(h * HEAD_DIM, (h + 1) * HEAD_DIM)
        kx, vx = kx_ref[0, :, sl], vx_ref[0, :, sl]
        for part in range(2):
            rows = slice(part * hq, (part + 1) * hq)
            keys = slice(part * hq, part * hq + band)
            q = q_ref[0, rows, sl]
            s = (lax.dot_general(q, k_all[keys, sl], NT_DIMS, preferred_element_type=F32)
                 + bias_ref[0, h, rows, keys].astype(F32))
            sx = lax.dot_general(q, kx, NT_DIMS, preferred_element_type=F32)
            m = jnp.maximum(jnp.max(s, axis=-1, keepdims=True), jnp.max(sx, axis=-1, keepdims=True))
            p = jnp.exp2(s - m)
            px = jnp.exp2(sx - m)
            den = jnp.sum(p, axis=-1, keepdims=True) + jnp.sum(px, axis=-1, keepdims=True)
            o = (jnp.dot(p.astype(BF16), v_all[keys, sl], preferred_element_type=F32)
                 + jnp.dot(px.astype(BF16), vx, preferred_element_type=F32))
            o_ref[0, rows, sl] = (o / den).astype(o_ref.dtype)


def _na_attention(qkv, qkv_c, bias_tab, n_heads):
    b, l, _ = qkv.shape
    lc = qkv_c.shape[1]
    hps = NA_HEADS_PER_STEP
    wd = hps * HEAD_DIM
    tq = NA_ROWS_PER_BLOCK * GRID_W
    th = tq // 2
    nb = l // tq
    nh = l // th
    ng = n_heads // hps
    assert nb >= 2 and l % tq == 0 and n_heads % hps == 0

    def btype(i):
        return jnp.where(i == 0, 0, jnp.where(i == nb - 1, 2, 1))

    def cur(off):
        return pl.BlockSpec((1, tq, wd), lambda h, bi, i: (bi, i, off + h))

    def prev(off):
        return pl.BlockSpec((1, th, wd), lambda h, bi, i: (bi, jnp.maximum(2 * i - 1, 0), off + h))

    def nxt(off):
        return pl.BlockSpec((1, th, wd), lambda h, bi, i: (bi, jnp.minimum(2 * i + 2, nh - 1), off + h))

    def ctx(off):
        return pl.BlockSpec((1, lc, wd), lambda h, bi, i: (bi, 0, off + h))

    ko, vo = ng, 2 * ng
    return pl.pallas_call(
        _na_kernel,
        out_shape=jax.ShapeDtypeStruct((b, l, n_heads * HEAD_DIM), BF16),
        grid=(ng, b, nb),
        in_specs=[cur(0), prev(ko), cur(ko), nxt(ko), prev(vo), cur(vo), nxt(vo), ctx(ko), ctx(vo),
                  pl.BlockSpec((1, hps, tq, 2 * tq), lambda h, bi, i: (btype(i), h, 0, 0))],
        out_specs=pl.BlockSpec((1, tq, wd), lambda h, bi, i: (bi, i, h)),
        compiler_params=_cparams(("parallel", "parallel", "arbitrary")),
        name="na_attention",
    )(qkv, qkv, qkv, qkv, qkv, qkv, qkv, qkv_c, qkv_c, bias_tab)


def _ctx_attn_kernel(q_ref, k_ref, v_ref, o_ref):
    s = lax.dot_general(q_ref[0], k_ref[0], NT_DIMS, preferred_element_type=F32)
    m = jnp.max(s, axis=-1, keepdims=True)
    p = jnp.exp2(s - m)
    den = jnp.sum(p, axis=-1, keepdims=True)
    o_ref[0] = (jnp.dot(p.astype(BF16), v_ref[0], preferred_element_type=F32) / den).astype(o_ref.dtype)


def _ctx_attention(qkv_c, n_heads):
    b, lc, _ = qkv_c.shape
    hd = HEAD_DIM

    def spec(off):
        return pl.BlockSpec((1, lc, hd), lambda bi, h: (bi, 0, off + h))

    return pl.pallas_call(
        _ctx_attn_kernel,
        out_shape=jax.ShapeDtypeStruct((b, lc, n_heads * hd), F32),
        grid=(b, n_heads),
        in_specs=[spec(0), spec(n_heads), spec(2 * n_heads)],
        out_specs=spec(0),
        compiler_params=_cparams(("parallel", "parallel")),
        name="ctx_attention",
    )(qkv_c, qkv_c, qkv_c)


def _cos_sin(n_out, n_in, period):
    ang = 2.0 * np.pi * ((np.arange(n_out)[:, None] * np.arange(n_in)[None, :]) % period) / period
    return np.cos(ang), np.sin(ang)


def _lane_cat(x3):
    return jnp.concatenate([x3[j] for j in range(x3.shape[0])], axis=1)


def _rows_from_lanes(re, im, nc):
    return jnp.concatenate(
        [jnp.concatenate([re[:, j * LANES:(j + 1) * LANES], im[:, j * LANES:(j + 1) * LANES]], axis=1)
         for j in range(nc)], axis=0)


def _lanes_from_rows(x, nc, r):
    re = jnp.concatenate([x[j * r:(j + 1) * r, :LANES] for j in range(nc)], axis=1)
    im = jnp.concatenate([x[j * r:(j + 1) * r, LANES:] for j in range(nc)], axis=1)
    return jnp.concatenate([re, im], axis=0)


def _fourier_tables(r, nc):
    l = r * LANES
    c1, s1 = _cos_sin(r, r, r)
    m1 = np.block([[c1, s1], [-s1, c1]])
    tc, ts = _cos_sin(r, LANES, l)
    c2, s2 = _cos_sin(LANES, LANES, LANES)
    m2 = np.concatenate([c2, s2], axis=0) / math.sqrt(l)
    return (_const_bf16(m1), jnp.asarray(np.tile(tc, (1, nc)), F32), jnp.asarray(np.tile(ts, (1, nc)), F32),
            _const_bf16(m2))


def _fourier_kernel(zr_ref, zi_ref, m1_ref, tc_ref, ts_ref, m2_ref, o_ref, *, nc, r):
    z = jnp.concatenate([_lane_cat(zr_ref[0]), _lane_cat(zi_ref[0])], axis=0)
    a = _dot1(m1_ref[...], z)
    ar, ai = a[:r], a[r:]
    tc, ts = tc_ref[...], ts_ref[...]
    br = (ar * tc + ai * ts).astype(BF16)
    bi = (ai * tc - ar * ts).astype(BF16)
    y = _dot1(_rows_from_lanes(br, bi, nc), m2_ref[...])
    for j in range(nc):
        o_ref[0, j] = y[j * r:(j + 1) * r].T.astype(o_ref.dtype)


def _fourier_latent(pt, n_ch, nc):
    b, _, l = pt.shape
    r = l // LANES
    p4 = pt.reshape(b, pt.shape[1], r, LANES)
    m1, tc, ts, m2 = _fourier_tables(r, nc)
    nblk = n_ch // nc

    def const(a):
        return pl.BlockSpec(a.shape, lambda bi, c: (0,) * a.ndim)

    out = pl.pallas_call(
        functools.partial(_fourier_kernel, nc=nc, r=r),
        out_shape=jax.ShapeDtypeStruct((b, n_ch, LANES, r), BF16),
        grid=(b, nblk),
        in_specs=[
            pl.BlockSpec((1, nc, r, LANES), lambda bi, c: (bi, c, 0, 0)),
            pl.BlockSpec((1, nc, r, LANES), lambda bi, c: (bi, nblk + c, 0, 0)),
            const(m1), const(tc), const(ts), const(m2),
        ],
        out_specs=pl.BlockSpec((1, nc, LANES, r), lambda bi, c: (bi, c, 0, 0)),
        compiler_params=_cparams(("parallel", "parallel")),
        name="fourier_latent",
    )(p4, p4, m1, tc, ts, m2)
    return out.reshape(b, n_ch, l)


def _filter_mlp_kernel(bands_ref, w1t_ref, w1c_ref, w1s_ref, b1_ref, w2_ref, b2_ref, fr_ref, o_ref, *, l, tl):
    pos = (pl.program_id(0) * tl + lax.broadcasted_iota(jnp.int32, (1, tl), 1)).astype(F32)
    t = pos / float(max(l - 1, 1))
    ang = bands_ref[...] * (2.0 * math.pi / l) * pos
    fr = fr_ref[...]
    cos_a, sin_a = jnp.cos(ang), jnp.sin(ang)
    w1c, w1s = w1c_ref[...], w1s_ref[...]
    pre = w1t_ref[...] * t
    for k in range(HY_BANDS):
        pre = pre + w1c[:, k:k + 1] * cos_a[k:k + 1, :] - w1s[:, k:k + 1] * sin_a[k:k + 1, :]
    kpad = jnp.zeros((w2_ref.shape[1] - w2_ref.shape[0], tl), F32)
    h1 = jnp.sin(fr * (pre + b1_ref[...]))
    o_ref[...] = jnp.sin(fr * (_dot3k(w2_ref[...], jnp.concatenate([h1, kpad], axis=0)) + b2_ref[...]))


def _flip_lanes(x, anti_eye):
    nb = x.shape[1] // LANES
    hi, lo = _hi_lo(x)
    blocks = []
    for b in range(nb):
        sl = slice((nb - 1 - b) * LANES, (nb - b) * LANES)
        blocks.append(jnp.dot(hi[:, sl], anti_eye, preferred_element_type=F32)
                      + jnp.dot(lo[:, sl], anti_eye, preferred_element_type=F32))
    return jnp.concatenate(blocks, axis=1)


def _filter_out_kernel(h_ref, nb_ref, eye_ref, w3_ref, dl_ref, o_ref, *, l, tl):
    i = pl.program_id(0)
    lane = lax.broadcasted_iota(jnp.int32, (1, tl), 1)
    n = i * tl + lane
    t = jnp.where(n < l, n, 2 * l - n).astype(F32) / float(max(l - 1, 1))
    tile = h_ref[...]
    rev = pltpu.roll(_flip_lanes(tile, eye_ref[...]), 1, 1)
    rev = jnp.where(lane == 0, nb_ref[:, 0:1], rev)
    h2 = jnp.where(i >= l // tl, rev, tile)
    kpad = jnp.zeros((w3_ref.shape[2] - h2.shape[0], tl), F32)
    out = _dot1(w3_ref[0], jnp.concatenate([h2, kpad], axis=0)) * jnp.exp(-t * dl_ref[...])
    o_ref[...] = jnp.where(n == l, 0.0, out).astype(o_ref.dtype)


def _hyena_filters(l, w1, b1, w2, b2, w3, freq, width):
    hid = w1.shape[1]
    kdim = -(-hid // LANES) * LANES
    pad = kdim - hid
    tl = min(2048, l)
    rows = HY_ORDER * width
    bands = np.linspace(1e-4, HY_BANDS - 1, HY_BANDS, dtype=np.float32).reshape(HY_BANDS, 1)
    deltas = np.abs(np.linspace(HY_MIN_DECAY, HY_MAX_DECAY, width, dtype=np.float32))
    dl = np.tile(deltas, HY_ORDER).reshape(rows, 1)
    w1t = w1.T
    w2t = jnp.pad(w2.T, ((0, 0), (0, pad)))
    w3d = w3.reshape(hid, HY_ORDER, 2, width).transpose(2, 1, 3, 0).reshape(2, rows, hid)
    w3d = jnp.pad(w3d, ((0, 0), (0, 0), (0, pad)))
    col = lambda v: v.reshape(hid, 1)

    def const(shape):
        return pl.BlockSpec(shape, lambda i: (0,) * len(shape))

    hidden = pl.pallas_call(
        functools.partial(_filter_mlp_kernel, l=l, tl=tl),
        out_shape=jax.ShapeDtypeStruct((hid, l), F32),
        grid=(l // tl,),
        in_specs=[const((HY_BANDS, 1)), const((hid, 1)), const((hid, HY_BANDS)), const((hid, HY_BANDS)),
                  const((hid, 1)), const((hid, kdim)), const((hid, 1)), const((hid, 1))],
        out_specs=pl.BlockSpec((hid, tl), lambda i: (0, i)),
        compiler_params=_cparams(("parallel",)),
        name="hyena_filter_mlp",
    )(jnp.asarray(bands), w1t[:, 0:1], w1t[:, 1:1 + HY_BANDS], w1t[:, 1 + HY_BANDS:], col(b1), w2t, col(b2),
      col(freq))
    nt = l // tl
    anti_eye = _const_bf16(np.eye(LANES)[::-1])
    return pl.pallas_call(
        functools.partial(_filter_out_kernel, l=l, tl=tl),
        out_shape=jax.ShapeDtypeStruct((rows, 2 * l), BF16),
        grid=(2 * nt,),
        in_specs=[pl.BlockSpec((hid, tl), lambda i: (0, jnp.where(i < nt, i, 2 * nt - 1 - i))),
                  pl.BlockSpec((hid, tl), lambda i: (0, jnp.clip(2 * nt - i, 0, nt - 1))),
                  const((LANES, LANES)),
                  pl.BlockSpec((1, rows, kdim), lambda i: (i // nt, 0, 0)),
                  const((rows, 1))],
        out_specs=pl.BlockSpec((rows, tl), lambda i: (0, i)),
        compiler_params=_cparams(("parallel",)),
        name="hyena_filters",
    )(hidden, hidden, anti_eye, w3d, jnp.asarray(dl))


def _conv_tables(r, nc):
    r2 = 2 * r
    n = r2 * LANES
    c1, s1 = _cos_sin(r2, r2, r2)
    c1h, s1h = c1[:, :r], s1[:, :r]
    m1 = np.block([[c1h, s1h], [-s1h, c1h]])
    m1_real = np.concatenate([c1, -s1], axis=0)
    tc, ts = _cos_sin(r2, LANES, n)
    c2, s2 = _cos_sin(LANES, LANES, LANES)
    m2 = np.block([[c2, -s2], [s2, c2]])
    m2i = np.block([[c2, s2], [-s2, c2]])
    ct, st = c1h.T, s1h.T
    m1i = np.block([[ct, -st], [st, ct]]) / n
    return dict(
        m1=_const_bf16(m1), m1_real=_const_bf16(m1_real), m2=_const_bf16(m2), m2i=_const_bf16(m2i),
        m1i=_const_bf16(m1i),
        tc_l=_const_bf16(np.tile(tc, (1, nc))), ts_l=_const_bf16(np.tile(ts, (1, nc))),
        tc_r=_const_bf16(np.tile(tc, (nc, 1))), ts_r=_const_bf16(np.tile(ts, (nc, 1))))


def _kernel_spectrum(k3, m1_real, tc_l, ts_l, m2, nc, r2):
    a = _dot1(m1_real, _lane_cat(k3)).astype(BF16)
    ar, ai = a[:r2], a[r2:]
    return _dot1(_rows_from_lanes(ar * tc_l + ai * ts_l, ai * tc_l - ar * ts_l, nc), m2).astype(BF16)


def _shift_tokens(t, r, direction):
    lane = lax.broadcasted_iota(jnp.int32, t.shape, 2)
    row = lax.broadcasted_iota(jnp.int32, t.shape, 1)
    if direction < 0:
        near = pltpu.roll(t, 1, 2)
        wrap = pltpu.roll(near, 1, 1)
        edge = lane == 0
        dead = edge & (row == 0)
    else:
        near = pltpu.roll(t, LANES - 1, 2)
        wrap = pltpu.roll(near, r - 1, 1)
        edge = lane == LANES - 1
        dead = edge & (row == r - 1)
    return jnp.where(dead, 0.0, jnp.where(edge, wrap, near))


def _long_conv(zr, zi, kspec, m1, m2, m2i, m1i, tc_l, ts_l, tc_r, ts_r, nc, r):
    r2 = 2 * r
    a = _dot1(m1, jnp.concatenate([zr.astype(BF16), zi.astype(BF16)], axis=0)).astype(BF16)
    ar, ai = a[:r2], a[r2:]
    x = _dot1(_rows_from_lanes(ar * tc_l + ai * ts_l, ai * tc_l - ar * ts_l, nc), m2).astype(BF16)
    xr, xi = x[:, :LANES], x[:, LANES:]
    kr, ki = kspec[:, :LANES], kspec[:, LANES:]
    bb = _dot1(jnp.concatenate([xr * kr - xi * ki, xr * ki + xi * kr], axis=1), m2i).astype(BF16)
    pr, pi = bb[:, :LANES], bb[:, LANES:]
    q = jnp.concatenate([pr * tc_r - pi * ts_r, pi * tc_r + pr * ts_r], axis=1)
    out = _dot1(m1i, _lanes_from_rows(q, nc, r2))
    return out[:r], out[r:]


def _hyena_kernel(v0, v1, a0, a1, b0, b1, k0_ref, k1_ref, cw_ref, cb_ref, bias_ref,
                  m1_ref, m2_ref, m2i_ref, m1i_ref, tcl_ref, tsl_ref, tcr_ref, tsr_ref, m1r_ref, o_ref, *, nc, r):
    tabs = (m1_ref[...], m2_ref[...], m2i_ref[...], m1i_ref[...],
            tcl_ref[...], tsl_ref[...], tcr_ref[...], tsr_ref[...])

    def spectrum(k_ref):
        return _kernel_spectrum(k_ref[...], m1r_ref[...], tcl_ref[...], tsl_ref[...], m2_ref[...], nc, 2 * r)

    def short_conv(ref, part):
        t = ref[0].astype(F32)
        w = cw_ref[part]
        return (_lane_cat(_shift_tokens(t, r, -1)) * w[0:1] + _lane_cat(t) * w[1:2]
                + _lane_cat(_shift_tokens(t, r, +1)) * w[2:3] + cb_ref[part])

    vr, vi = short_conv(v0, 0), short_conv(v1, 0)
    x1r, x1i = short_conv(a0, 1), short_conv(a1, 1)
    x2r, x2i = short_conv(b0, 2), short_conv(b1, 2)
    bias = bias_ref[...]
    yr, yi = _long_conv(vr, vi, spectrum(k0_ref), *tabs, nc, r)
    zr = x1r * (yr + vr * bias[0:1])
    zi = x1i * (yi + vi * bias[0:1])
    yr, yi = _long_conv(zr, zi, spectrum(k1_ref), *tabs, nc, r)
    outr = x2r * (yr + zr * bias[1:2])
    outi = x2i * (yi + zi * bias[1:2])
    for j in range(nc):
        o_ref[0, j] = outr[:, j * LANES:(j + 1) * LANES].astype(o_ref.dtype)
        o_ref[1, j] = outi[:, j * LANES:(j + 1) * LANES].astype(o_ref.dtype)


def _hyena_latent(pt, ch0, width, k2t, conv_w, conv_b, bias, tabs, nc):
    b, c_all, l = pt.shape
    assert b == 2, "the two batch entries ride one complex transform"
    r = l // LANES
    r2 = 2 * r
    p4 = pt.reshape(b, c_all, r, LANES)
    k3 = k2t.reshape(k2t.shape[0], r2, LANES)
    nblk = width // nc
    rep = lambda a: jnp.repeat(a, LANES, axis=-1)
    cw = rep(conv_w.reshape(3, 3, width).transpose(1, 0, 2))
    cb = rep(conv_b.reshape(3, 1, width))
    bs = rep(bias)

    def inp(bi, part):
        off = (ch0 + part * width) // nc
        return pl.BlockSpec((1, nc, r, LANES), lambda c: (bi, off + c, 0, 0))

    def const(a):
        return pl.BlockSpec(a.shape, lambda c: (0,) * a.ndim)

    names = ("m1", "m2", "m2i", "m1i", "tc_l", "ts_l", "tc_r", "ts_r", "m1_real")
    consts = [tabs[k] for k in names]
    out = pl.pallas_call(
        functools.partial(_hyena_kernel, nc=nc, r=r),
        out_shape=jax.ShapeDtypeStruct((b, width, r, LANES), BF16),
        grid=(nblk,),
        in_specs=[inp(0, 0), inp(1, 0), inp(0, 1), inp(1, 1), inp(0, 2), inp(1, 2),
                  pl.BlockSpec((nc, r2, LANES), lambda c: (c, 0, 0)),
                  pl.BlockSpec((nc, r2, LANES), lambda c: (nblk + c, 0, 0)),
                  pl.BlockSpec((3, 3, nc * LANES), lambda c: (0, 0, c)),
                  pl.BlockSpec((3, 1, nc * LANES), lambda c: (0, 0, c)),
                  pl.BlockSpec((HY_ORDER, nc * LANES), lambda c: (0, c))]
                 + [const(a) for a in consts],
        out_specs=pl.BlockSpec((b, nc, r, LANES), lambda c: (0, c, 0, 0)),
        compiler_params=_cparams(("parallel",)),
        name="hyena_latent",
    )(p4, p4, p4, p4, p4, p4, k3, k3, cw, cb, bs, *consts)
    return out.reshape(b, width, l)


def _ctx_fourier_kernel(zr_ref, zi_ref, m_ref, o_ref):
    z = jnp.concatenate([zr_ref[0], zi_ref[0]], axis=1).astype(F32)
    o_ref[0] = _dot_data_const(z, m_ref[...])


def _ctx_fourier(ptc, n_ch):
    b, _, lc = ptc.shape
    c, s = _cos_sin(lc, lc, lc)
    m = _const_rhs3(np.concatenate([c, s], axis=0) / math.sqrt(lc))
    return pl.pallas_call(
        _ctx_fourier_kernel,
        out_shape=jax.ShapeDtypeStruct((b, n_ch, lc), F32),
        grid=(b,),
        in_specs=[pl.BlockSpec((1, n_ch, lc), lambda bi: (bi, 0, 0)),
                  pl.BlockSpec((1, n_ch, lc), lambda bi: (bi, 1, 0)),
                  pl.BlockSpec(m.shape, lambda bi: (0, 0))],
        out_specs=pl.BlockSpec((1, n_ch, lc), lambda bi: (bi, 0, 0)),
        compiler_params=_cparams(("parallel",)),
        name="ctx_fourier",
    )(ptc, ptc, m)


def _ctx_hyena_kernel(v0, v1, a0, a1, b0, b1, k_ref, cw_ref, cb_ref, bias_ref, mk_ref, mf_ref, mi_ref,
                      o_ref, *, lc, width):
    def short_conv(ref, part):
        t = ref[0].astype(F32)
        lane = lax.broadcasted_iota(jnp.int32, t.shape, 1)
        prv = jnp.where(lane == 0, 0.0, pltpu.roll(t, 1, 1))
        nxt = jnp.where(lane == lc - 1, 0.0, pltpu.roll(t, lc - 1, 1))
        w = cw_ref[part]
        return prv * w[:, 0:1] + t * w[:, 1:2] + nxt * w[:, 2:3] + cb_ref[part]

    kspec = _dot_data_const(k_ref[...].astype(F32), mk_ref[...])
    n = 2 * lc

    def long_conv(zr, zi, ks):
        x = _dot_data_const(jnp.concatenate([zr, zi], axis=1), mf_ref[...])
        xr, xi = x[:, :n], x[:, n:]
        kr, ki = ks[:, :n], ks[:, n:]
        y = jnp.concatenate([xr * kr - xi * ki, xr * ki + xi * kr], axis=1)
        out = _dot_data_const(y, mi_ref[...])
        return out[:, :lc], out[:, lc:]

    vr, vi = short_conv(v0, 0), short_conv(v1, 0)
    x1r, x1i = short_conv(a0, 1), short_conv(a1, 1)
    x2r, x2i = short_conv(b0, 2), short_conv(b1, 2)
    bias = bias_ref[...]
    yr, yi = long_conv(vr, vi, kspec[:width])
    zr = x1r * (yr + vr * bias[:, 0:1])
    zi = x1i * (yi + vi * bias[:, 0:1])
    yr, yi = long_conv(zr, zi, kspec[width:])
    o_ref[0] = x2r * (yr + zr * bias[:, 1:2])
    o_ref[1] = x2i * (yi + zi * bias[:, 1:2])


def _ctx_hyena(ptc, ch0, width, k2t, conv_w, conv_b, bias):
    b, _, lc = ptc.shape
    assert b == 2
    n = 2 * lc
    c, s = _cos_sin(n, n, n)
    mk = _const_rhs3(np.concatenate([c, -s], axis=1))
    ch, sh = c[:lc], s[:lc]
    mf = _const_rhs3(np.block([[ch, -sh], [sh, ch]]))
    ci, si = c[:, :lc], s[:, :lc]
    mi = _const_rhs3(np.block([[ci, si], [-si, ci]]) / n)
    cw = conv_w.reshape(3, 3, width).transpose(1, 2, 0)
    cb = conv_b.reshape(3, width, 1)

    def inp(bi, part):
        return pl.BlockSpec((1, width, lc), lambda i: (bi, ch0 // width + part, 0))

    def const(a):
        return pl.BlockSpec(a.shape, lambda i: (0,) * a.ndim)

    args = (k2t, cw, cb, bias.T, mk, mf, mi)
    return pl.pallas_call(
        functools.partial(_ctx_hyena_kernel, lc=lc, width=width),
        out_shape=jax.ShapeDtypeStruct((b, width, lc), F32),
        grid=(1,),
        in_specs=[inp(0, 0), inp(1, 0), inp(0, 1), inp(1, 1), inp(0, 2), inp(1, 2)] + [const(a) for a in args],
        out_specs=pl.BlockSpec((b, width, lc), lambda i: (0, 0, 0)),
        compiler_params=_cparams(("arbitrary",)),
        name="ctx_hyena",
    )(ptc, ptc, ptc, ptc, ptc, ptc, *args)


def _merge_kernel(a_ref, f_ref, hy_ref, x_ref, gate_ref, ga_ref, gf_ref, gh_ref, wa_ref, wf_ref, wh_ref, o_ref):
    a = a_ref[0].astype(F32)
    ya = a * lax.rsqrt(jnp.mean(a * a, axis=-1, keepdims=True) + EPS) * ga_ref[...]
    acc = jnp.dot(ya.astype(BF16), wa_ref[...], preferred_element_type=F32)

    def cm_part(ref, g_ref, w_ref):
        t = ref[0].astype(F32)
        y = t * lax.rsqrt(jnp.mean(t * t, axis=0, keepdims=True) + EPS) * g_ref[...]
        return jnp.dot(y.T.astype(BF16), w_ref[...], preferred_element_type=F32)

    acc = acc + cm_part(f_ref, gf_ref, wf_ref) + cm_part(hy_ref, gh_ref, wh_ref)
    o_ref[0] = x_ref[0] + gate_ref[0] * acc


def _merge_out(a, ft, ht, x, gate, g, w_all, layer, tm):
    b, l, d = x.shape
    wa_n, wf_n, wh_n = a.shape[2], ft.shape[1], ht.shape[1]
    assert wa_n % wf_n == 0 and wf_n == wh_n
    tm = min(tm, l)
    ga = g[:wa_n].reshape(1, wa_n)
    gf = g[wa_n:wa_n + wf_n].reshape(wf_n, 1)
    gh = g[wa_n + wf_n:].reshape(wh_n, 1)

    def const(arr):
        return pl.BlockSpec(arr.shape, lambda bi, i: (0,) * arr.ndim)

    def w_rows(n, blk):
        return pl.BlockSpec((None, n, d), lambda bi, i: (layer, blk, 0))

    return pl.pallas_call(
        _merge_kernel,
        out_shape=jax.ShapeDtypeStruct((b, l, d), F32),
        grid=(b, l // tm),
        in_specs=[pl.BlockSpec((1, tm, wa_n), lambda bi, i: (bi, i, 0)),
                  pl.BlockSpec((1, wf_n, tm), lambda bi, i: (bi, 0, i)),
                  pl.BlockSpec((1, wh_n, tm), lambda bi, i: (bi, 0, i)),
                  pl.BlockSpec((1, tm, d), lambda bi, i: (bi, i, 0)),
                  pl.BlockSpec((1, 1, d), lambda bi, i: (bi, 0, 0)),
                  const(ga), const(gf), const(gh),
                  w_rows(wa_n, 0), w_rows(wf_n, wa_n // wf_n), w_rows(wh_n, wa_n // wf_n + 1)],
        out_specs=pl.BlockSpec((1, tm, d), lambda bi, i: (bi, i, 0)),
        compiler_params=_cparams(("parallel", "parallel")),
        name="merge_out",
    )(a, ft, ht, x, gate.reshape(b, 1, d), ga, gf, gh, w_all, w_all, w_all)


HALO = 16


def _ffn_kernel(xp_ref, x_ref, xn_ref, g_ref, sh_ref, sc_ref, gate_ref, wg_ref, wu_ref, cw_ref, cb_ref, wd_ref,
                fg_ref, o_ref, h_ref, *, tm, final_norm):
    i = pl.program_id(1)
    c = pl.program_id(2)
    last_tile = pl.num_programs(1) - 1

    n_ext = tm + 2 * HALO

    def down_partial(ge, up):
        row = lax.broadcasted_iota(jnp.int32, ge.shape, 0)
        outside = ((row < HALO) & (i == 0)) | ((row >= HALO + tm) & (i == last_tile))
        ge = jnp.where(outside, 0.0, ge)
        gp = pltpu.roll(ge, 1, 0)[HALO:HALO + tm]
        gn = pltpu.roll(ge, n_ext - 1, 0)[HALO:HALO + tm]
        cw = cw_ref[...]
        conv = gp * cw[0:1] + ge[HALO:HALO + tm] * cw[1:2] + gn * cw[2:3] + cb_ref[...]
        inner = 0.7978845608028654 * (conv + 0.044715 * (conv * conv * conv))
        act = 0.5 * conv * (1.0 + jnp.tanh(inner)) * up
        return jnp.dot(act.astype(BF16), wd_ref[...], preferred_element_type=F32)

    @pl.when(c == 0)
    def _():
        g, sh, sc = g_ref[...], sh_ref[0], sc_ref[0]
        wg, wu = wg_ref[...], wu_ref[...]
        rs = tm // NORM_SLABS
        bounds = [(0, HALO)] + [(HALO + s * rs, HALO + (s + 1) * rs) for s in range(NORM_SLABS)] + [(HALO + tm, n_ext)]
        ge_parts, up_parts = [], []
        for lo, hi in bounds:
            if lo == 0:
                xs = xp_ref[0, 0]
            elif hi == n_ext:
                xs = xn_ref[0, 0]
            else:
                xs = x_ref[0, lo - HALO:hi - HALO, :]
            h = _norm_mod(xs, g, sh, sc).astype(BF16)
            h_ref[lo:hi, :] = h
            ge_parts.append(jnp.dot(h, wg, preferred_element_type=F32))
            if lo != 0 and hi != n_ext:
                up_parts.append(jnp.dot(h, wu, preferred_element_type=F32))
        o_ref[0] = down_partial(jnp.concatenate(ge_parts, axis=0), jnp.concatenate(up_parts, axis=0))

    @pl.when(c > 0)
    def _():
        h = h_ref[...]
        ge = jnp.dot(h, wg_ref[...], preferred_element_type=F32)
        up = jnp.dot(h[HALO:HALO + tm], wu_ref[...], preferred_element_type=F32)
        o_ref[0] += down_partial(ge, up)

    @pl.when(c == pl.num_programs(2) - 1)
    def _():
        y = x_ref[0] + gate_ref[0] * o_ref[0]
        if final_norm:
            y = y * lax.rsqrt(jnp.mean(y * y, axis=-1, keepdims=True) + EPS) * fg_ref[...]
        o_ref[0] = y


def _ffn(x, g, sh, sc, gate, w_up, conv_w, conv_b, w_down, layer, final_g, tm, tf, *, final_norm):
    b, l, d = x.shape
    dff = w_down.shape[1]
    tm = min(tm, l)
    nch = dff // tf
    x4 = x.reshape(b, l // HALO, HALO, d)
    per_tile = tm // HALO
    nhalo = l // HALO
    vec = pl.BlockSpec((1, 1, d), lambda bi, i, c: (bi, 0, 0))
    return pl.pallas_call(
        functools.partial(_ffn_kernel, tm=tm, final_norm=final_norm),
        out_shape=jax.ShapeDtypeStruct((b, l, d), F32),
        grid=(b, l // tm, nch),
        in_specs=[
            pl.BlockSpec((1, 1, HALO, d), lambda bi, i, c: (bi, jnp.maximum(i * per_tile - 1, 0), 0, 0)),
            pl.BlockSpec((1, tm, d), lambda bi, i, c: (bi, i, 0)),
            pl.BlockSpec((1, 1, HALO, d), lambda bi, i, c: (bi, jnp.minimum((i + 1) * per_tile, nhalo - 1), 0, 0)),
            pl.BlockSpec((1, d), lambda bi, i, c: (0, 0)),
            vec, vec, vec,
            pl.BlockSpec((None, d, tf), lambda bi, i, c: (layer, 0, c)),
            pl.BlockSpec((None, d, tf), lambda bi, i, c: (layer, 0, nch + c)),
            pl.BlockSpec((3, tf), lambda bi, i, c: (0, c)),
            pl.BlockSpec((1, tf), lambda bi, i, c: (0, c)),
            pl.BlockSpec((None, tf, d), lambda bi, i, c: (layer, c, 0)),
            pl.BlockSpec((1, d), lambda bi, i, c: (0, 0)),
        ],
        out_specs=pl.BlockSpec((1, tm, d), lambda bi, i, c: (bi, i, 0)),
        scratch_shapes=[pltpu.VMEM((tm + 2 * HALO, d), BF16)],
        compiler_params=_cparams(("parallel", "parallel", "arbitrary")),
        name="ffn",
    )(x4, x, x4, g.reshape(1, d), sh.reshape(b, 1, d), sc.reshape(b, 1, d), gate.reshape(b, 1, d),
      w_up, w_up, conv_w, conv_b.reshape(1, dff), w_down, final_g.reshape(1, d))


TOKEN_TILE = 512
PROJ_TOKEN_TILE = 1024
PROJ_TOK_COL_TILE = 1024
PROJ_CM_COL_TILE = 1280
FFN_TOKEN_TILE = 1024
FFN_COL_TILE = 512
MIX_CHANNELS = 8
FOURIER_CHANNELS = 32


def kernel(x, c, ctx, c_ctx, ada_w, ada_b, norm1_g, norm2_g, w_in, na_rpb, hy_conv_w, hy_conv_b, hy_w1, hy_b1,
           hy_w2, hy_b2, hy_w3, hy_freq, hy_bias, mix_norm_g, w_out, ffn_w_up, ffn_conv_w, ffn_conv_b, ffn_w_down,
           final_norm_g):
    b, l, d = x.shape
    lc = ctx.shape[1]
    depth = ada_w.shape[0]
    na_w, fn_w = d // 2, d // 4
    hy_w = d - na_w - fn_w
    n_heads = na_w // HEAD_DIM
    qkv_w = 3 * na_w
    rows = l // GRID_W
    assert b + 1 <= 8 and l % (NA_ROWS_PER_BLOCK * GRID_W) == 0 and rows >= 2 * NA_ROWS_PER_BLOCK
    r = l // LANES
    nc = MIX_CHANNELS
    conv_tabs = _conv_tables(r, nc)

    s_in = jnp.concatenate([c, c_ctx[None], jnp.zeros((8 - b - 1, d), F32)], axis=0)
    mod_all = _ada_mod(s_in, ada_w, ada_b)
    xc = ctx
    w_out_bf, w_up_bf, w_down_bf = w_out.astype(BF16), ffn_w_up.astype(BF16), ffn_w_down.astype(BF16)
    for layer in range(depth):
        update_ctx = layer < depth - 1
        mod = mod_all[layer]
        sh1, sc1, g1, sh2, sc2, g2 = jnp.split(mod[:b], N_MOD, axis=-1)
        csh1, csc1, cg1, csh2, csc2, cg2 = jnp.split(jnp.broadcast_to(mod[b:b + 1], (b, N_MOD * d)), N_MOD, axis=-1)

        wl = w_in[layer]
        w_fold = _fold_fourier_weights(wl[:, qkv_w:qkv_w + fn_w])
        w_tok = jnp.concatenate([wl[:, :na_w] * (HEAD_DIM ** -0.5 * LOG2E), wl[:, na_w:qkv_w]], axis=1).astype(BF16)
        w_cm = jnp.concatenate([w_fold[0], w_fold[1], wl[:, qkv_w + fn_w:]], axis=1).astype(BF16)
        hy0 = 2 * fn_w

        tiles = dict(tm=PROJ_TOKEN_TILE, tn_tok=PROJ_TOK_COL_TILE, tn_cm=PROJ_CM_COL_TILE)
        qkv, pt = _proj(x, norm1_g[layer], sh1, sc1, w_tok, w_cm, **tiles)
        qkv_c, ptc = _proj(xc, norm1_g[layer], csh1, csc1, w_tok, w_cm, **tiles)

        a = _na_attention(qkv, qkv_c, _na_bias_table(na_rpb[layer], rows), n_heads)
        yf = _fourier_latent(pt, fn_w, FOURIER_CHANNELS)
        filt = (hy_w1[layer], hy_b1[layer], hy_w2[layer], hy_b2[layer], hy_w3[layer], hy_freq[layer])
        yh = _hyena_latent(pt, hy0, hy_w, _hyena_filters(l, *filt, hy_w), hy_conv_w[layer], hy_conv_b[layer],
                           hy_bias[layer], conv_tabs, nc)
        x_new = _merge_out(a, yf, yh, x, g1, mix_norm_g[layer], w_out_bf, layer, TOKEN_TILE)

        if update_ctx:
            ac = _ctx_attention(qkv_c, n_heads)
            yfc = _ctx_fourier(ptc, fn_w)
            yhc = _ctx_hyena(ptc, hy0, hy_w, _hyena_filters(lc, *filt, hy_w), hy_conv_w[layer], hy_conv_b[layer],
                             hy_bias[layer])
            xc = _merge_out(ac, yfc, yhc, xc, cg1, mix_norm_g[layer], w_out_bf, layer, TOKEN_TILE)
        x = x_new

        x = _ffn(x, norm2_g[layer], sh2, sc2, g2, w_up_bf, ffn_conv_w[layer], ffn_conv_b[layer], w_down_bf, layer,
                 final_norm_g, FFN_TOKEN_TILE, FFN_COL_TILE, final_norm=not update_ctx)
        if update_ctx:
            xc = _ffn(xc, norm2_g[layer], csh2, csc2, cg2, w_up_bf, ffn_conv_w[layer], ffn_conv_b[layer], w_down_bf,
                      layer, final_norm_g, FFN_TOKEN_TILE, FFN_COL_TILE, final_norm=False)
    return x
```

```python
import functools
import math

import numpy as np
import jax
import jax.numpy as jnp
from jax import lax
from jax.experimental import pallas as pl
from jax.experimental.pallas import tpu as pltpu

F32 = jnp.float32
BF16 = jnp.bfloat16

EPS = 1e-6
HEAD_DIM = 128
GRID_W = 64
NA_KH = 8
NA_KW = 16
NA_ROWS_PER_BLOCK = 8
FN_GROUP_DIM = 128
HY_ORDER = 2
HY_BANDS = 16
HY_FAST_DECAY = 0.3
HY_SLOW_DECAY = 1.5
HY_TARGET = 1e-2
HY_MIN_DECAY = math.log(HY_TARGET) / HY_SLOW_DECAY
HY_MAX_DECAY = math.log(HY_TARGET) / HY_FAST_DECAY
N_MOD = 6
LANES = 128
NEG_INF = -1e30
LOG2E = math.log2(math.e)
VMEM_LIMIT = 60 * 1024 * 1024

NT_DIMS = (((1,), (1,)), ((), ()))


def _cparams(sem):
    return pltpu.CompilerParams(dimension_semantics=sem, vmem_limit_bytes=VMEM_LIMIT)


def _hi_lo(a):
    hi = a.astype(BF16)
    lo = (a - hi.astype(F32)).astype(BF16)
    return hi, lo


def _np_hi_lo(m):
    m = np.asarray(m, np.float32)
    hi = m.astype(BF16)
    lo = (m - hi.astype(np.float32)).astype(BF16)
    return hi, lo


def _const_rhs3(m):
    hi, lo = _np_hi_lo(m)
    return jnp.asarray(np.concatenate([hi, hi, lo], axis=0))


def _dot_data_const(a, c3):
    hi, lo = _hi_lo(a)
    return jnp.dot(jnp.concatenate([hi, lo, hi], axis=1), c3, preferred_element_type=F32)


def _dot3k(a, b):
    ah, al = _hi_lo(a)
    bh, bl = _hi_lo(b)
    return jnp.dot(jnp.concatenate([ah, al, ah], axis=1), jnp.concatenate([bh, bh, bl], axis=0),
                   preferred_element_type=F32)


def _const_bf16(m):
    return jnp.asarray(np.asarray(m, np.float32).astype(BF16))


def _dot1(a, b):
    return jnp.dot(a.astype(BF16), b.astype(BF16), preferred_element_type=F32)


def _ada_kernel(s_ref, w_ref, b_ref, o_ref):
    s = s_ref[...]
    s = s / (1.0 + jnp.exp(-s))
    o_ref[0] = _dot1(s, w_ref[0]) + b_ref[0]


def _ada_mod(s_in, ada_w, ada_b):
    depth, d, n = ada_w.shape
    tn = 1024
    return pl.pallas_call(
        _ada_kernel,
        out_shape=jax.ShapeDtypeStruct((depth, 8, n), F32),
        grid=(depth, n // tn),
        in_specs=[
            pl.BlockSpec((8, d), lambda l, j: (0, 0)),
            pl.BlockSpec((1, d, tn), lambda l, j: (l, 0, j)),
            pl.BlockSpec((1, 1, tn), lambda l, j: (l, 0, j)),
        ],
        out_specs=pl.BlockSpec((1, 8, tn), lambda l, j: (l, 0, j)),
        compiler_params=_cparams(("parallel", "parallel")),
        name="ada_mod",
    )(s_in, ada_w, ada_b.reshape(depth, 1, n))


def _norm_mod(x, g, sh, sc):
    ms = jnp.mean(x * x, axis=-1, keepdims=True)
    return (x * lax.rsqrt(ms + EPS) * g) * (1.0 + sc) + sh


NORM_SLABS = 4


def _proj_kernel(*refs, n_tm):
    x_refs = refs[:NORM_SLABS]
    g_ref, sh_ref, sc_ref, wa_ref, wb_ref, otm_ref, ocm_ref, h_ref = refs[NORM_SLABS:]
    j = pl.program_id(2)
    rs = x_refs[0].shape[1]

    @pl.when(j == 0)
    def _():
        g, sh, sc = g_ref[...], sh_ref[0], sc_ref[0]
        w = wa_ref[...]
        for s in range(NORM_SLABS):
            h = _norm_mod(x_refs[s][0], g, sh, sc).astype(BF16)
            h_ref[s * rs:(s + 1) * rs, :] = h
            otm_ref[0, s * rs:(s + 1) * rs, :] = jnp.dot(h, w, preferred_element_type=F32).astype(otm_ref.dtype)

    @pl.when((j > 0) & (j < n_tm))
    def _():
        otm_ref[0] = jnp.dot(h_ref[...], wa_ref[...], preferred_element_type=F32).astype(otm_ref.dtype)

    @pl.when(j >= n_tm)
    def _():
        y = jnp.dot(h_ref[...], wb_ref[...], preferred_element_type=F32)
        ocm_ref[0] = y.T.astype(ocm_ref.dtype)


def _proj(x, g, sh, sc, w_tok, w_cm, *, tm, tn_tok, tn_cm):
    b, l, d = x.shape
    n_tok, n_ch = w_tok.shape[1], w_cm.shape[1]
    tm = min(tm, l)
    n_tm, n_cm = n_tok // tn_tok, n_ch // tn_cm
    assert n_tm * tn_tok == n_tok and n_cm * tn_cm == n_ch and tm % (16 * NORM_SLABS) == 0
    vec = pl.BlockSpec((1, 1, d), lambda bi, i, j: (bi, 0, 0))
    rs = tm // NORM_SLABS
    nt, nj = l // tm, n_tm + n_cm
    n_slabs = b * l // rs
    assert nj >= NORM_SLABS

    def slab(s):
        def index(bi, i, j):
            tile = bi * nt + i + (j >= nj - s).astype(jnp.int32)
            return (jnp.minimum(tile * NORM_SLABS + s, n_slabs - 1), 0, 0)
        return pl.BlockSpec((1, rs, d), index)

    tok_j = lambda j: jnp.minimum(j, n_tm - 1)
    cm_j = lambda j: jnp.maximum(j - n_tm, 0)
    return pl.pallas_call(
        functools.partial(_proj_kernel, n_tm=n_tm),
        out_shape=(jax.ShapeDtypeStruct((b, l, n_tok), BF16), jax.ShapeDtypeStruct((b, n_ch, l), BF16)),
        grid=(b, nt, nj),
        in_specs=[slab(s) for s in range(NORM_SLABS)] + [
            pl.BlockSpec((1, d), lambda bi, i, j: (0, 0)),
            vec, vec,
            pl.BlockSpec((d, tn_tok), lambda bi, i, j: (0, tok_j(j))),
            pl.BlockSpec((d, tn_cm), lambda bi, i, j: (0, cm_j(j))),
        ],
        out_specs=(pl.BlockSpec((1, tm, tn_tok), lambda bi, i, j: (bi, i, tok_j(j))),
                   pl.BlockSpec((1, tn_cm, tm), lambda bi, i, j: (bi, cm_j(j), i))),
        scratch_shapes=[pltpu.VMEM((tm, d), BF16)],
        compiler_params=_cparams(("parallel", "parallel", "arbitrary")),
        name="proj_in",
    )(*([x.reshape(n_slabs, rs, d)] * NORM_SLABS), g.reshape(1, d), sh.reshape(b, 1, d), sc.reshape(b, 1, d),
      w_tok, w_cm)


def _fold_kernel(w_ref, m_ref, o_ref):
    o_ref[0] = _dot_data_const(w_ref[...], m_ref[...])


def _fold_fourier_weights(w_f):
    d, width = w_f.shape
    gd = FN_GROUP_DIM
    idx = np.arange(gd)
    ang = 2.0 * np.pi * np.outer(idx, idx) / gd
    mat = np.concatenate([np.cos(ang), -np.sin(ang)], axis=1) / math.sqrt(gd)
    groups = width // gd
    return pl.pallas_call(
        _fold_kernel,
        out_shape=jax.ShapeDtypeStruct((2, d, width), F32),
        grid=(groups, 2),
        in_specs=[
            pl.BlockSpec((d, gd), lambda g, p: (0, g)),
            pl.BlockSpec((3 * gd, gd), lambda g, p: (0, p)),
        ],
        out_specs=pl.BlockSpec((1, d, gd), lambda g, p: (p, 0, g)),
        compiler_params=_cparams(("parallel", "parallel")),
        name="fold_fourier",
    )(w_f, _const_rhs3(mat))


def _na_bias_table(rpb, rows):
    h = rpb.shape[0]
    w, kw, kh, rb = GRID_W, NA_KW, NA_KH, NA_ROWS_PER_BLOCK
    col = np.arange(w)
    cs = np.clip(col - kw // 2, 0, w - kw)
    kc = np.arange(w)[None, :]
    col_ok = (kc >= cs[:, None]) & (kc < cs[:, None] + kw)
    padded = jnp.pad((rpb * LOG2E).astype(BF16), ((0, 0), (0, 0), (w - kw, w - kw)))
    t1 = jnp.stack([padded[:, :, w - 1 - q:2 * w - 1 - q] for q in range(w)], axis=1)
    t1 = jnp.where(col_ok[None, :, None, :], t1, NEG_INF)
    slots = 2 * rb
    blocks = []
    for t, row0 in enumerate((0, rb, rows - rb)):
        per_row = []
        for qr in range(rb):
            r = row0 + qr
            r0 = r - kh // 2 if t == 1 else min(max(r - kh // 2, 0), rows - kh)
            valid = [kr for kr in range(slots)
                     if r0 <= row0 - rb // 2 + kr < r0 + kh and (t == 1 or 0 <= row0 - rb // 2 + kr < rows)]
            lo, hi = valid[0], valid[-1] + 1
            d0 = (row0 - rb // 2 + lo) - r + (kh - 1)
            live = [t1[:, :, d0 + k, :] for k in range(hi - lo)]
            dead = jnp.full((h, w, w), NEG_INF, t1.dtype)
            per_row.append(jnp.concatenate([dead] * lo + live + [dead] * (slots - hi), axis=-1))
        blocks.append(jnp.stack(per_row, axis=1))
    return jnp.stack(blocks, axis=0).reshape(3, h, rb * w, slots * w)


NA_HEADS_PER_STEP = 8


def _na_kernel(q_ref, kp_ref, kc_ref, kn_ref, vp_ref, vc_ref, vn_ref, kx_ref, vx_ref, bias_ref, o_ref):
    k_all = jnp.concatenate([kp_ref[0], kc_ref[0], kn_ref[0]], axis=0)
    v_all = jnp.concatenate([vp_ref[0], vc_ref[0], vn_ref[0]], axis=0)
    hq = q_ref.shape[1] // 2
    band = (NA_ROWS_PER_BLOCK // 2 + NA_KH) * GRID_W
    for h in range(NA_HEADS_PER_STEP):
        sl = slice(h * HEAD_DIM, (h + 1) * HEAD_DIM)
        kx, vx = kx_ref[0, :, sl], vx_ref[0, :, sl]
        for part in range(2):
            rows = slice(part * hq, (part + 1) * hq)
            keys = slice(part * hq, part * hq + band)
            q = q_ref[0, rows, sl]
            s = (lax.dot_general(q, k_all[keys, sl], NT_DIMS, preferred_element_type=F32)
                 + bias_ref[0, h, rows, keys].astype(F32))
            sx = lax.dot_general(q, kx, NT_DIMS, preferred_element_type=F32)
            m = jnp.maximum(jnp.max(s, axis=-1, keepdims=True), jnp.max(sx, axis=-1, keepdims=True))
            p = jnp.exp2(s - m)
            px = jnp.exp2(sx - m)
            den = jnp.sum(p, axis=-1, keepdims=True) + jnp.sum(px, axis=-1, keepdims=True)
            o = (jnp.dot(p.astype(BF16), v_all[keys, sl], preferred_element_type=F32)
                 + jnp.dot(px.astype(BF16), vx, preferred_element_type=F32))
            o_ref[0, rows, sl] = (o / den).astype(o_ref.dtype)


def _na_attention(qkv, qkv_c, bias_tab, n_heads):
    b, l, _ = qkv.shape
    lc = qkv_c.shape[1]
    hps = NA_HEADS_PER_STEP
    wd = hps * HEAD_DIM
    tq = NA_ROWS_PER_BLOCK * GRID_W
    th = tq // 2
    nb = l // tq
    nh = l // th
    ng = n_heads // hps
    assert nb >= 2 and l % tq == 0 and n_heads % hps == 0

    def btype(i):
        return jnp.where(i == 0, 0, jnp.where(i == nb - 1, 2, 1))

    def cur(off):
        return pl.BlockSpec((1, tq, wd), lambda h, bi, i: (bi, i, off + h))

    def prev(off):
        return pl.BlockSpec((1, th, wd), lambda h, bi, i: (bi, jnp.maximum(2 * i - 1, 0), off + h))

    def nxt(off):
        return pl.BlockSpec((1, th, wd), lambda h, bi, i: (bi, jnp.minimum(2 * i + 2, nh - 1), off + h))

    def ctx(off):
        return pl.BlockSpec((1, lc, wd), lambda h, bi, i: (bi, 0, off + h))

    ko, vo = ng, 2 * ng
    return pl.pallas_call(
        _na_kernel,
        out_shape=jax.ShapeDtypeStruct((b, l, n_heads * HEAD_DIM), BF16),
        grid=(ng, b, nb),
        in_specs=[cur(0), prev(ko), cur(ko), nxt(ko), prev(vo), cur(vo), nxt(vo), ctx(ko), ctx(vo),
                  pl.BlockSpec((1, hps, tq, 2 * tq), lambda h, bi, i: (btype(i), h, 0, 0))],
        out_specs=pl.BlockSpec((1, tq, wd), lambda h, bi, i: (bi, i, h)),
        compiler_params=_cparams(("parallel", "parallel", "arbitrary")),
        name="na_attention",
    )(qkv, qkv, qkv, qkv, qkv, qkv, qkv, qkv_c, qkv_c, bias_tab)


def _ctx_attn_kernel(q_ref, k_ref, v_ref, o_ref):
    s = lax.dot_general(q_ref[0], k_ref[0], NT_DIMS, preferred_element_type=F32)
    m = jnp.max(s, axis=-1, keepdims=True)
    p = jnp.exp2(s - m)
    den = jnp.sum(p, axis=-1, keepdims=True)
    o_ref[0] = (jnp.dot(p.astype(BF16), v_ref[0], preferred_element_type=F32) / den).astype(o_ref.dtype)


def _ctx_attention(qkv_c, n_heads):
    b, lc, _ = qkv_c.shape
    hd = HEAD_DIM

    def spec(off):
        return pl.BlockSpec((1, lc, hd), lambda bi, h: (bi, 0, off + h))

    return pl.pallas_call(
        _ctx_attn_kernel,
        out_shape=jax.ShapeDtypeStruct((b, lc, n_heads * hd), F32),
        grid=(b, n_heads),
        in_specs=[spec(0), spec(n_heads), spec(2 * n_heads)],
        out_specs=spec(0),
        compiler_params=_cparams(("parallel", "parallel")),
        name="ctx_attention",
    )(qkv_c, qkv_c, qkv_c)


def _cos_sin(n_out, n_in, period):
    ang = 2.0 * np.pi * ((np.arange(n_out)[:, None] * np.arange(n_in)[None, :]) % period) / period
    return np.cos(ang), np.sin(ang)


def _lane_cat(x3):
    return jnp.concatenate([x3[j] for j in range(x3.shape[0])], axis=1)


def _rows_from_lanes(re, im, nc):
    return jnp.concatenate(
        [jnp.concatenate([re[:, j * LANES:(j + 1) * LANES], im[:, j * LANES:(j + 1) * LANES]], axis=1)
         for j in range(nc)], axis=0)


def _lanes_from_rows(x, nc, r):
    re = jnp.concatenate([x[j * r:(j + 1) * r, :LANES] for j in range(nc)], axis=1)
    im = jnp.concatenate([x[j * r:(j + 1) * r, LANES:] for j in range(nc)], axis=1)
    return jnp.concatenate([re, im], axis=0)


def _fourier_tables(r, nc):
    l = r * LANES
    c1, s1 = _cos_sin(r, r, r)
    m1 = np.block([[c1, s1], [-s1, c1]])
    tc, ts = _cos_sin(r, LANES, l)
    c2, s2 = _cos_sin(LANES, LANES, LANES)
    m2 = np.concatenate([c2, s2], axis=0) / math.sqrt(l)
    return (_const_bf16(m1), jnp.asarray(np.tile(tc, (1, nc)), F32), jnp.asarray(np.tile(ts, (1, nc)), F32),
            _const_bf16(m2))


def _fourier_kernel(zr_ref, zi_ref, m1_ref, tc_ref, ts_ref, m2_ref, o_ref, *, nc, r):
    z = jnp.concatenate([_lane_cat(zr_ref[0]), _lane_cat(zi_ref[0])], axis=0)
    a = _dot1(m1_ref[...], z)
    ar, ai = a[:r], a[r:]
    tc, ts = tc_ref[...], ts_ref[...]
    br = (ar * tc + ai * ts).astype(BF16)
    bi = (ai * tc - ar * ts).astype(BF16)
    y = _dot1(_rows_from_lanes(br, bi, nc), m2_ref[...])
    for j in range(nc):
        o_ref[0, j] = y[j * r:(j + 1) * r].T.astype(o_ref.dtype)


def _fourier_latent(pt, n_ch, nc):
    b, _, l = pt.shape
    r = l // LANES
    p4 = pt.reshape(b, pt.shape[1], r, LANES)
    m1, tc, ts, m2 = _fourier_tables(r, nc)
    nblk = n_ch // nc

    def const(a):
        return pl.BlockSpec(a.shape, lambda bi, c: (0,) * a.ndim)

    out = pl.pallas_call(
        functools.partial(_fourier_kernel, nc=nc, r=r),
        out_shape=jax.ShapeDtypeStruct((b, n_ch, LANES, r), BF16),
        grid=(b, nblk),
        in_specs=[
            pl.BlockSpec((1, nc, r, LANES), lambda bi, c: (bi, c, 0, 0)),
            pl.BlockSpec((1, nc, r, LANES), lambda bi, c: (bi, nblk + c, 0, 0)),
            const(m1), const(tc), const(ts), const(m2),
        ],
        out_specs=pl.BlockSpec((1, nc, LANES, r), lambda bi, c: (bi, c, 0, 0)),
        compiler_params=_cparams(("parallel", "parallel")),
        name="fourier_latent",
    )(p4, p4, m1, tc, ts, m2)
    return out.reshape(b, n_ch, l)


def _filter_mlp_kernel(bands_ref, w1t_ref, w1c_ref, w1s_ref, b1_ref, w2_ref, b2_ref, fr_ref, o_ref, *, l, tl):
    pos = (pl.program_id(0) * tl + lax.broadcasted_iota(jnp.int32, (1, tl), 1)).astype(F32)
    t = pos / float(max(l - 1, 1))
    ang = bands_ref[...] * (2.0 * math.pi / l) * pos
    fr = fr_ref[...]
    cos_a, sin_a = jnp.cos(ang), jnp.sin(ang)
    w1c, w1s = w1c_ref[...], w1s_ref[...]
    pre = w1t_ref[...] * t
    for k in range(HY_BANDS):
        pre = pre + w1c[:, k:k + 1] * cos_a[k:k + 1, :] - w1s[:, k:k + 1] * sin_a[k:k + 1, :]
    kpad = jnp.zeros((w2_ref.shape[1] - w2_ref.shape[0], tl), F32)
    h1 = jnp.sin(fr * (pre + b1_ref[...]))
    o_ref[...] = jnp.sin(fr * (_dot3k(w2_ref[...], jnp.concatenate([h1, kpad], axis=0)) + b2_ref[...]))


def _flip_lanes(x, anti_eye):
    nb = x.shape[1] // LANES
    hi, lo = _hi_lo(x)
    blocks = []
    for b in range(nb):
        sl = slice((nb - 1 - b) * LANES, (nb - b) * LANES)
        blocks.append(jnp.dot(hi[:, sl], anti_eye, preferred_element_type=F32)
                      + jnp.dot(lo[:, sl], anti_eye, preferred_element_type=F32))
    return jnp.concatenate(blocks, axis=1)


def _filter_out_kernel(h_ref, nb_ref, eye_ref, w3_ref, dl_ref, o_ref, *, l, tl):
    i = pl.program_id(0)
    lane = lax.broadcasted_iota(jnp.int32, (1, tl), 1)
    n = i * tl + lane
    t = jnp.where(n < l, n, 2 * l - n).astype(F32) / float(max(l - 1, 1))
    tile = h_ref[...]
    rev = pltpu.roll(_flip_lanes(tile, eye_ref[...]), 1, 1)
    rev = jnp.where(lane == 0, nb_ref[:, 0:1], rev)
    h2 = jnp.where(i >= l // tl, rev, tile)
    kpad = jnp.zeros((w3_ref.shape[2] - h2.shape[0], tl), F32)
    out = _dot1(w3_ref[0], jnp.concatenate([h2, kpad], axis=0)) * jnp.exp(-t * dl_ref[...])
    o_ref[...] = jnp.where(n == l, 0.0, out).astype(o_ref.dtype)


def _hyena_filters(l, w1, b1, w2, b2, w3, freq, width):
    hid = w1.shape[1]
    kdim = -(-hid // LANES) * LANES
    pad = kdim - hid
    tl = min(2048, l)
    rows = HY_ORDER * width
    bands = np.linspace(1e-4, HY_BANDS - 1, HY_BANDS, dtype=np.float32).reshape(HY_BANDS, 1)
    deltas = np.abs(np.linspace(HY_MIN_DECAY, HY_MAX_DECAY, width, dtype=np.float32))
    dl = np.tile(deltas, HY_ORDER).reshape(rows, 1)
    w1t = w1.T
    w2t = jnp.pad(w2.T, ((0, 0), (0, pad)))
    w3d = w3.reshape(hid, HY_ORDER, 2, width).transpose(2, 1, 3, 0).reshape(2, rows, hid)
    w3d = jnp.pad(w3d, ((0, 0), (0, 0), (0, pad)))
    col = lambda v: v.reshape(hid, 1)

    def const(shape):
        return pl.BlockSpec(shape, lambda i: (0,) * len(shape))

    hidden = pl.pallas_call(
        functools.partial(_filter_mlp_kernel, l=l, tl=tl),
        out_shape=jax.ShapeDtypeStruct((hid, l), F32),
        grid=(l // tl,),
        in_specs=[const((HY_BANDS, 1)), const((hid, 1)), const((hid, HY_BANDS)), const((hid, HY_BANDS)),
                  const((hid, 1)), const((hid, kdim)), const((hid, 1)), const((hid, 1))],
        out_specs=pl.BlockSpec((hid, tl), lambda i: (0, i)),
        compiler_params=_cparams(("parallel",)),
        name="hyena_filter_mlp",
    )(jnp.asarray(bands), w1t[:, 0:1], w1t[:, 1:1 + HY_BANDS], w1t[:, 1 + HY_BANDS:], col(b1), w2t, col(b2),
      col(freq))
    nt = l // tl
    anti_eye = _const_bf16(np.eye(LANES)[::-1])
    return pl.pallas_call(
        functools.partial(_filter_out_kernel, l=l, tl=tl),
        out_shape=jax.ShapeDtypeStruct((rows, 2 * l), BF16),
        grid=(2 * nt,),
        in_specs=[pl.BlockSpec((hid, tl), lambda i: (0, jnp.where(i < nt, i, 2 * nt - 1 - i))),
                  pl.BlockSpec((hid, tl), lambda i: (0, jnp.clip(2 * nt - i, 0, nt - 1))),
                  const((LANES, LANES)),
                  pl.BlockSpec((1, rows, kdim), lambda i: (i // nt, 0, 0)),
                  const((rows, 1))],
        out_specs=pl.BlockSpec((rows, tl), lambda i: (0, i)),
        compiler_params=_cparams(("parallel",)),
        name="hyena_filters",
    )(hidden, hidden, anti_eye, w3d, jnp.asarray(dl))


def _conv_tables(r, nc):
    r2 = 2 * r
    n = r2 * LANES
    c1, s1 = _cos_sin(r2, r2, r2)
    c1h, s1h = c1[:, :r], s1[:, :r]
    m1 = np.block([[c1h, s1h], [-s1h, c1h]])
    m1_real = np.concatenate([c1, -s1], axis=0)
    tc, ts = _cos_sin(r2, LANES, n)
    c2, s2 = _cos_sin(LANES, LANES, LANES)
    m2 = np.block([[c2, -s2], [s2, c2]])
    m2i = np.block([[c2, s2], [-s2, c2]])
    ct, st = c1h.T, s1h.T
    m1i = np.block([[ct, -st], [st, ct]]) / n
    return dict(
        m1=_const_bf16(m1), m1_real=_const_bf16(m1_real), m2=_const_bf16(m2), m2i=_const_bf16(m2i),
        m1i=_const_bf16(m1i),
        tc_l=_const_bf16(np.tile(tc, (1, nc))), ts_l=_const_bf16(np.tile(ts, (1, nc))),
        tc_r=_const_bf16(np.tile(tc, (nc, 1))), ts_r=_const_bf16(np.tile(ts, (nc, 1))))


def _kernel_spectrum(k3, m1_real, tc_l, ts_l, m2, nc, r2):
    a = _dot1(m1_real, _lane_cat(k3)).astype(BF16)
    ar, ai = a[:r2], a[r2:]
    return _dot1(_rows_from_lanes(ar * tc_l + ai * ts_l, ai * tc_l - ar * ts_l, nc), m2).astype(BF16)


def _shift_tokens(t, r, direction):
    lane = lax.broadcasted_iota(jnp.int32, t.shape, 2)
    row = lax.broadcasted_iota(jnp.int32, t.shape, 1)
    if direction < 0:
        near = pltpu.roll(t, 1, 2)
        wrap = pltpu.roll(near, 1, 1)
        edge = lane == 0
        dead = edge & (row == 0)
    else:
        near = pltpu.roll(t, LANES - 1, 2)
        wrap = pltpu.roll(near, r - 1, 1)
        edge = lane == LANES - 1
        dead = edge & (row == r - 1)
    return jnp.where(dead, 0.0, jnp.where(edge, wrap, near))


def _long_conv(zr, zi, kspec, m1, m2, m2i, m1i, tc_l, ts_l, tc_r, ts_r, nc, r):
    r2 = 2 * r
    a = _dot1(m1, jnp.concatenate([zr.astype(BF16), zi.astype(BF16)], axis=0)).astype(BF16)
    ar, ai = a[:r2], a[r2:]
    x = _dot1(_rows_from_lanes(ar * tc_l + ai * ts_l, ai * tc_l - ar * ts_l, nc), m2).astype(BF16)
    xr, xi = x[:, :LANES], x[:, LANES:]
    kr, ki = kspec[:, :LANES], kspec[:, LANES:]
    bb = _dot1(jnp.concatenate([xr * kr - xi * ki, xr * ki + xi * kr], axis=1), m2i).astype(BF16)
    pr, pi = bb[:, :LANES], bb[:, LANES:]
    q = jnp.concatenate([pr * tc_r - pi * ts_r, pi * tc_r + pr * ts_r], axis=1)
    out = _dot1(m1i, _lanes_from_rows(q, nc, r2))
    return out[:r], out[r:]


def _hyena_kernel(v0, v1, a0, a1, b0, b1, k0_ref, k1_ref, cw_ref, cb_ref, bias_ref,
                  m1_ref, m2_ref, m2i_ref, m1i_ref, tcl_ref, tsl_ref, tcr_ref, tsr_ref, m1r_ref, o_ref, *, nc, r):
    tabs = (m1_ref[...], m2_ref[...], m2i_ref[...], m1i_ref[...],
            tcl_ref[...], tsl_ref[...], tcr_ref[...], tsr_ref[...])

    def spectrum(k_ref):
        return _kernel_spectrum(k_ref[...], m1r_ref[...], tcl_ref[...], tsl_ref[...], m2_ref[...], nc, 2 * r)

    def short_conv(ref, part):
        t = ref[0].astype(F32)
        w = cw_ref[part]
        return (_lane_cat(_shift_tokens(t, r, -1)) * w[0:1] + _lane_cat(t) * w[1:2]
                + _lane_cat(_shift_tokens(t, r, +1)) * w[2:3] + cb_ref[part])

    vr, vi = short_conv(v0, 0), short_conv(v1, 0)
    x1r, x1i = short_conv(a0, 1), short_conv(a1, 1)
    x2r, x2i = short_conv(b0, 2), short_conv(b1, 2)
    bias = bias_ref[...]
    yr, yi = _long_conv(vr, vi, spectrum(k0_ref), *tabs, nc, r)
    zr = x1r * (yr + vr * bias[0:1])
    zi = x1i * (yi + vi * bias[0:1])
    yr, yi = _long_conv(zr, zi, spectrum(k1_ref), *tabs, nc, r)
    outr = x2r * (yr + zr * bias[1:2])
    outi = x2i * (yi + zi * bias[1:2])
    for j in range(nc):
        o_ref[0, j] = outr[:, j * LANES:(j + 1) * LANES].astype(o_ref.dtype)
        o_ref[1, j] = outi[:, j * LANES:(j + 1) * LANES].astype(o_ref.dtype)


def _hyena_latent(pt, ch0, width, k2t, conv_w, conv_b, bias, tabs, nc):
    b, c_all, l = pt.shape
    assert b == 2, "the two batch entries ride one complex transform"
    r = l // LANES
    r2 = 2 * r
    p4 = pt.reshape(b, c_all, r, LANES)
    k3 = k2t.reshape(k2t.shape[0], r2, LANES)
    nblk = width // nc
    rep = lambda a: jnp.repeat(a, LANES, axis=-1)
    cw = rep(conv_w.reshape(3, 3, width).transpose(1, 0, 2))
    cb = rep(conv_b.reshape(3, 1, width))
    bs = rep(bias)

    def inp(bi, part):
        off = (ch0 + part * width) // nc
        return pl.BlockSpec((1, nc, r, LANES), lambda c: (bi, off + c, 0, 0))

    def const(a):
        return pl.BlockSpec(a.shape, lambda c: (0,) * a.ndim)

    names = ("m1", "m2", "m2i", "m1i", "tc_l", "ts_l", "tc_r", "ts_r", "m1_real")
    consts = [tabs[k] for k in names]
    out = pl.pallas_call(
        functools.partial(_hyena_kernel, nc=nc, r=r),
        out_shape=jax.ShapeDtypeStruct((b, width, r, LANES), BF16),
        grid=(nblk,),
        in_specs=[inp(0, 0), inp(1, 0), inp(0, 1), inp(1, 1), inp(0, 2), inp(1, 2),
                  pl.BlockSpec((nc, r2, LANES), lambda c: (c, 0, 0)),
                  pl.BlockSpec((nc, r2, LANES), lambda c: (nblk + c, 0, 0)),
                  pl.BlockSpec((3, 3, nc * LANES), lambda c: (0, 0, c)),
                  pl.BlockSpec((3, 1, nc * LANES), lambda c: (0, 0, c)),
                  pl.BlockSpec((HY_ORDER, nc * LANES), lambda c: (0, c))]
                 + [const(a) for a in consts],
        out_specs=pl.BlockSpec((b, nc, r, LANES), lambda c: (0, c, 0, 0)),
        compiler_params=_cparams(("parallel",)),
        name="hyena_latent",
    )(p4, p4, p4, p4, p4, p4, k3, k3, cw, cb, bs, *consts)
    return out.reshape(b, width, l)


def _ctx_fourier_kernel(zr_ref, zi_ref, m_ref, o_ref):
    z = jnp.concatenate([zr_ref[0], zi_ref[0]], axis=1).astype(F32)
    o_ref[0] = _dot_data_const(z, m_ref[...])


def _ctx_fourier(ptc, n_ch):
    b, _, lc = ptc.shape
    c, s = _cos_sin(lc, lc, lc)
    m = _const_rhs3(np.concatenate([c, s], axis=0) / math.sqrt(lc))
    return pl.pallas_call(
        _ctx_fourier_kernel,
        out_shape=jax.ShapeDtypeStruct((b, n_ch, lc), F32),
        grid=(b,),
        in_specs=[pl.BlockSpec((1, n_ch, lc), lambda bi: (bi, 0, 0)),
                  pl.BlockSpec((1, n_ch, lc), lambda bi: (bi, 1, 0)),
                  pl.BlockSpec(m.shape, lambda bi: (0, 0))],
        out_specs=pl.BlockSpec((1, n_ch, lc), lambda bi: (bi, 0, 0)),
        compiler_params=_cparams(("parallel",)),
        name="ctx_fourier",
    )(ptc, ptc, m)


def _ctx_hyena_kernel(v0, v1, a0, a1, b0, b1, k_ref, cw_ref, cb_ref, bias_ref, mk_ref, mf_ref, mi_ref,
                      o_ref, *, lc, width):
    def short_conv(ref, part):
        t = ref[0].astype(F32)
        lane = lax.broadcasted_iota(jnp.int32, t.shape, 1)
        prv = jnp.where(lane == 0, 0.0, pltpu.roll(t, 1, 1))
        nxt = jnp.where(lane == lc - 1, 0.0, pltpu.roll(t, lc - 1, 1))
        w = cw_ref[part]
        return prv * w[:, 0:1] + t * w[:, 1:2] + nxt * w[:, 2:3] + cb_ref[part]

    kspec = _dot_data_const(k_ref[...].astype(F32), mk_ref[...])
    n = 2 * lc

    def long_conv(zr, zi, ks):
        x = _dot_data_const(jnp.concatenate([zr, zi], axis=1), mf_ref[...])
        xr, xi = x[:, :n], x[:, n:]
        kr, ki = ks[:, :n], ks[:, n:]
        y = jnp.concatenate([xr * kr - xi * ki, xr * ki + xi * kr], axis=1)
        out = _dot_data_const(y, mi_ref[...])
        return out[:, :lc], out[:, lc:]

    vr, vi = short_conv(v0, 0), short_conv(v1, 0)
    x1r, x1i = short_conv(a0, 1), short_conv(a1, 1)
    x2r, x2i = short_conv(b0, 2), short_conv(b1, 2)
    bias = bias_ref[...]
    yr, yi = long_conv(vr, vi, kspec[:width])
    zr = x1r * (yr + vr * bias[:, 0:1])
    zi = x1i * (yi + vi * bias[:, 0:1])
    yr, yi = long_conv(zr, zi, kspec[width:])
    o_ref[0] = x2r * (yr + zr * bias[:, 1:2])
    o_ref[1] = x2i * (yi + zi * bias[:, 1:2])


def _ctx_hyena(ptc, ch0, width, k2t, conv_w, conv_b, bias):
    b, _, lc = ptc.shape
    assert b == 2
    n = 2 * lc
    c, s = _cos_sin(n, n, n)
    mk = _const_rhs3(np.concatenate([c, -s], axis=1))
    ch, sh = c[:lc], s[:lc]
    mf = _const_rhs3(np.block([[ch, -sh], [sh, ch]]))
    ci, si = c[:, :lc], s[:, :lc]
    mi = _const_rhs3(np.block([[ci, si], [-si, ci]]) / n)
    cw = conv_w.reshape(3, 3, width).transpose(1, 2, 0)
    cb = conv_b.reshape(3, width, 1)

    def inp(bi, part):
        return pl.BlockSpec((1, width, lc), lambda i: (bi, ch0 // width + part, 0))

    def const(a):
        return pl.BlockSpec(a.shape, lambda i: (0,) * a.ndim)

    args = (k2t, cw, cb, bias.T, mk, mf, mi)
    return pl.pallas_call(
        functools.partial(_ctx_hyena_kernel, lc=lc, width=width),
        out_shape=jax.ShapeDtypeStruct((b, width, lc), F32),
        grid=(1,),
        in_specs=[inp(0, 0), inp(1, 0), inp(0, 1), inp(1, 1), inp(0, 2), inp(1, 2)] + [const(a) for a in args],
        out_specs=pl.BlockSpec((b, width, lc), lambda i: (0, 0, 0)),
        compiler_params=_cparams(("arbitrary",)),
        name="ctx_hyena",
    )(ptc, ptc, ptc, ptc, ptc, ptc, *args)


def _merge_kernel(a_ref, f_ref, hy_ref, x_ref, gate_ref, ga_ref, gf_ref, gh_ref, wa_ref, wf_ref, wh_ref, o_ref):
    a = a_ref[0].astype(F32)
    ya = a * lax.rsqrt(jnp.mean(a * a, axis=-1, keepdims=True) + EPS) * ga_ref[...]
    acc = jnp.dot(ya.astype(BF16), wa_ref[...], preferred_element_type=F32)

    def cm_part(ref, g_ref, w_ref):
        t = ref[0].astype(F32)
        y = t * lax.rsqrt(jnp.mean(t * t, axis=0, keepdims=True) + EPS) * g_ref[...]
        return jnp.dot(y.T.astype(BF16), w_ref[...], preferred_element_type=F32)

    acc = acc + cm_part(f_ref, gf_ref, wf_ref) + cm_part(hy_ref, gh_ref, wh_ref)
    o_ref[0] = x_ref[0] + gate_ref[0] * acc


def _merge_out(a, ft, ht, x, gate, g, w_all, layer, tm):
    b, l, d = x.shape
    wa_n, wf_n, wh_n = a.shape[2], ft.shape[1], ht.shape[1]
    assert wa_n % wf_n == 0 and wf_n == wh_n
    tm = min(tm, l)
    ga = g[:wa_n].reshape(1, wa_n)
    gf = g[wa_n:wa_n + wf_n].reshape(wf_n, 1)
    gh = g[wa_n + wf_n:].reshape(wh_n, 1)

    def const(arr):
        return pl.BlockSpec(arr.shape, lambda bi, i: (0,) * arr.ndim)

    def w_rows(n, blk):
        return pl.BlockSpec((None, n, d), lambda bi, i: (layer, blk, 0))

    return pl.pallas_call(
        _merge_kernel,
        out_shape=jax.ShapeDtypeStruct((b, l, d), F32),
        grid=(b, l // tm),
        in_specs=[pl.BlockSpec((1, tm, wa_n), lambda bi, i: (bi, i, 0)),
                  pl.BlockSpec((1, wf_n, tm), lambda bi, i: (bi, 0, i)),
                  pl.BlockSpec((1, wh_n, tm), lambda bi, i: (bi, 0, i)),
                  pl.BlockSpec((1, tm, d), lambda bi, i: (bi, i, 0)),
                  pl.BlockSpec((1, 1, d), lambda bi, i: (bi, 0, 0)),
                  const(ga), const(gf), const(gh),
                  w_rows(wa_n, 0), w_rows(wf_n, wa_n // wf_n), w_rows(wh_n, wa_n // wf_n + 1)],
        out_specs=pl.BlockSpec((1, tm, d), lambda bi, i: (bi, i, 0)),
        compiler_params=_cparams(("parallel", "parallel")),
        name="merge_out",
    )(a, ft, ht, x, gate.reshape(b, 1, d), ga, gf, gh, w_all, w_all, w_all)


HALO = 16


def _ffn_kernel(xp_ref, x_ref, xn_ref, g_ref, sh_ref, sc_ref, gate_ref, wg_ref, wu_ref, cw_ref, cb_ref, wd_ref,
                fg_ref, o_ref, h_ref, *, tm, final_norm):
    i = pl.program_id(1)
    c = pl.program_id(2)
    last_tile = pl.num_programs(1) - 1

    n_ext = tm + 2 * HALO

    def down_partial(ge, up):
        row = lax.broadcasted_iota(jnp.int32, ge.shape, 0)
        outside = ((row < HALO) & (i == 0)) | ((row >= HALO + tm) & (i == last_tile))
        ge = jnp.where(outside, 0.0, ge)
        gp = pltpu.roll(ge, 1, 0)[HALO:HALO + tm]
        gn = pltpu.roll(ge, n_ext - 1, 0)[HALO:HALO + tm]
        cw = cw_ref[...]
        conv = gp * cw[0:1] + ge[HALO:HALO + tm] * cw[1:2] + gn * cw[2:3] + cb_ref[...]
        inner = 0.7978845608028654 * (conv + 0.044715 * (conv * conv * conv))
        act = 0.5 * conv * (1.0 + jnp.tanh(inner)) * up
        return jnp.dot(act.astype(BF16), wd_ref[...], preferred_element_type=F32)

    @pl.when(c == 0)
    def _():
        g, sh, sc = g_ref[...], sh_ref[0], sc_ref[0]
        wg, wu = wg_ref[...], wu_ref[...]
        rs = tm // NORM_SLABS
        bounds = [(0, HALO)] + [(HALO + s * rs, HALO + (s + 1) * rs) for s in range(NORM_SLABS)] + [(HALO + tm, n_ext)]
        ge_parts, up_parts = [], []
        for lo, hi in bounds:
            if lo == 0:
                xs = xp_ref[0, 0]
            elif hi == n_ext:
                xs = xn_ref[0, 0]
            else:
                xs = x_ref[0, lo - HALO:hi - HALO, :]
            h = _norm_mod(xs, g, sh, sc).astype(BF16)
            h_ref[lo:hi, :] = h
            ge_parts.append(jnp.dot(h, wg, preferred_element_type=F32))
            if lo != 0 and hi != n_ext:
                up_parts.append(jnp.dot(h, wu, preferred_element_type=F32))
        o_ref[0] = down_partial(jnp.concatenate(ge_parts, axis=0), jnp.concatenate(up_parts, axis=0))

    @pl.when(c > 0)
    def _():
        h = h_ref[...]
        ge = jnp.dot(h, wg_ref[...], preferred_element_type=F32)
        up = jnp.dot(h[HALO:HALO + tm], wu_ref[...], preferred_element_type=F32)
        o_ref[0] += down_partial(ge, up)

    @pl.when(c == pl.num_programs(2) - 1)
    def _():
        y = x_ref[0] + gate_ref[0] * o_ref[0]
        if final_norm:
            y = y * lax.rsqrt(jnp.mean(y * y, axis=-1, keepdims=True) + EPS) * fg_ref[...]
        o_ref[0] = y


def _ffn(x, g, sh, sc, gate, w_up, conv_w, conv_b, w_down, layer, final_g, tm, tf, *, final_norm):
    b, l, d = x.shape
    dff = w_down.shape[1]
    tm = min(tm, l)
    nch = dff // tf
    x4 = x.reshape(b, l // HALO, HALO, d)
    per_tile = tm // HALO
    nhalo = l // HALO
    vec = pl.BlockSpec((1, 1, d), lambda bi, i, c: (bi, 0, 0))
    return pl.pallas_call(
        functools.partial(_ffn_kernel, tm=tm, final_norm=final_norm),
        out_shape=jax.ShapeDtypeStruct((b, l, d), F32),
        grid=(b, l // tm, nch),
        in_specs=[
            pl.BlockSpec((1, 1, HALO, d), lambda bi, i, c: (bi, jnp.maximum(i * per_tile - 1, 0), 0, 0)),
            pl.BlockSpec((1, tm, d), lambda bi, i, c: (bi, i, 0)),
            pl.BlockSpec((1, 1, HALO, d), lambda bi, i, c: (bi, jnp.minimum((i + 1) * per_tile, nhalo - 1), 0, 0)),
            pl.BlockSpec((1, d), lambda bi, i, c: (0, 0)),
            vec, vec, vec,
            pl.BlockSpec((None, d, tf), lambda bi, i, c: (layer, 0, c)),
            pl.BlockSpec((None, d, tf), lambda bi, i, c: (layer, 0, nch + c)),
            pl.BlockSpec((3, tf), lambda bi, i, c: (0, c)),
            pl.BlockSpec((1, tf), lambda bi, i, c: (0, c)),
            pl.BlockSpec((None, tf, d), lambda bi, i, c: (layer, c, 0)),
            pl.BlockSpec((1, d), lambda bi, i, c: (0, 0)),
        ],
        out_specs=pl.BlockSpec((1, tm, d), lambda bi, i, c: (bi, i, 0)),
        scratch_shapes=[pltpu.VMEM((tm + 2 * HALO, d), BF16)],
        compiler_params=_cparams(("parallel", "parallel", "arbitrary")),
        name="ffn",
    )(x4, x, x4, g.reshape(1, d), sh.reshape(b, 1, d), sc.reshape(b, 1, d), gate.reshape(b, 1, d),
      w_up, w_up, conv_w, conv_b.reshape(1, dff), w_down, final_g.reshape(1, d))


TOKEN_TILE = 512
PROJ_TOKEN_TILE = 1024
PROJ_TOK_COL_TILE = 1536
PROJ_CM_COL_TILE = 1280
FFN_TOKEN_TILE = 1024
FFN_COL_TILE = 512
MIX_CHANNELS = 8
FOURIER_CHANNELS = 32


def kernel(x, c, ctx, c_ctx, ada_w, ada_b, norm1_g, norm2_g, w_in, na_rpb, hy_conv_w, hy_conv_b, hy_w1, hy_b1,
           hy_w2, hy_b2, hy_w3, hy_freq, hy_bias, mix_norm_g, w_out, ffn_w_up, ffn_conv_w, ffn_conv_b, ffn_w_down,
           final_norm_g):
    b, l, d = x.shape
    lc = ctx.shape[1]
    depth = ada_w.shape[0]
    na_w, fn_w = d // 2, d // 4
    hy_w = d - na_w - fn_w
    n_heads = na_w // HEAD_DIM
    qkv_w = 3 * na_w
    rows = l // GRID_W
    assert b + 1 <= 8 and l % (NA_ROWS_PER_BLOCK * GRID_W) == 0 and rows >= 2 * NA_ROWS_PER_BLOCK
    r = l // LANES
    nc = MIX_CHANNELS
    conv_tabs = _conv_tables(r, nc)

    s_in = jnp.concatenate([c, c_ctx[None], jnp.zeros((8 - b - 1, d), F32)], axis=0)
    mod_all = _ada_mod(s_in, ada_w, ada_b)
    xc = ctx
    w_out_bf, w_up_bf, w_down_bf = w_out.astype(BF16), ffn_w_up.astype(BF16), ffn_w_down.astype(BF16)
    for layer in range(depth):
        update_ctx = layer < depth - 1
        mod = mod_all[layer]
        sh1, sc1, g1, sh2, sc2, g2 = jnp.split(mod[:b], N_MOD, axis=-1)
        csh1, csc1, cg1, csh2, csc2, cg2 = jnp.split(jnp.broadcast_to(mod[b:b + 1], (b, N_MOD * d)), N_MOD, axis=-1)

        wl = w_in[layer]
        w_fold = _fold_fourier_weights(wl[:, qkv_w:qkv_w + fn_w])
        w_tok = jnp.concatenate([wl[:, :na_w] * (HEAD_DIM ** -0.5 * LOG2E), wl[:, na_w:qkv_w]], axis=1).astype(BF16)
        w_cm = jnp.concatenate([w_fold[0], w_fold[1], wl[:, qkv_w + fn_w:]], axis=1).astype(BF16)
        hy0 = 2 * fn_w

        tiles = dict(tm=PROJ_TOKEN_TILE, tn_tok=PROJ_TOK_COL_TILE, tn_cm=PROJ_CM_COL_TILE)
        qkv, pt = _proj(x, norm1_g[layer], sh1, sc1, w_tok, w_cm, **tiles)
        qkv_c, ptc = _proj(xc, norm1_g[layer], csh1, csc1, w_tok, w_cm, **tiles)

        a = _na_attention(qkv, qkv_c, _na_bias_table(na_rpb[layer], rows), n_heads)
        yf = _fourier_latent(pt, fn_w, FOURIER_CHANNELS)
        filt = (hy_w1[layer], hy_b1[layer], hy_w2[layer], hy_b2[layer], hy_w3[layer], hy_freq[layer])
        yh = _hyena_latent(pt, hy0, hy_w, _hyena_filters(l, *filt, hy_w), hy_conv_w[layer], hy_conv_b[layer],
                           hy_bias[layer], conv_tabs, nc)
        x_new = _merge_out(a, yf, yh, x, g1, mix_norm_g[layer], w_out_bf, layer, TOKEN_TILE)

        if update_ctx:
            ac = _ctx_attention(qkv_c, n_heads)
            yfc = _ctx_fourier(ptc, fn_w)
            yhc = _ctx_hyena(ptc, hy0, hy_w, _hyena_filters(lc, *filt, hy_w), hy_conv_w[layer], hy_conv_b[layer],
                             hy_bias[layer])
            xc = _merge_out(ac, yfc, yhc, xc, cg1, mix_norm_g[layer], w_out_bf, layer, TOKEN_TILE)
        x = x_new

        x = _ffn(x, norm2_g[layer], sh2, sc2, g2, w_up_bf, ffn_conv_w[layer], ffn_conv_b[layer], w_down_bf, layer,
                 final_norm_g, FFN_TOKEN_TILE, FFN_COL_TILE, final_norm=not update_ctx)
        if update_ctx:
            xc = _ffn(xc, norm2_g[layer], csh2, csc2, cg2, w_up_bf, ffn_conv_w[layer], ffn_conv_b[layer], w_down_bf,
                      layer, final_norm_g, FFN_TOKEN_TILE, FFN_COL_TILE, final_norm=False)
    return x
```
